```python
import jax, jax.numpy as jnp
from jax import lax
import numpy as np

D_MODEL = 2048
BATCH = 1
SEQ = 16384
DEPTH = 2

N_META = 16
N_A_LAYERS = DEPTH // 2
N_B_LAYERS = DEPTH - N_A_LAYERS
GLA_HEADS = 4
GLA_QK = D_MODEL // 2
GLA_VD = D_MODEL
GLA_DK = GLA_QK // GLA_HEADS
GLA_DV = GLA_VD // GLA_HEADS
GLA_RANK = 16
GLA_GATE_NORM = 16.0
GLA_CHUNK = 64
GLA_IN = 2 * GLA_QK + 2 * GLA_VD + GLA_RANK
SWA_HEAD_DIM = 64
SWA_Q_HEADS = D_MODEL // SWA_HEAD_DIM
SWA_KV_HEADS = 4
SWA_GROUP = SWA_Q_HEADS // SWA_KV_HEADS
SWA_WINDOW = 128
SWA_BLOCK = 128
ROPE_THETA = 10000.0
D_FF = 4 * D_MODEL
RMS_EPS = 1e-6
NEG_INF = -1e30

kernel_name = 'hybrid_gla_swa_sink_yoco_meta'


def rmsnorm(x, g):
    xf = x.astype(jnp.float32)
    y = xf * lax.rsqrt(jnp.mean(xf * xf, axis=-1, keepdims=True) + RMS_EPS)
    return (y * g.astype(jnp.float32)).astype(x.dtype)


def rope(x, pos):
    half = x.shape[-1] // 2
    inv_freq = ROPE_THETA ** (-jnp.arange(half, dtype=jnp.float32) / half)
    ang = pos.astype(jnp.float32)[:, None] * inv_freq[None, :]
    cos = jnp.cos(ang)[None, :, None, :].astype(x.dtype)
    sin = jnp.sin(ang)[None, :, None, :].astype(x.dtype)
    x1, x2 = x[..., :half], x[..., half:]
    return jnp.concatenate([x1 * cos - x2 * sin, x2 * cos + x1 * sin], axis=-1)


def sq_relu_mlp(xn, w_up, w_down):
    return jnp.square(jax.nn.relu(xn @ w_up)) @ w_down


def gla_mixer(xn, w_in, w_gate_up, b_gate, norm_out, w_out):
    B, L, _ = xn.shape
    pad = GLA_CHUNK - N_META
    nc = (L + pad) // GLA_CHUNK
    q, k, v, g, r = jnp.split(xn @ w_in, [GLA_QK, 2 * GLA_QK, 2 * GLA_QK + GLA_VD, 2 * GLA_QK + 2 * GLA_VD], axis=-1)
    gk = jax.nn.log_sigmoid((r @ w_gate_up + b_gate).astype(jnp.float32)) / GLA_GATE_NORM

    def chunks(t, d):
        t = jnp.pad(t, ((0, 0), (pad, 0), (0, 0)))
        return t.reshape(B, nc, GLA_CHUNK, GLA_HEADS, d).transpose(1, 0, 3, 2, 4)

    qc = chunks(q * (GLA_DK ** -0.5), GLA_DK)
    kc = chunks(k, GLA_DK)
    vc = chunks(v, GLA_DV)
    gc = chunks(gk, GLA_DK)
    causal = jnp.tril(jnp.ones((GLA_CHUNK, GLA_CHUNK), dtype=bool))[:, :, None]

    def step(S, inp):
        q_c, k_c, v_c, g_c = inp
        b = jnp.cumsum(g_c, axis=2)
        diff = b[:, :, :, None, :] - b[:, :, None, :, :]
        decay = jnp.exp(jnp.where(causal, diff, -jnp.inf))
        attn = jnp.einsum('bhid,bhjd,bhijd->bhij', q_c, k_c, decay)
        o_intra = jnp.einsum('bhij,bhjv->bhiv', attn, v_c)
        o_inter = jnp.einsum('bhid,bhdv->bhiv', q_c * jnp.exp(b), S)
        b_last = b[:, :, -1:, :]
        k_dec = k_c * jnp.exp(b_last - b)
        S_new = S * jnp.exp(b_last[:, :, 0, :])[..., None] + jnp.einsum('bhjd,bhjv->bhdv', k_dec, v_c)
        return S_new, o_intra + o_inter

    S0 = jnp.zeros((B, GLA_HEADS, GLA_DK, GLA_DV), jnp.float32)
    _, o = lax.scan(step, S0, (qc, kc, vc, gc))
    o = o.transpose(1, 0, 3, 2, 4).reshape(B, nc * GLA_CHUNK, GLA_HEADS, GLA_DV)[:, pad:]
    o = o.astype(jnp.float32)
    o = o * lax.rsqrt(jnp.mean(o * o, axis=-1, keepdims=True) + RMS_EPS) * norm_out.astype(jnp.float32)
    o = o * jax.nn.silu(g.reshape(B, L, GLA_HEADS, GLA_DV).astype(jnp.float32))
    return o.reshape(B, L, GLA_VD).astype(xn.dtype) @ w_out


def shared_kv(h, kv_norm, w_kv, pos):
    B, L, _ = h.shape
    k, v = jnp.split(rmsnorm(h, kv_norm) @ w_kv, 2, axis=-1)
    k = rope(k.reshape(B, L, SWA_KV_HEADS, SWA_HEAD_DIM), pos)
    v = v.reshape(B, L, SWA_KV_HEADS, SWA_HEAD_DIM)
    return k, v


def swa_mixer(xn, k_sh, v_sh, w_q, sinks, w_out, pos):
    B, L, _ = xn.shape
    q_pad = SWA_BLOCK - N_META
    k_pad = q_pad + SWA_BLOCK
    nb = (L + q_pad) // SWA_BLOCK
    q = rope((xn @ w_q).reshape(B, L, SWA_Q_HEADS, SWA_HEAD_DIM), pos) * (SWA_HEAD_DIM ** -0.5)
    qb = jnp.pad(q, ((0, 0), (q_pad, 0), (0, 0), (0, 0)))
    qb = qb.reshape(B, nb, SWA_BLOCK, SWA_KV_HEADS, SWA_GROUP, SWA_HEAD_DIM).transpose(1, 0, 3, 4, 2, 5)

    def band(t):
        tp = jnp.pad(t, ((0, 0), (k_pad, 0), (0, 0), (0, 0)))
        tb = tp.reshape(B, nb + 1, SWA_BLOCK, SWA_KV_HEADS, SWA_HEAD_DIM)
        tb = jnp.concatenate([tb[:, :-1], tb[:, 1:]], axis=2)
        return tb.transpose(1, 0, 3, 2, 4)

    kb, vb = band(k_sh), band(v_sh)
    k_meta = k_sh[:, :N_META].transpose(0, 2, 1, 3)
    v_meta = v_sh[:, :N_META].transpose(0, 2, 1, 3)
    sink = sinks.astype(jnp.float32).reshape(1, SWA_KV_HEADS, SWA_GROUP, 1, 1)
    i_idx = jnp.arange(SWA_BLOCK)
    j_idx = jnp.arange(2 * SWA_BLOCK)
    m_idx = jnp.arange(N_META)

    def block(inp):
        n, q_n, k_n, v_n = inp
        q_pos = n * SWA_BLOCK - q_pad + i_idx
        k_pos = n * SWA_BLOCK - k_pad + j_idx
        rel = q_pos[:, None] - k_pos[None, :]
        band_ok = (k_pos[None, :] >= N_META) & (rel >= 0) & (rel < SWA_WINDOW)
        meta_ok = m_idx[None, :] <= q_pos[:, None]
        ok = jnp.concatenate([meta_ok, band_ok], axis=1)
        keys = jnp.concatenate([k_meta, k_n], axis=2)
        vals = jnp.concatenate([v_meta, v_n], axis=2)
        s = jnp.einsum('bkgid,bkjd->bkgij', q_n, keys).astype(jnp.float32)
        s = jnp.where(ok, s, NEG_INF)
        s = jnp.concatenate([s, jnp.broadcast_to(sink, s.shape[:-1] + (1,))], axis=-1)
        p = jax.nn.softmax(s, axis=-1)[..., :-1].astype(vals.dtype)
        return jnp.einsum('bkgij,bkjd->bkgid', p, vals)

    o = lax.map(block, (jnp.arange(nb), qb, kb, vb))
    o = o.transpose(1, 0, 4, 2, 3, 5).reshape(B, nb * SWA_BLOCK, SWA_Q_HEADS * SWA_HEAD_DIM)[:, q_pad:]
    return o @ w_out


def setup_inputs(seed: int = 0) -> dict:
    key = jax.random.key(seed)
    ks = jax.random.split(key, 18)
    f32 = jnp.float32

    def w(k, shape, fan_in):
        return jax.random.normal(k, shape, f32) * (fan_in ** -0.5)

    def gain(k, shape):
        return 1.0 + 0.05 * jax.random.normal(k, shape, f32)

    return {
        'x': jax.random.normal(ks[0], (BATCH, SEQ, D_MODEL), f32),
        'meta_tokens': jax.random.normal(ks[1], (N_META, D_MODEL), f32),
        'norm_mix': gain(ks[2], (DEPTH, D_MODEL)),
        'norm_mlp': gain(ks[3], (DEPTH, D_MODEL)),
        'w_mlp_up': w(ks[4], (DEPTH, D_MODEL, D_FF), D_MODEL),
        'w_mlp_down': w(ks[5], (DEPTH, D_FF, D_MODEL), D_FF),
        'a_w_in': w(ks[6], (N_A_LAYERS, D_MODEL, GLA_IN), D_MODEL),
        'a_w_gate_up': w(ks[7], (N_A_LAYERS, GLA_RANK, GLA_QK), GLA_RANK),
        'a_b_gate': 0.1 * jax.random.normal(ks[8], (N_A_LAYERS, GLA_QK), f32),
        'a_norm_out': gain(ks[9], (N_A_LAYERS, GLA_DV)),
        'a_w_out': w(ks[10], (N_A_LAYERS, GLA_VD, D_MODEL), GLA_VD),
        'kv_norm': gain(ks[11], (D_MODEL,)),
        'w_kv': w(ks[12], (D_MODEL, 2 * SWA_KV_HEADS * SWA_HEAD_DIM), D_MODEL),
        'b_w_q': w(ks[13], (N_B_LAYERS, D_MODEL, SWA_Q_HEADS * SWA_HEAD_DIM), D_MODEL),
        'b_sinks': 0.5 * jax.random.normal(ks[14], (N_B_LAYERS, SWA_Q_HEADS), f32),
        'b_w_out': w(ks[15], (N_B_LAYERS, SWA_Q_HEADS * SWA_HEAD_DIM, D_MODEL), SWA_Q_HEADS * SWA_HEAD_DIM),
        'norm_final': gain(ks[16], (D_MODEL,)),
    }


def reference(x, meta_tokens, norm_mix, norm_mlp, w_mlp_up, w_mlp_down, a_w_in, a_w_gate_up, a_b_gate,
              a_norm_out, a_w_out, kv_norm, w_kv, b_w_q, b_sinks, b_w_out, norm_final):
    B = x.shape[0]
    meta = jnp.broadcast_to(meta_tokens[None].astype(x.dtype), (B, N_META, D_MODEL))
    h = jnp.concatenate([meta, x], axis=1)
    pos = jnp.arange(h.shape[1], dtype=jnp.int32)
    k_sh, v_sh = None, None
    for layer in range(DEPTH):
        xn = rmsnorm(h, norm_mix[layer])
        if layer < N_A_LAYERS:
            i = layer
            h = h + gla_mixer(xn, a_w_in[i], a_w_gate_up[i], a_b_gate[i], a_norm_out[i], a_w_out[i])
        else:
            j = layer - N_A_LAYERS
            h = h + swa_mixer(xn, k_sh, v_sh, b_w_q[j], b_sinks[j], b_w_out[j], pos)
        h = h + sq_relu_mlp(rmsnorm(h, norm_mlp[layer]), w_mlp_up[layer], w_mlp_down[layer])
        if layer == N_A_LAYERS - 1:
            k_sh, v_sh = shared_kv(h, kv_norm, w_kv, pos)
    return rmsnorm(h[:, N_META:], norm_final)
```

```python
import functools
import math

import jax
import jax.numpy as jnp
from jax import lax
from jax.experimental import pallas as pl
from jax.experimental.pallas import tpu as pltpu

D_MODEL = 2048
N_META = 16
D_FF = 4 * D_MODEL
RMS_EPS = 1e-6
NEG_INF = -1e30

GLA_HEADS = 4
GLA_DK = 256
GLA_DV = 512
GLA_QK = GLA_HEADS * GLA_DK
GLA_VD = GLA_HEADS * GLA_DV
GLA_RANK = 16
GLA_GATE_NORM = 16.0
GLA_IN = 2 * GLA_QK + 2 * GLA_VD + GLA_RANK

SWA_HEAD_DIM = 64
SWA_Q_HEADS = 32
SWA_KV_HEADS = 4
SWA_GROUP = SWA_Q_HEADS // SWA_KV_HEADS
SWA_WINDOW = 128
ROPE_THETA = 10000.0

LANES = 128
FRONT = 512
META_ROW0 = FRONT - N_META
TM = 512
GLA_CHUNK = 256
GLA_SUB = 64
SWA_BLOCK = 128
SWA_HALF = SWA_BLOCK // 2
SWA_KEYS = SWA_WINDOW + SWA_HALF + N_META
RANK_PAD = LANES
GLA_IN_PAD = GLA_IN - GLA_RANK + RANK_PAD
VMEM_LIMIT = 52 * 1024 * 1024

_F32 = jnp.float32
_BF16 = jnp.bfloat16


def _dot(a, b):
    return jnp.dot(a, b, preferred_element_type=_F32)


def _dot_nt(a, b):
    return lax.dot_general(a, b, (((1,), (1,)), ((), ())), preferred_element_type=_F32)


def _dot_tn(a, b):
    return lax.dot_general(a, b, (((0,), (0,)), ((), ())), preferred_element_type=_F32)


def _params(*semantics):
    return pltpu.CompilerParams(dimension_semantics=semantics, vmem_limit_bytes=VMEM_LIMIT)


def _rms_scale(x):
    return lax.rsqrt(jnp.mean(x * x, axis=-1, keepdims=True) + RMS_EPS)


def _norm_matmul_kernel(h_ref, g_ref, w_ref, o_ref, xn_ref):
    @pl.when(pl.program_id(1) == 0)
    def _():
        x = h_ref[...]
        xn_ref[...] = (x * _rms_scale(x) * g_ref[...]).astype(_BF16)

    o_ref[...] = _dot(xn_ref[...], w_ref[...]).astype(o_ref.dtype)


def _norm_matmul(h, gain, w, tn):
    rows, d = h.shape
    n = w.shape[1]
    return pl.pallas_call(
        _norm_matmul_kernel,
        grid=(rows // TM, n // tn),
        in_specs=[
            pl.BlockSpec((TM, d), lambda i, j: (i, 0)),
            pl.BlockSpec((1, d), lambda i, j: (0, 0)),
            pl.BlockSpec((d, tn), lambda i, j: (0, j)),
        ],
        out_specs=pl.BlockSpec((TM, tn), lambda i, j: (i, j)),
        out_shape=jax.ShapeDtypeStruct((rows, n), _BF16),
        scratch_shapes=[pltpu.VMEM((TM, d), _BF16)],
        compiler_params=_params("parallel", "arbitrary"),
        name="norm_matmul",
    )(h, gain.reshape(1, d), w)


def _matmul_residual_kernel(a_ref, w_ref, h_ref, o_ref):
    o_ref[...] = h_ref[...] + _dot(a_ref[...], w_ref[...])


def _matmul_residual(a, w, h):
    rows, k = a.shape
    d = w.shape[1]
    return pl.pallas_call(
        _matmul_residual_kernel,
        grid=(rows // TM,),
        in_specs=[
            pl.BlockSpec((TM, k), lambda i: (i, 0)),
            pl.BlockSpec((k, d), lambda i: (0, 0)),
            pl.BlockSpec((TM, d), lambda i: (i, 0)),
        ],
        out_specs=pl.BlockSpec((TM, d), lambda i: (i, 0)),
        out_shape=jax.ShapeDtypeStruct((rows, d), _F32),
        compiler_params=_params("parallel"),
        name="matmul_residual",
    )(a, w, h)


def _mlp_kernel(h_ref, g_ref, wu_ref, wd_ref, o_ref, xn_ref):
    @pl.when(pl.program_id(1) == 0)
    def _():
        x = h_ref[...]
        xn_ref[...] = (x * _rms_scale(x) * g_ref[...]).astype(_BF16)
        o_ref[...] = x

    u = jnp.maximum(_dot(xn_ref[...], wu_ref[...]), 0.0)
    o_ref[...] += _dot((u * u).astype(_BF16), wd_ref[...])


def _mlp(h, gain, w_up, w_down, tf):
    rows, d = h.shape
    ff = w_up.shape[1]
    return pl.pallas_call(
        _mlp_kernel,
        grid=(rows // TM, ff // tf),
        in_specs=[
            pl.BlockSpec((TM, d), lambda i, j: (i, 0)),
            pl.BlockSpec((1, d), lambda i, j: (0, 0)),
            pl.BlockSpec((d, tf), lambda i, j: (0, j)),
            pl.BlockSpec((tf, d), lambda i, j: (j, 0)),
        ],
        out_specs=pl.BlockSpec((TM, d), lambda i, j: (i, 0)),
        out_shape=jax.ShapeDtypeStruct((rows, d), _F32),
        scratch_shapes=[pltpu.VMEM((TM, d), _BF16)],
        compiler_params=_params("parallel", "arbitrary"),
        name="mlp",
    )(h, gain.reshape(1, d), w_up, w_down)


def _final_norm_kernel(h_ref, g_ref, o_ref):
    x = h_ref[...]
    o_ref[...] = x * _rms_scale(x) * g_ref[...]


def _final_norm(h, gain, seq):
    d = h.shape[1]
    skip = FRONT // TM
    return pl.pallas_call(
        _final_norm_kernel,
        grid=(seq // TM,),
        in_specs=[
            pl.BlockSpec((TM, d), lambda i: (i + skip, 0)),
            pl.BlockSpec((1, d), lambda i: (0, 0)),
        ],
        out_specs=pl.BlockSpec((TM, d), lambda i: (i, 0)),
        out_shape=jax.ShapeDtypeStruct((seq, d), _F32),
        compiler_params=_params("parallel"),
        name="final_norm",
    )(h, gain.reshape(1, d))


def _gla_kernel(q_ref, k_ref, v_ref, g_ref, r_ref, wg_ref, bg_ref, no_ref, o_ref, s_ref):
    step = pl.program_id(0)
    c = GLA_CHUNK
    nsub = c // GLA_SUB

    @pl.when(step == 0)
    def _():
        s_ref[...] = jnp.zeros_like(s_ref)
        o_ref[...] = jnp.zeros_like(o_ref)

    @pl.when(step > 0)
    def _():
        x = _dot(r_ref[...], wg_ref[...]) + bg_ref[...]
        gk = (jnp.minimum(x, 0.0) - jnp.log1p(jnp.exp(-jnp.abs(x)))) * (1.0 / GLA_GATE_NORM)
        row = step * c + lax.broadcasted_iota(jnp.int32, (c, 1), 0)
        gk = jnp.where(row >= META_ROW0, gk, 0.0)
        tri = (lax.broadcasted_iota(jnp.int32, (c, c), 0)
               >= lax.broadcasted_iota(jnp.int32, (c, c), 1)).astype(_BF16)
        hi = gk.astype(_BF16)
        lo = (gk - hi.astype(_F32)).astype(_BF16)
        b_all = _dot(tri, hi) + _dot(tri, lo)

        sub_r = lax.broadcasted_iota(jnp.int32, (GLA_SUB, GLA_SUB), 0)
        sub_c = lax.broadcasted_iota(jnp.int32, (GLA_SUB, GLA_SUB), 1)
        diag_ok = sub_c <= sub_r

        for h in range(GLA_HEADS):
            ks = slice(h * GLA_DK, (h + 1) * GLA_DK)
            vs = slice(h * GLA_DV, (h + 1) * GLA_DV)
            b = b_all[:, ks]
            qf = q_ref[:, ks].astype(_F32) * (GLA_DK ** -0.5)
            kf = k_ref[:, ks].astype(_F32)
            vh = v_ref[:, vs]
            state = s_ref[h]
            b_last = b[c - 1:c, :]

            o_inter = _dot((qf * jnp.exp(b)).astype(_BF16), state.astype(_BF16))

            o_rows = []
            for i in range(nsub):
                lo_r, hi_r = i * GLA_SUB, (i + 1) * GLA_SUB
                ref = b[lo_r - 1:lo_r, :] if i > 0 else jnp.zeros((1, GLA_DK), _F32)
                qs = (qf[lo_r:hi_r] * jnp.exp(b[lo_r:hi_r] - ref)).astype(_BF16)
                kt = (kf[:hi_r] * jnp.exp(ref - b[:hi_r])).astype(_BF16)
                a = _dot_nt(qs, kt)
                a_diag = jnp.where(diag_ok, a[:, lo_r:hi_r], 0.0)
                a = a_diag if i == 0 else jnp.concatenate([a[:, :lo_r], a_diag], axis=1)
                o_rows.append(_dot(a.astype(_BF16), vh[:hi_r]))
            o = jnp.concatenate(o_rows, axis=0) + o_inter

            k_dec = (kf * jnp.exp(b_last - b)).astype(_BF16)
            decay = jnp.exp(jnp.broadcast_to(b_last, (LANES, GLA_DK))).T
            decay = jnp.concatenate([decay] * (GLA_DV // LANES), axis=1)
            s_ref[h] = state * decay + _dot_tn(k_dec, vh)

            on = o * _rms_scale(o) * no_ref[...]
            gf = g_ref[:, vs].astype(_F32)
            o_ref[:, vs] = (on * gf * (1.0 / (1.0 + jnp.exp(-gf)))).astype(o_ref.dtype)


def _gla(proj, w_gate, b_gate, norm_out):
    rows = proj.shape[0]
    c = GLA_CHUNK
    r_col = (GLA_IN - GLA_RANK) // RANK_PAD
    return pl.pallas_call(
        _gla_kernel,
        grid=(rows // c,),
        in_specs=[
            pl.BlockSpec((c, GLA_QK), lambda s: (s, 0)),
            pl.BlockSpec((c, GLA_QK), lambda s: (s, 1)),
            pl.BlockSpec((c, GLA_VD), lambda s: (s, 1)),
            pl.BlockSpec((c, GLA_VD), lambda s: (s, 2)),
            pl.BlockSpec((c, RANK_PAD), lambda s: (s, r_col)),
            pl.BlockSpec((RANK_PAD, GLA_QK), lambda s: (0, 0)),
            pl.BlockSpec((1, GLA_QK), lambda s: (0, 0)),
            pl.BlockSpec((1, GLA_DV), lambda s: (0, 0)),
        ],
        out_specs=pl.BlockSpec((c, GLA_VD), lambda s: (s, 0)),
        out_shape=jax.ShapeDtypeStruct((rows, GLA_VD), _BF16),
        scratch_shapes=[pltpu.VMEM((GLA_HEADS, GLA_DK, GLA_DV), _F32)],
        compiler_params=_params("arbitrary"),
        name="gla",
    )(proj, proj, proj, proj, proj, w_gate, b_gate.reshape(1, GLA_QK), norm_out.reshape(1, GLA_DV))


def _swap_halves(x, width):
    lane = lax.broadcasted_iota(jnp.int32, x.shape, 1)
    first = (lane % (2 * width)) < width
    return jnp.where(first, pltpu.roll(x, LANES - width, 1), pltpu.roll(x, width, 1))


def _qkv_kernel(h_ref, gq_ref, gkv_ref, wq_ref, wkv_ref, q_ref, k_ref, v_ref):
    half = SWA_HEAD_DIM // 2
    x = h_ref[...]
    xhat = x * _rms_scale(x)
    q = _dot((xhat * gq_ref[...]).astype(_BF16), wq_ref[...])
    kv = _dot((xhat * gkv_ref[...]).astype(_BF16), wkv_ref[...])

    row = pl.program_id(0) * TM + lax.broadcasted_iota(jnp.int32, (TM, LANES), 0)
    lane = lax.broadcasted_iota(jnp.int32, (TM, LANES), 1)
    pos = (row - META_ROW0).astype(_F32)
    inv_freq = jnp.exp((lane % half).astype(_F32) * (-math.log(ROPE_THETA) / half))
    ang = pos * inv_freq
    cos = jnp.cos(ang)
    sin = jnp.where((lane % SWA_HEAD_DIM) < half, -jnp.sin(ang), jnp.sin(ang))

    def rope(t):
        return t * cos + _swap_halves(t, half) * sin

    qscale = SWA_HEAD_DIM ** -0.5
    for p in range(SWA_Q_HEADS // 2):
        sl = slice(p * LANES, (p + 1) * LANES)
        q_ref[:, sl] = (rope(q[:, sl]) * qscale).astype(q_ref.dtype)

    low = (lane % LANES) < SWA_HEAD_DIM
    kv_width = SWA_KV_HEADS * SWA_HEAD_DIM
    for p in range(SWA_KV_HEADS // 2):
        sl = slice(p * LANES, (p + 1) * LANES)
        for t, o_ref in ((rope(kv[:, sl]), k_ref), (kv[:, kv_width + p * LANES:kv_width + (p + 1) * LANES], v_ref)):
            swapped = pltpu.roll(t, SWA_HEAD_DIM, 1)
            o_ref[:, (2 * p) * LANES:(2 * p + 1) * LANES] = jnp.where(low, t, swapped).astype(o_ref.dtype)
            o_ref[:, (2 * p + 1) * LANES:(2 * p + 2) * LANES] = jnp.where(low, swapped, t).astype(o_ref.dtype)


def _qkv(h, gain_q, gain_kv, w_q, w_kv):
    rows, d = h.shape
    nq = w_q.shape[1]
    nkv = w_kv.shape[1]
    dup = SWA_KV_HEADS * LANES
    return pl.pallas_call(
        _qkv_kernel,
        grid=(rows // TM,),
        in_specs=[
            pl.BlockSpec((TM, d), lambda i: (i, 0)),
            pl.BlockSpec((1, d), lambda i: (0, 0)),
            pl.BlockSpec((1, d), lambda i: (0, 0)),
            pl.BlockSpec((d, nq), lambda i: (0, 0)),
            pl.BlockSpec((d, nkv), lambda i: (0, 0)),
        ],
        out_specs=[
            pl.BlockSpec((TM, nq), lambda i: (i, 0)),
            pl.BlockSpec((TM, dup), lambda i: (i, 0)),
            pl.BlockSpec((TM, dup), lambda i: (i, 0)),
        ],
        out_shape=[
            jax.ShapeDtypeStruct((rows, nq), _BF16),
            jax.ShapeDtypeStruct((rows, dup), _BF16),
            jax.ShapeDtypeStruct((rows, dup), _BF16),
        ],
        compiler_params=_params("parallel"),
        name="qkv_rope",
    )(h, gain_q.reshape(1, d), gain_kv.reshape(1, d), w_q, w_kv)


def _swa_kernel(sink_ref, q_ref, kp_ref, kc_ref, km_ref, vp_ref, vc_ref, vm_ref, o_ref):
    n = pl.program_id(0)
    first_block = META_ROW0 // SWA_BLOCK
    meta_lo = META_ROW0 % SWA_BLOCK

    @pl.when(n < first_block)
    def _():
        o_ref[...] = jnp.zeros_like(o_ref)

    @pl.when(n >= first_block)
    def _():
        kb = jnp.concatenate([km_ref[meta_lo:, :], kp_ref[...], kc_ref[...], km_ref[meta_lo:, :]], axis=0)
        vb = jnp.concatenate([vm_ref[meta_lo:, :], vp_ref[...], vc_ref[...], vm_ref[meta_lo:, :]], axis=0)
        lane = lax.broadcasted_iota(jnp.int32, (SWA_HALF, LANES), 1)
        low = lane < SWA_HEAD_DIM
        high = lane >= SWA_HEAD_DIM

        qi0 = lax.broadcasted_iota(jnp.int32, (SWA_HALF, SWA_KEYS), 0)
        col = lax.broadcasted_iota(jnp.int32, (SWA_HALF, SWA_KEYS), 1)
        band_w = SWA_KEYS - N_META
        caps = []
        for hq in range(2):
            qi = qi0 + hq * SWA_HALF
            if hq == 0:
                is_meta = col < N_META
                kj = col - N_META
                m = col
                start = 0
            else:
                is_meta = col >= band_w
                kj = col + SWA_HALF
                m = col - band_w
                start = N_META + SWA_HALF
            band_ok = (kj > qi) & (kj <= qi + SWA_WINDOW) & ((n - 1) * SWA_BLOCK + kj >= FRONT)
            meta_ok = (META_ROW0 + m) <= n * SWA_BLOCK + qi
            ok = (is_meta & meta_ok) | (jnp.logical_not(is_meta) & band_ok)
            cap = jnp.where(ok, jnp.inf, NEG_INF).astype(_F32)
            caps.append((jnp.concatenate([cap] * SWA_GROUP, axis=0), start))

        for g in range(SWA_KV_HEADS):
            kg = kb[:, g * LANES:(g + 1) * LANES]
            vg = vb[:, g * LANES:(g + 1) * LANES]
            sink = jnp.concatenate(
                [jnp.full((SWA_HALF, 1), sink_ref[g * SWA_GROUP + j], _F32) for j in range(SWA_GROUP)], axis=0)
            for hq in range(2):
                cap, start = caps[hq]
                rows = slice(hq * SWA_HALF, (hq + 1) * SWA_HALF)
                parts = []
                for j in range(SWA_GROUP):
                    pair = (g * SWA_GROUP + j) // 2
                    t = q_ref[rows, pair * LANES:(pair + 1) * LANES]
                    parts.append(jnp.where(low if j % 2 == 0 else high, t, jnp.zeros_like(t)))
                lhs = jnp.concatenate(parts, axis=0)
                kw = kg[start:start + SWA_KEYS]
                vw = vg[start:start + SWA_KEYS]
                s = _dot_nt(lhs, kw)
                s = jnp.minimum(s, cap)
                mx = jnp.maximum(jnp.max(s, axis=-1, keepdims=True), sink)
                p = jnp.exp(s - mx)
                denom = jnp.sum(p, axis=-1, keepdims=True) + jnp.exp(sink - mx)
                o = _dot(p.astype(_BF16), vw) * (1.0 / denom)
                for j in range(0, SWA_GROUP, 2):
                    pair = (g * SWA_GROUP + j) // 2
                    even = o[j * SWA_HALF:(j + 1) * SWA_HALF]
                    odd = o[(j + 1) * SWA_HALF:(j + 2) * SWA_HALF]
                    o_ref[rows, pair * LANES:(pair + 1) * LANES] = jnp.where(low, even, odd).astype(o_ref.dtype)


def _swa(q, k_dup, v_dup, sinks):
    rows, nq = q.shape
    dup = k_dup.shape[1]
    first_block = META_ROW0 // SWA_BLOCK
    blk = pl.BlockSpec((SWA_BLOCK, dup), lambda n: (n, 0))
    prev = pl.BlockSpec((SWA_BLOCK, dup), lambda n: (jnp.maximum(n - 1, 0), 0))
    meta = pl.BlockSpec((SWA_BLOCK, dup), lambda n: (first_block, 0))
    return pl.pallas_call(
        _swa_kernel,
        grid=(rows // SWA_BLOCK,),
        in_specs=[
            pl.BlockSpec(memory_space=pltpu.SMEM),
            pl.BlockSpec((SWA_BLOCK, nq), lambda n: (n, 0)),
            prev, blk, meta, prev, blk, meta,
        ],
        out_specs=pl.BlockSpec((SWA_BLOCK, nq), lambda n: (n, 0)),
        out_shape=jax.ShapeDtypeStruct((rows, nq), _BF16),
        compiler_params=_params("parallel"),
        name="swa",
    )(sinks, q, k_dup, k_dup, k_dup, v_dup, v_dup, v_dup)


def kernel(x, meta_tokens, norm_mix, norm_mlp, w_mlp_up, w_mlp_down, a_w_in, a_w_gate_up, a_b_gate, a_norm_out, a_w_out, kv_norm, w_kv, b_w_q, b_sinks, b_w_out, norm_final):
    batch, seq, d = x.shape
    assert batch == 1 and d == D_MODEL and seq % TM == 0
    assert norm_mix.shape[0] == 2 and a_w_in.shape[0] == 1 and b_w_q.shape[0] == 1

    h = jnp.concatenate([jnp.zeros((META_ROW0, d), x.dtype), meta_tokens.astype(x.dtype), x[0]], axis=0)
    bf = lambda w: w.astype(_BF16)
    w_in = jnp.pad(bf(a_w_in[0]), ((0, 0), (0, GLA_IN_PAD - GLA_IN)))
    w_gate = jnp.pad(bf(a_w_gate_up[0]), ((0, RANK_PAD - GLA_RANK), (0, 0)))

    proj = _norm_matmul(h, norm_mix[0], w_in, tn=GLA_IN_PAD // 7)
    o = _gla(proj, w_gate, a_b_gate[0], a_norm_out[0])
    h = _matmul_residual(o, bf(a_w_out[0]), h)
    h = _mlp(h, norm_mlp[0], bf(w_mlp_up[0]), bf(w_mlp_down[0]), tf=1024)

    q, k_dup, v_dup = _qkv(h, norm_mix[1], kv_norm, bf(b_w_q[0]), bf(w_kv))
    o = _swa(q, k_dup, v_dup, b_sinks[0])
    h = _matmul_residual(o, bf(b_w_out[0]), h)
    h = _mlp(h, norm_mlp[1], bf(w_mlp_up[1]), bf(w_mlp_down[1]), tf=1024)

    return _final_norm(h, norm_final, seq)[None]
```

```python
import functools
import math

import jax
import jax.numpy as jnp
from jax import lax
from jax.experimental import pallas as pl
from jax.experimental.pallas import tpu as pltpu

D_MODEL = 2048
N_META = 16
D_FF = 4 * D_MODEL
RMS_EPS = 1e-6
NEG_INF = -1e30

GLA_HEADS = 4
GLA_DK = 256
GLA_DV = 512
GLA_QK = GLA_HEADS * GLA_DK
GLA_VD = GLA_HEADS * GLA_DV
GLA_RANK = 16
GLA_GATE_NORM = 16.0
GLA_IN = 2 * GLA_QK + 2 * GLA_VD + GLA_RANK

SWA_HEAD_DIM = 64
SWA_Q_HEADS = 32
SWA_KV_HEADS = 4
SWA_GROUP = SWA_Q_HEADS // SWA_KV_HEADS
SWA_WINDOW = 128
ROPE_THETA = 10000.0

LANES = 128
FRONT = 512
META_ROW0 = FRONT - N_META
TM = 512
GLA_CHUNK = 256
GLA_SUB = 64
SWA_BLOCK = 128
SWA_HALF = SWA_BLOCK // 2
SWA_KEYS = 2 * SWA_BLOCK
LOG2E = math.log2(math.e)
RANK_PAD = LANES
VMEM_LIMIT = 52 * 1024 * 1024

_F32 = jnp.float32
_BF16 = jnp.bfloat16


def _dot(a, b):
    return jnp.dot(a, b, preferred_element_type=_F32)


def _dot_nt(a, b):
    return lax.dot_general(a, b, (((1,), (1,)), ((), ())), preferred_element_type=_F32)


def _dot_tn(a, b):
    return lax.dot_general(a, b, (((0,), (0,)), ((), ())), preferred_element_type=_F32)


def _params(*semantics):
    return pltpu.CompilerParams(dimension_semantics=semantics, vmem_limit_bytes=VMEM_LIMIT)


def _rms_scale(x):
    return lax.rsqrt(jnp.mean(x * x, axis=-1, keepdims=True) + RMS_EPS)


def _stream_tile(head_ref, x_ref, i):
    return jnp.where(i == 0, head_ref[...], x_ref[...])


def _stream_specs(d, grid_rank):
    if grid_rank == 1:
        return [pl.BlockSpec((TM, d), lambda i: (0, 0)),
                pl.BlockSpec((TM, d), lambda i: (jnp.maximum(i - 1, 0), 0))]
    return [pl.BlockSpec((TM, d), lambda i, j: (0, 0)),
            pl.BlockSpec((TM, d), lambda i, j: (jnp.maximum(i - 1, 0), 0))]


def _in_proj_kernel(head_ref, x_ref, g_ref, w_ref, wr_ref, o_ref, r_ref, xn_ref):
    @pl.when(pl.program_id(1) == 0)
    def _():
        x = _stream_tile(head_ref, x_ref, pl.program_id(0))
        xn = (x * _rms_scale(x) * g_ref[...]).astype(_BF16)
        xn_ref[...] = xn
        r_ref[...] = _dot(xn, wr_ref[...]).astype(r_ref.dtype)

    o_ref[...] = _dot(xn_ref[...], w_ref[...]).astype(o_ref.dtype)


def _in_proj(head, x, gain, w, w_r, tn):
    d = x.shape[1]
    rows = head.shape[0] + x.shape[0]
    n = w.shape[1]
    nr = w_r.shape[1]
    return pl.pallas_call(
        _in_proj_kernel,
        grid=(rows // TM, n // tn),
        in_specs=_stream_specs(d, 2) + [
            pl.BlockSpec((1, d), lambda i, j: (0, 0)),
            pl.BlockSpec((d, tn), lambda i, j: (0, j)),
            pl.BlockSpec((d, nr), lambda i, j: (0, 0)),
        ],
        out_specs=[pl.BlockSpec((TM, tn), lambda i, j: (i, j)),
                   pl.BlockSpec((TM, nr), lambda i, j: (i, 0))],
        out_shape=[jax.ShapeDtypeStruct((rows, n), _BF16),
                   jax.ShapeDtypeStruct((rows, nr), _BF16)],
        scratch_shapes=[pltpu.VMEM((TM, d), _BF16)],
        compiler_params=_params("parallel", "arbitrary"),
        name="in_proj",
    )(head, x, gain.reshape(1, d), w, w_r)


def _matmul_residual_split_kernel(a_ref, w_ref, head_ref, x_ref, o_ref):
    o_ref[...] = _stream_tile(head_ref, x_ref, pl.program_id(0)) + _dot(a_ref[...], w_ref[...])


def _matmul_residual_kernel(a_ref, w_ref, h_ref, o_ref):
    o_ref[...] = h_ref[...] + _dot(a_ref[...], w_ref[...])


def _matmul_residual(a, w, *stream):
    rows, k = a.shape
    d = w.shape[1]
    split = len(stream) == 2
    stream_specs = _stream_specs(d, 1) if split else [pl.BlockSpec((TM, d), lambda i: (i, 0))]
    return pl.pallas_call(
        _matmul_residual_split_kernel if split else _matmul_residual_kernel,
        grid=(rows // TM,),
        in_specs=[
            pl.BlockSpec((TM, k), lambda i: (i, 0)),
            pl.BlockSpec((k, d), lambda i: (0, 0)),
        ] + stream_specs,
        out_specs=pl.BlockSpec((TM, d), lambda i: (i, 0)),
        out_shape=jax.ShapeDtypeStruct((rows, d), _F32),
        compiler_params=_params("parallel"),
        name="matmul_residual",
    )(a, w, *stream)


def _mlp_kernel(h_ref, g_ref, wu_ref, wd_ref, gf_ref, o_ref, xn_ref, *, final):
    j = pl.program_id(1)

    @pl.when(j == 0)
    def _():
        x = h_ref[...]
        xn_ref[...] = (x * _rms_scale(x) * g_ref[...]).astype(_BF16)
        o_ref[...] = x

    u = jnp.maximum(_dot(xn_ref[...], wu_ref[...]), 0.0)
    o_ref[...] += _dot((u * u).astype(_BF16), wd_ref[...])

    if final:
        @pl.when(j == pl.num_programs(1) - 1)
        def _():
            y = o_ref[...]
            o_ref[...] = y * _rms_scale(y) * gf_ref[...]


def _mlp(h, gain, w_up, w_down, layer, tf, final_gain=None):
    rows, d = h.shape
    ff = w_up.shape[2]
    final = final_gain is not None
    skip = FRONT // TM if final else 0
    out_rows = rows - skip * TM
    gf = (final_gain if final else gain).reshape(1, d)
    return pl.pallas_call(
        functools.partial(_mlp_kernel, final=final),
        grid=(out_rows // TM, ff // tf),
        in_specs=[
            pl.BlockSpec((TM, d), lambda i, j: (i + skip, 0)),
            pl.BlockSpec((1, d), lambda i, j: (0, 0)),
            pl.BlockSpec((None, d, tf), lambda i, j: (layer, 0, j)),
            pl.BlockSpec((None, tf, d), lambda i, j: (layer, j, 0)),
            pl.BlockSpec((1, d), lambda i, j: (0, 0)),
        ],
        out_specs=pl.BlockSpec((TM, d), lambda i, j: (i, 0)),
        out_shape=jax.ShapeDtypeStruct((out_rows, d), _F32),
        scratch_shapes=[pltpu.VMEM((TM, d), _BF16)],
        compiler_params=_params("parallel", "arbitrary"),
        name="mlp_final" if final else "mlp",
    )(h, gain.reshape(1, d), w_up, w_down, gf)


def _gla_kernel(q_ref, k_ref, v_ref, g_ref, r_ref, wg_ref, bg_ref, no_ref, o_ref, s_ref):
    step = pl.program_id(0)
    c = GLA_CHUNK
    nsub = c // GLA_SUB

    @pl.when(step == 0)
    def _():
        s_ref[...] = jnp.zeros_like(s_ref)
        o_ref[...] = jnp.zeros_like(o_ref)

    @pl.when(step > 0)
    def _():
        x = _dot(r_ref[...], wg_ref[...]) + bg_ref[...]
        gk = (jnp.minimum(x, 0.0) - jnp.log1p(jnp.exp(-jnp.abs(x)))) * (1.0 / GLA_GATE_NORM)
        row = step * c + lax.broadcasted_iota(jnp.int32, (c, 1), 0)
        gk = jnp.where(row >= META_ROW0, gk, 0.0)
        tri = (lax.broadcasted_iota(jnp.int32, (c, c), 0)
               >= lax.broadcasted_iota(jnp.int32, (c, c), 1)).astype(_BF16)
        hi = gk.astype(_BF16)
        lo = (gk - hi.astype(_F32)).astype(_BF16)
        b_all = _dot(tri, hi) + _dot(tri, lo)

        sub_r = lax.broadcasted_iota(jnp.int32, (GLA_SUB, GLA_SUB), 0)
        sub_c = lax.broadcasted_iota(jnp.int32, (GLA_SUB, GLA_SUB), 1)
        diag_ok = sub_c <= sub_r

        for h in range(GLA_HEADS):
            ks = slice(h * GLA_DK, (h + 1) * GLA_DK)
            vs = slice(h * GLA_DV, (h + 1) * GLA_DV)
            b = b_all[:, ks]
            qf = q_ref[:, ks].astype(_F32) * (GLA_DK ** -0.5)
            kf = k_ref[:, ks].astype(_F32)
            vh = v_ref[:, vs]
            state = s_ref[h]
            b_last = b[c - 1:c, :]

            o_inter = _dot((qf * jnp.exp(b)).astype(_BF16), state.astype(_BF16))

            o_rows = []
            for i in range(nsub):
                lo_r, hi_r = i * GLA_SUB, (i + 1) * GLA_SUB
                ref = b[lo_r - 1:lo_r, :] if i > 0 else jnp.zeros((1, GLA_DK), _F32)
                qs = (qf[lo_r:hi_r] * jnp.exp(b[lo_r:hi_r] - ref)).astype(_BF16)
                kt = (kf[:hi_r] * jnp.exp(ref - b[:hi_r])).astype(_BF16)
                a = _dot_nt(qs, kt)
                a_diag = jnp.where(diag_ok, a[:, lo_r:hi_r], 0.0)
                a = a_diag if i == 0 else jnp.concatenate([a[:, :lo_r], a_diag], axis=1)
                o_rows.append(_dot(a.astype(_BF16), vh[:hi_r]))
            o = jnp.concatenate(o_rows, axis=0) + o_inter

            k_dec = (kf * jnp.exp(b_last - b)).astype(_BF16)
            decay = jnp.exp(jnp.broadcast_to(b_last, (LANES, GLA_DK))).T
            decay = jnp.concatenate([decay] * (GLA_DV // LANES), axis=1)
            s_ref[h] = state * decay + _dot_tn(k_dec, vh)

            on = o * _rms_scale(o) * no_ref[...]
            gf = g_ref[:, vs].astype(_F32)
            o_ref[:, vs] = (on * gf * (1.0 / (1.0 + jnp.exp(-gf)))).astype(o_ref.dtype)


def _gla(proj, r, w_gate, b_gate, norm_out):
    rows = proj.shape[0]
    c = GLA_CHUNK
    return pl.pallas_call(
        _gla_kernel,
        grid=(rows // c,),
        in_specs=[
            pl.BlockSpec((c, GLA_QK), lambda s: (s, 0)),
            pl.BlockSpec((c, GLA_QK), lambda s: (s, 1)),
            pl.BlockSpec((c, GLA_VD), lambda s: (s, 1)),
            pl.BlockSpec((c, GLA_VD), lambda s: (s, 2)),
            pl.BlockSpec((c, RANK_PAD), lambda s: (s, 0)),
            pl.BlockSpec((RANK_PAD, GLA_QK), lambda s: (0, 0)),
            pl.BlockSpec((1, GLA_QK), lambda s: (0, 0)),
            pl.BlockSpec((1, GLA_DV), lambda s: (0, 0)),
        ],
        out_specs=pl.BlockSpec((c, GLA_VD), lambda s: (s, 0)),
        out_shape=jax.ShapeDtypeStruct((rows, GLA_VD), _BF16),
        scratch_shapes=[pltpu.VMEM((GLA_HEADS, GLA_DK, GLA_DV), _F32)],
        compiler_params=_params("arbitrary"),
        name="gla",
    )(proj, proj, proj, proj, r, w_gate, b_gate.reshape(1, GLA_QK), norm_out.reshape(1, GLA_DV))


def _swap_halves(x, width):
    lane = lax.broadcasted_iota(jnp.int32, x.shape, 1)
    first = (lane % (2 * width)) < width
    return jnp.where(first, pltpu.roll(x, LANES - width, 1), pltpu.roll(x, width, 1))


def _qkv_kernel(h_ref, gq_ref, gkv_ref, wq_ref, wkv_ref, q_ref, k_ref, v_ref):
    half = SWA_HEAD_DIM // 2
    x = h_ref[...]
    xhat = x * _rms_scale(x)
    q = _dot((xhat * gq_ref[...]).astype(_BF16), wq_ref[...])
    kv = _dot((xhat * gkv_ref[...]).astype(_BF16), wkv_ref[...])

    row = pl.program_id(0) * TM + lax.broadcasted_iota(jnp.int32, (TM, LANES), 0)
    lane = lax.broadcasted_iota(jnp.int32, (TM, LANES), 1)
    pos = (row - META_ROW0).astype(_F32)
    inv_freq = jnp.exp((lane % half).astype(_F32) * (-math.log(ROPE_THETA) / half))
    ang = pos * inv_freq
    cos = jnp.cos(ang)
    sin = jnp.where((lane % SWA_HEAD_DIM) < half, -jnp.sin(ang), jnp.sin(ang))

    def rope(t):
        return t * cos + _swap_halves(t, half) * sin

    qscale = SWA_HEAD_DIM ** -0.5 * LOG2E
    for p in range(SWA_Q_HEADS // 2):
        sl = slice(p * LANES, (p + 1) * LANES)
        q_ref[:, sl] = (rope(q[:, sl]) * qscale).astype(q_ref.dtype)

    low = (lane % LANES) < SWA_HEAD_DIM
    kv_width = SWA_KV_HEADS * SWA_HEAD_DIM
    for p in range(SWA_KV_HEADS // 2):
        sl = slice(p * LANES, (p + 1) * LANES)
        for t, o_ref in ((rope(kv[:, sl]), k_ref), (kv[:, kv_width + p * LANES:kv_width + (p + 1) * LANES], v_ref)):
            swapped = pltpu.roll(t, SWA_HEAD_DIM, 1)
            o_ref[:, (2 * p) * LANES:(2 * p + 1) * LANES] = jnp.where(low, t, swapped).astype(o_ref.dtype)
            o_ref[:, (2 * p + 1) * LANES:(2 * p + 2) * LANES] = jnp.where(low, swapped, t).astype(o_ref.dtype)


def _qkv(h, gain_q, gain_kv, w_q, w_kv):
    rows, d = h.shape
    nq = w_q.shape[1]
    nkv = w_kv.shape[1]
    dup = SWA_KV_HEADS * LANES
    return pl.pallas_call(
        _qkv_kernel,
        grid=(rows // TM,),
        in_specs=[
            pl.BlockSpec((TM, d), lambda i: (i, 0)),
            pl.BlockSpec((1, d), lambda i: (0, 0)),
            pl.BlockSpec((1, d), lambda i: (0, 0)),
            pl.BlockSpec((d, nq), lambda i: (0, 0)),
            pl.BlockSpec((d, nkv), lambda i: (0, 0)),
        ],
        out_specs=[
            pl.BlockSpec((TM, nq), lambda i: (i, 0)),
            pl.BlockSpec((TM, dup), lambda i: (i, 0)),
            pl.BlockSpec((TM, dup), lambda i: (i, 0)),
        ],
        out_shape=[
            jax.ShapeDtypeStruct((rows, nq), _BF16),
            jax.ShapeDtypeStruct((rows, dup), _BF16),
            jax.ShapeDtypeStruct((rows, dup), _BF16),
        ],
        compiler_params=_params("parallel"),
        name="qkv_rope",
    )(h, gain_q.reshape(1, d), gain_kv.reshape(1, d), w_q, w_kv)


def _swa_kernel(sink_ref, q_ref, kp_ref, kc_ref, km_ref, vp_ref, vc_ref, vm_ref, o_ref):
    n = pl.program_id(0)
    first_block = META_ROW0 // SWA_BLOCK
    meta_lo = META_ROW0 % SWA_BLOCK

    @pl.when(n < first_block)
    def _():
        o_ref[...] = jnp.zeros_like(o_ref)

    @pl.when(n >= first_block)
    def _():
        kb = jnp.concatenate([km_ref[meta_lo:, :], kp_ref[...], kc_ref[...], km_ref[meta_lo:, :]], axis=0)
        vb = jnp.concatenate([vm_ref[meta_lo:, :], vp_ref[...], vc_ref[...], vm_ref[meta_lo:, :]], axis=0)
        lane = lax.broadcasted_iota(jnp.int32, (SWA_HALF, LANES), 1)
        low = lane < SWA_HEAD_DIM
        high = lane >= SWA_HEAD_DIM

        qi0 = lax.broadcasted_iota(jnp.int32, (SWA_HALF, SWA_KEYS), 0)
        col = lax.broadcasted_iota(jnp.int32, (SWA_HALF, SWA_KEYS), 1)
        band_w = SWA_KEYS - N_META
        caps = []
        for hq in range(2):
            qi = qi0 + hq * SWA_HALF
            if hq == 0:
                is_meta = col < N_META
                kj = col - N_META
                m = col
                start = 0
            else:
                is_meta = col >= band_w
                kj = col + (2 * SWA_BLOCK - band_w)
                m = col - band_w
                start = 2 * N_META + 2 * SWA_BLOCK - SWA_KEYS
            band_ok = (kj > qi) & (kj <= qi + SWA_WINDOW) & ((n - 1) * SWA_BLOCK + kj >= FRONT)
            meta_ok = (META_ROW0 + m) <= n * SWA_BLOCK + qi
            ok = (is_meta & meta_ok) | (jnp.logical_not(is_meta) & band_ok)
            caps.append((jnp.where(ok, jnp.inf, NEG_INF).astype(_F32), start))

        ones = jnp.ones((SWA_KEYS, LANES), _BF16)

        def sink_tile(g):
            return jnp.concatenate(
                [jnp.full((SWA_HALF, LANES), sink_ref[g * SWA_GROUP + j] * LOG2E, _F32)
                 for j in range(SWA_GROUP)], axis=0)

        def scores(g, hq):
            cap, start = caps[hq]
            rows = slice(hq * SWA_HALF, (hq + 1) * SWA_HALF)
            parts = []
            for j in range(SWA_GROUP):
                pair = (g * SWA_GROUP + j) // 2
                t = q_ref[rows, pair * LANES:(pair + 1) * LANES]
                parts.append(jnp.where(low if j % 2 == 0 else high, t, jnp.zeros_like(t)))
            lhs = jnp.concatenate(parts, axis=0)
            s = _dot_nt(lhs, kb[start:start + SWA_KEYS, g * LANES:(g + 1) * LANES])
            s = jnp.concatenate(
                [jnp.minimum(s[j * SWA_HALF:(j + 1) * SWA_HALF], cap) for j in range(SWA_GROUP)], axis=0)
            row_max = jnp.max(s, axis=-1, keepdims=True)
            m = jnp.maximum(jnp.broadcast_to(row_max, (SWA_GROUP * SWA_HALF, LANES)), sink_tile(g))
            return s, m

        def probs(s, m, g):
            p = jnp.concatenate([jnp.exp2(s[:, c * LANES:(c + 1) * LANES] - m)
                                 for c in range(SWA_KEYS // LANES)], axis=1).astype(_BF16)
            return p, jnp.exp2(sink_tile(g) - m)

        def outputs(p, e, g, hq):
            _, start = caps[hq]
            rows = slice(hq * SWA_HALF, (hq + 1) * SWA_HALF)
            vw = jnp.concatenate([vb[start:start + SWA_KEYS, g * LANES:(g + 1) * LANES], ones], axis=1)
            od = _dot(p, vw)
            o = od[:, :LANES] / (od[:, LANES:] + e)
            for j in range(0, SWA_GROUP, 2):
                pair = (g * SWA_GROUP + j) // 2
                even = o[j * SWA_HALF:(j + 1) * SWA_HALF]
                odd = o[(j + 1) * SWA_HALF:(j + 2) * SWA_HALF]
                o_ref[rows, pair * LANES:(pair + 1) * LANES] = jnp.where(low, even, odd).astype(o_ref.dtype)

        tiles = [(g, hq) for g in range(SWA_KV_HEADS) for hq in range(2)]
        stage_a, stage_b = {}, {}
        for t in range(len(tiles) + 2):
            if t < len(tiles):
                stage_a[t] = scores(*tiles[t])
            if 0 <= t - 1 < len(tiles):
                stage_b[t - 1] = probs(*stage_a.pop(t - 1), tiles[t - 1][0])
            if 0 <= t - 2 < len(tiles):
                outputs(*stage_b.pop(t - 2), *tiles[t - 2])


def _swa(q, k_dup, v_dup, sinks):
    rows, nq = q.shape
    dup = k_dup.shape[1]
    first_block = META_ROW0 // SWA_BLOCK
    blk = pl.BlockSpec((SWA_BLOCK, dup), lambda n: (n, 0))
    prev = pl.BlockSpec((SWA_BLOCK, dup), lambda n: (jnp.maximum(n - 1, 0), 0))
    meta = pl.BlockSpec((SWA_BLOCK, dup), lambda n: (first_block, 0))
    return pl.pallas_call(
        _swa_kernel,
        grid=(rows // SWA_BLOCK,),
        in_specs=[
            pl.BlockSpec(memory_space=pltpu.SMEM),
            pl.BlockSpec((SWA_BLOCK, nq), lambda n: (n, 0)),
            prev, blk, meta, prev, blk, meta,
        ],
        out_specs=pl.BlockSpec((SWA_BLOCK, nq), lambda n: (n, 0)),
        out_shape=jax.ShapeDtypeStruct((rows, nq), _BF16),
        compiler_params=_params("parallel"),
        name="swa",
    )(sinks, q, k_dup, k_dup, k_dup, v_dup, v_dup, v_dup)


def kernel(x, meta_tokens, norm_mix, norm_mlp, w_mlp_up, w_mlp_down, a_w_in, a_w_gate_up, a_b_gate, a_norm_out, a_w_out, kv_norm, w_kv, b_w_q, b_sinks, b_w_out, norm_final):
    batch, seq, d = x.shape
    assert batch == 1 and d == D_MODEL and seq % TM == 0
    assert norm_mix.shape[0] == 2 and a_w_in.shape[0] == 1 and b_w_q.shape[0] == 1

    head = jnp.concatenate([jnp.zeros((META_ROW0, d), x.dtype), meta_tokens.astype(x.dtype)], axis=0)
    x2 = x[0]
    bf = lambda w: w.astype(_BF16)
    n_main = GLA_IN - GLA_RANK
    w_in = bf(a_w_in[0])
    w_r = jnp.pad(w_in[:, n_main:], ((0, 0), (0, RANK_PAD - GLA_RANK)))
    w_gate = jnp.pad(bf(a_w_gate_up[0]), ((0, RANK_PAD - GLA_RANK), (0, 0)))
    w_up, w_down = bf(w_mlp_up), bf(w_mlp_down)

    proj, r = _in_proj(head, x2, norm_mix[0], w_in[:, :n_main], w_r, tn=1024)
    o = _gla(proj, r, w_gate, a_b_gate[0], a_norm_out[0])
    h = _matmul_residual(o, bf(a_w_out[0]), head, x2)
    h = _mlp(h, norm_mlp[0], w_up, w_down, 0, tf=1024)

    q, k_dup, v_dup = _qkv(h, norm_mix[1], kv_norm, bf(b_w_q[0]), bf(w_kv))
    o = _swa(q, k_dup, v_dup, b_sinks[0])
    h = _matmul_residual(o, bf(b_w_out[0]), h)
    return _mlp(h, norm_mlp[1], w_up, w_down, 1, tf=1024, final_gain=norm_final)[None]
```

```python
import functools
import math

import jax
import jax.numpy as jnp
from jax import lax
from jax.experimental import pallas as pl
from jax.experimental.pallas import tpu as pltpu

D_MODEL = 2048
N_META = 16
D_FF = 4 * D_MODEL
RMS_EPS = 1e-6
NEG_INF = -1e30

GLA_HEADS = 4
GLA_DK = 256
GLA_DV = 512
GLA_QK = GLA_HEADS * GLA_DK
GLA_VD = GLA_HEADS * GLA_DV
GLA_RANK = 16
GLA_GATE_NORM = 16.0
GLA_IN = 2 * GLA_QK + 2 * GLA_VD + GLA_RANK

SWA_HEAD_DIM = 64
SWA_Q_HEADS = 32
SWA_KV_HEADS = 4
SWA_GROUP = SWA_Q_HEADS // SWA_KV_HEADS
SWA_WINDOW = 128
ROPE_THETA = 10000.0

LANES = 128
FRONT = 512
META_ROW0 = FRONT - N_META
TM = 512
GLA_CHUNK = 256
GLA_SUB = 64
SWA_BLOCK = 128
SWA_HALF = SWA_BLOCK // 2
SWA_KEYS = 2 * SWA_BLOCK
LOG2E = math.log2(math.e)
RANK_PAD = LANES
VMEM_LIMIT = 52 * 1024 * 1024

_F32 = jnp.float32
_BF16 = jnp.bfloat16


def _dot(a, b):
    return jnp.dot(a, b, preferred_element_type=_F32)


def _dot_nt(a, b):
    return lax.dot_general(a, b, (((1,), (1,)), ((), ())), preferred_element_type=_F32)


def _dot_tn(a, b):
    return lax.dot_general(a, b, (((0,), (0,)), ((), ())), preferred_element_type=_F32)


def _params(*semantics):
    return pltpu.CompilerParams(dimension_semantics=semantics, vmem_limit_bytes=VMEM_LIMIT)


def _rms_scale(x):
    return lax.rsqrt(jnp.mean(x * x, axis=-1, keepdims=True) + RMS_EPS)


def _stream_tile(head_ref, x_ref, i):
    return jnp.where(i == 0, head_ref[...], x_ref[...])


def _stream_specs(d, grid_rank):
    if grid_rank == 1:
        return [pl.BlockSpec((TM, d), lambda i: (0, 0)),
                pl.BlockSpec((TM, d), lambda i: (jnp.maximum(i - 1, 0), 0))]
    return [pl.BlockSpec((TM, d), lambda i, j: (0, 0)),
            pl.BlockSpec((TM, d), lambda i, j: (jnp.maximum(i - 1, 0), 0))]


def _in_proj_kernel(head_ref, x_ref, g_ref, w_ref, wr_ref, o_ref, r_ref, xn_ref):
    @pl.when(pl.program_id(1) == 0)
    def _():
        x = _stream_tile(head_ref, x_ref, pl.program_id(0))
        xn = (x * _rms_scale(x) * g_ref[...]).astype(_BF16)
        xn_ref[...] = xn
        r_ref[...] = _dot(xn, wr_ref[...]).astype(r_ref.dtype)

    o_ref[...] = _dot(xn_ref[...], w_ref[...]).astype(o_ref.dtype)


def _in_proj(head, x, gain, w, w_r, tn):
    d = x.shape[1]
    rows = head.shape[0] + x.shape[0]
    n = w.shape[1]
    nr = w_r.shape[1]
    return pl.pallas_call(
        _in_proj_kernel,
        grid=(rows // TM, n // tn),
        in_specs=_stream_specs(d, 2) + [
            pl.BlockSpec((1, d), lambda i, j: (0, 0)),
            pl.BlockSpec((d, tn), lambda i, j: (0, j)),
            pl.BlockSpec((d, nr), lambda i, j: (0, 0)),
        ],
        out_specs=[pl.BlockSpec((TM, tn), lambda i, j: (i, j)),
                   pl.BlockSpec((TM, nr), lambda i, j: (i, 0))],
        out_shape=[jax.ShapeDtypeStruct((rows, n), _BF16),
                   jax.ShapeDtypeStruct((rows, nr), _BF16)],
        scratch_shapes=[pltpu.VMEM((TM, d), _BF16)],
        compiler_params=_params("parallel", "arbitrary"),
        name="in_proj",
    )(head, x, gain.reshape(1, d), w, w_r)


def _matmul_residual_kernel(a_ref, w_ref, h_ref, o_ref):
    o_ref[...] = h_ref[...] + _dot(a_ref[...], w_ref[...])


def _matmul_residual(a, w, h):
    rows, k = a.shape
    d = w.shape[1]
    return pl.pallas_call(
        _matmul_residual_kernel,
        grid=(rows // TM,),
        in_specs=[
            pl.BlockSpec((TM, k), lambda i: (i, 0)),
            pl.BlockSpec((k, d), lambda i: (0, 0)),
            pl.BlockSpec((TM, d), lambda i: (i, 0)),
        ],
        out_specs=pl.BlockSpec((TM, d), lambda i: (i, 0)),
        out_shape=jax.ShapeDtypeStruct((rows, d), _F32),
        compiler_params=_params("parallel"),
        name="matmul_residual",
    )(a, w, h)


def _mlp_kernel(h_ref, g_ref, wu_ref, wd_ref, gf_ref, o_ref, xn_ref, *, final):
    j = pl.program_id(1)

    @pl.when(j == 0)
    def _():
        x = h_ref[...]
        xn_ref[...] = (x * _rms_scale(x) * g_ref[...]).astype(_BF16)
        o_ref[...] = x

    u = jnp.maximum(_dot(xn_ref[...], wu_ref[...]), 0.0)
    o_ref[...] += _dot((u * u).astype(_BF16), wd_ref[...])

    if final:
        @pl.when(j == pl.num_programs(1) - 1)
        def _():
            y = o_ref[...]
            o_ref[...] = y * _rms_scale(y) * gf_ref[...]


def _mlp(h, gain, w_up, w_down, layer, tf, final_gain=None):
    rows, d = h.shape
    ff = w_up.shape[2]
    final = final_gain is not None
    skip = FRONT // TM if final else 0
    out_rows = rows - skip * TM
    gf = (final_gain if final else gain).reshape(1, d)
    return pl.pallas_call(
        functools.partial(_mlp_kernel, final=final),
        grid=(out_rows // TM, ff // tf),
        in_specs=[
            pl.BlockSpec((TM, d), lambda i, j: (i + skip, 0)),
            pl.BlockSpec((1, d), lambda i, j: (0, 0)),
            pl.BlockSpec((None, d, tf), lambda i, j: (layer, 0, j)),
            pl.BlockSpec((None, tf, d), lambda i, j: (layer, j, 0)),
            pl.BlockSpec((1, d), lambda i, j: (0, 0)),
        ],
        out_specs=pl.BlockSpec((TM, d), lambda i, j: (i, 0)),
        out_shape=jax.ShapeDtypeStruct((out_rows, d), _F32),
        scratch_shapes=[pltpu.VMEM((TM, d), _BF16)],
        compiler_params=_params("parallel", "arbitrary"),
        name="mlp_final" if final else "mlp",
    )(h, gain.reshape(1, d), w_up, w_down, gf)


def _gla_kernel(q_ref, k_ref, v_ref, r_ref, wg_ref, bg_ref, o_ref, s_ref, b_ref):
    step = pl.program_id(0)
    c = GLA_CHUNK
    nsub = c // GLA_SUB

    @pl.when(step == 0)
    def _():
        s_ref[...] = jnp.zeros_like(s_ref)
        b_ref[0] = jnp.zeros((c, GLA_QK), _F32)

    b_all = b_ref[step % 2]

    causal = [lax.broadcasted_iota(jnp.int32, (GLA_SUB, c), 1)
              <= lax.broadcasted_iota(jnp.int32, (GLA_SUB, c), 0) + i * GLA_SUB for i in range(nsub)]

    def rows_of(t, i):
        return t[i * GLA_SUB:(i + 1) * GLA_SUB]

    def scaled(t, e):
        return t * jnp.exp2(e).astype(_BF16)

    q_shift = math.log2(GLA_DK ** -0.5)

    def attention_weights(h):
        ks = slice(h * GLA_DK, (h + 1) * GLA_DK)
        b = b_all[:, ks]
        beta = [jnp.zeros((1, GLA_DK), _F32)] + [b[i * GLA_SUB - 1:i * GLA_SUB, :] for i in range(1, nsub + 1)]
        q_exp = jnp.concatenate([rows_of(b, i) - beta[i] for i in range(nsub)], axis=0)
        k_exp = jnp.concatenate([beta[i + 1] - rows_of(b, i) for i in range(nsub)], axis=0)
        qs = q_ref[:, ks] * jnp.exp2(q_exp + q_shift).astype(_BF16)
        kd = k_ref[:, ks] * jnp.exp2(k_exp).astype(_BF16)
        a_rows = []
        for i in range(nsub):
            kt = jnp.concatenate([scaled(rows_of(kd, j), beta[i] - beta[j + 1]) for j in range(i + 1)]
                                 + [rows_of(kd, j) for j in range(i + 1, nsub)], axis=0)
            a_rows.append(jnp.where(causal[i], _dot_nt(rows_of(qs, i), kt), 0.0).astype(_BF16))
        qe = jnp.concatenate([scaled(rows_of(qs, i), beta[i]) for i in range(nsub)], axis=0)
        k_dec = jnp.concatenate([scaled(rows_of(kd, j), beta[nsub] - beta[j + 1]) for j in range(nsub)], axis=0)
        decay = jnp.exp2(jnp.broadcast_to(beta[nsub], (LANES, GLA_DK))).T
        return jnp.concatenate([jnp.concatenate(a_rows, axis=0), qe], axis=1), k_dec, decay

    def outputs(h, lhs, k_dec, decay):
        vs = slice(h * GLA_DV, (h + 1) * GLA_DV)
        vh = v_ref[:, vs]
        state = s_ref[h]
        o_ref[:, vs] = _dot(lhs, jnp.concatenate([vh, state.astype(_BF16)], axis=0)).astype(o_ref.dtype)
        s_ref[h] = state * jnp.concatenate([decay] * (GLA_DV // LANES), axis=1) + _dot_tn(k_dec, vh)

    pending = None
    for h in range(GLA_HEADS):
        ready = attention_weights(h)
        if pending is not None:
            outputs(h - 1, *pending)
        pending = ready
    outputs(GLA_HEADS - 1, *pending)

    x = _dot(r_ref[...], wg_ref[...]) + bg_ref[...]
    gk = (jnp.minimum(x, 0.0) - jnp.log1p(jnp.exp(-jnp.abs(x)))) * (LOG2E / GLA_GATE_NORM)
    row = (step + 1) * c + lax.broadcasted_iota(jnp.int32, (c, 1), 0)
    gk = jnp.where(row >= META_ROW0, gk, 0.0)
    tri = (lax.broadcasted_iota(jnp.int32, (c, c), 0)
           >= lax.broadcasted_iota(jnp.int32, (c, c), 1)).astype(_BF16)
    hi = gk.astype(_BF16)
    lo = (gk - hi.astype(_F32)).astype(_BF16)
    b_ref[(step + 1) % 2] = _dot(tri, hi) + _dot(tri, lo)


def _gla(proj, r, w_gate, b_gate):
    rows = proj.shape[0]
    c = GLA_CHUNK
    last = rows // c - 1
    return pl.pallas_call(
        _gla_kernel,
        grid=(rows // c,),
        in_specs=[
            pl.BlockSpec((c, GLA_QK), lambda s: (s, 0)),
            pl.BlockSpec((c, GLA_QK), lambda s: (s, 1)),
            pl.BlockSpec((c, GLA_VD), lambda s: (s, 1)),
            pl.BlockSpec((c, RANK_PAD), lambda s: (jnp.minimum(s + 1, last), 0)),
            pl.BlockSpec((RANK_PAD, GLA_QK), lambda s: (0, 0)),
            pl.BlockSpec((1, GLA_QK), lambda s: (0, 0)),
        ],
        out_specs=pl.BlockSpec((c, GLA_VD), lambda s: (s, 0)),
        out_shape=jax.ShapeDtypeStruct((rows, GLA_VD), _BF16),
        scratch_shapes=[pltpu.VMEM((GLA_HEADS, GLA_DK, GLA_DV), _F32),
                        pltpu.VMEM((2, c, GLA_QK), _F32)],
        compiler_params=_params("arbitrary"),
        name="gla",
    )(proj, proj, proj, r, w_gate, b_gate.reshape(1, GLA_QK))


def _gla_out_kernel(o_ref, g_ref, no_ref, w_ref, head_ref, x_ref, out_ref):
    parts = []
    for h in range(GLA_HEADS):
        vs = slice(h * GLA_DV, (h + 1) * GLA_DV)
        o = o_ref[:, vs].astype(_F32)
        hg = 0.5 * g_ref[:, vs].astype(_F32)
        silu = hg * jnp.tanh(hg) + hg
        parts.append((o * _rms_scale(o) * no_ref[...] * silu).astype(_BF16))
    y = jnp.concatenate(parts, axis=1)
    out_ref[...] = _stream_tile(head_ref, x_ref, pl.program_id(0)) + _dot(y, w_ref[...])


def _gla_out(o, proj, norm_out, w, head, x):
    rows, k = o.shape
    d = w.shape[1]
    return pl.pallas_call(
        _gla_out_kernel,
        grid=(rows // TM,),
        in_specs=[
            pl.BlockSpec((TM, k), lambda i: (i, 0)),
            pl.BlockSpec((TM, GLA_VD), lambda i: (i, 2)),
            pl.BlockSpec((1, GLA_DV), lambda i: (0, 0)),
            pl.BlockSpec((k, d), lambda i: (0, 0)),
        ] + _stream_specs(d, 1),
        out_specs=pl.BlockSpec((TM, d), lambda i: (i, 0)),
        out_shape=jax.ShapeDtypeStruct((rows, d), _F32),
        compiler_params=_params("parallel"),
        name="gla_out",
    )(o, proj, norm_out.reshape(1, GLA_DV), w, head, x)


def _swap_halves(x, width):
    lane = lax.broadcasted_iota(jnp.int32, x.shape, 1)
    first = (lane % (2 * width)) < width
    return jnp.where(first, pltpu.roll(x, LANES - width, 1), pltpu.roll(x, width, 1))


def _qkv_kernel(h_ref, gq_ref, gkv_ref, wq_ref, wkv_ref, q_ref, k_ref, v_ref):
    half = SWA_HEAD_DIM // 2
    x = h_ref[...]
    xhat = x * _rms_scale(x)
    q = _dot((xhat * gq_ref[...]).astype(_BF16), wq_ref[...])
    kv = _dot((xhat * gkv_ref[...]).astype(_BF16), wkv_ref[...])

    row = pl.program_id(0) * TM + lax.broadcasted_iota(jnp.int32, (TM, LANES), 0)
    lane = lax.broadcasted_iota(jnp.int32, (TM, LANES), 1)
    pos = (row - META_ROW0).astype(_F32)
    inv_freq = jnp.exp((lane % half).astype(_F32) * (-math.log(ROPE_THETA) / half))
    ang = pos * inv_freq
    cos = jnp.cos(ang)
    sin = jnp.where((lane % SWA_HEAD_DIM) < half, -jnp.sin(ang), jnp.sin(ang))

    def rope(t):
        return t * cos + _swap_halves(t, half) * sin

    qscale = SWA_HEAD_DIM ** -0.5 * LOG2E
    for p in range(SWA_Q_HEADS // 2):
        sl = slice(p * LANES, (p + 1) * LANES)
        q_ref[:, sl] = (rope(q[:, sl]) * qscale).astype(q_ref.dtype)

    low = (lane % LANES) < SWA_HEAD_DIM
    kv_width = SWA_KV_HEADS * SWA_HEAD_DIM
    for p in range(SWA_KV_HEADS // 2):
        sl = slice(p * LANES, (p + 1) * LANES)
        for t, o_ref in ((rope(kv[:, sl]), k_ref), (kv[:, kv_width + p * LANES:kv_width + (p + 1) * LANES], v_ref)):
            swapped = pltpu.roll(t, SWA_HEAD_DIM, 1)
            o_ref[:, (2 * p) * LANES:(2 * p + 1) * LANES] = jnp.where(low, t, swapped).astype(o_ref.dtype)
            o_ref[:, (2 * p + 1) * LANES:(2 * p + 2) * LANES] = jnp.where(low, swapped, t).astype(o_ref.dtype)


def _qkv(h, gain_q, gain_kv, w_q, w_kv):
    rows, d = h.shape
    nq = w_q.shape[1]
    nkv = w_kv.shape[1]
    dup = SWA_KV_HEADS * LANES
    return pl.pallas_call(
        _qkv_kernel,
        grid=(rows // TM,),
        in_specs=[
            pl.BlockSpec((TM, d), lambda i: (i, 0)),
            pl.BlockSpec((1, d), lambda i: (0, 0)),
            pl.BlockSpec((1, d), lambda i: (0, 0)),
            pl.BlockSpec((d, nq), lambda i: (0, 0)),
            pl.BlockSpec((d, nkv), lambda i: (0, 0)),
        ],
        out_specs=[
            pl.BlockSpec((TM, nq), lambda i: (i, 0)),
            pl.BlockSpec((TM, dup), lambda i: (i, 0)),
            pl.BlockSpec((TM, dup), lambda i: (i, 0)),
        ],
        out_shape=[
            jax.ShapeDtypeStruct((rows, nq), _BF16),
            jax.ShapeDtypeStruct((rows, dup), _BF16),
            jax.ShapeDtypeStruct((rows, dup), _BF16),
        ],
        compiler_params=_params("parallel"),
        name="qkv_rope",
    )(h, gain_q.reshape(1, d), gain_kv.reshape(1, d), w_q, w_kv)


def _swa_kernel(sink_ref, q_ref, kp_ref, kc_ref, km_ref, vp_ref, vc_ref, vm_ref, o_ref):
    n = pl.program_id(0)
    first_block = META_ROW0 // SWA_BLOCK
    meta_lo = META_ROW0 % SWA_BLOCK

    @pl.when(n < first_block)
    def _():
        o_ref[...] = jnp.zeros_like(o_ref)

    @pl.when(n >= first_block)
    def _():
        kb = jnp.concatenate([km_ref[meta_lo:, :], kp_ref[...], kc_ref[...], km_ref[meta_lo:, :]], axis=0)
        vb = jnp.concatenate([vm_ref[meta_lo:, :], vp_ref[...], vc_ref[...], vm_ref[meta_lo:, :]], axis=0)
        lane = lax.broadcasted_iota(jnp.int32, (SWA_HALF, LANES), 1)
        low = lane < SWA_HEAD_DIM
        high = lane >= SWA_HEAD_DIM

        qi0 = lax.broadcasted_iota(jnp.int32, (SWA_HALF, SWA_KEYS), 0)
        col = lax.broadcasted_iota(jnp.int32, (SWA_HALF, SWA_KEYS), 1)
        band_w = SWA_KEYS - N_META
        caps = []
        for hq in range(2):
            qi = qi0 + hq * SWA_HALF
            if hq == 0:
                is_meta = col < N_META
                kj = col - N_META
                m = col
                start = 0
            else:
                is_meta = col >= band_w
                kj = col + (2 * SWA_BLOCK - band_w)
                m = col - band_w
                start = 2 * N_META + 2 * SWA_BLOCK - SWA_KEYS
            band_ok = (kj > qi) & (kj <= qi + SWA_WINDOW) & ((n - 1) * SWA_BLOCK + kj >= FRONT)
            meta_ok = (META_ROW0 + m) <= n * SWA_BLOCK + qi
            ok = (is_meta & meta_ok) | (jnp.logical_not(is_meta) & band_ok)
            caps.append((jnp.where(ok, jnp.inf, NEG_INF).astype(_F32), start))

        ones = jnp.ones((SWA_KEYS, LANES), _BF16)

        def sink_tile(g):
            return jnp.concatenate(
                [jnp.full((SWA_HALF, LANES), sink_ref[g * SWA_GROUP + j] * LOG2E, _F32)
                 for j in range(SWA_GROUP)], axis=0)

        def scores(g, hq):
            cap, start = caps[hq]
            rows = slice(hq * SWA_HALF, (hq + 1) * SWA_HALF)
            parts = []
            for j in range(SWA_GROUP):
                pair = (g * SWA_GROUP + j) // 2
                t = q_ref[rows, pair * LANES:(pair + 1) * LANES]
                parts.append(jnp.where(low if j % 2 == 0 else high, t, jnp.zeros_like(t)))
            lhs = jnp.concatenate(parts, axis=0)
            s = _dot_nt(lhs, kb[start:start + SWA_KEYS, g * LANES:(g + 1) * LANES])
            s = jnp.concatenate(
                [jnp.minimum(s[j * SWA_HALF:(j + 1) * SWA_HALF], cap) for j in range(SWA_GROUP)], axis=0)
            row_max = jnp.max(s, axis=-1, keepdims=True)
            m = jnp.maximum(jnp.broadcast_to(row_max, (SWA_GROUP * SWA_HALF, LANES)), sink_tile(g))
            return s, m

        def probs(s, m, g):
            p = jnp.concatenate([jnp.exp2(s[:, c * LANES:(c + 1) * LANES] - m)
                                 for c in range(SWA_KEYS // LANES)], axis=1).astype(_BF16)
            return p, jnp.exp2(sink_tile(g) - m)

        def outputs(p, e, g, hq):
            _, start = caps[hq]
            rows = slice(hq * SWA_HALF, (hq + 1) * SWA_HALF)
            vw = jnp.concatenate([vb[start:start + SWA_KEYS, g * LANES:(g + 1) * LANES], ones], axis=1)
            od = _dot(p, vw)
            o = od[:, :LANES] / (od[:, LANES:] + e)
            for j in range(0, SWA_GROUP, 2):
                pair = (g * SWA_GROUP + j) // 2
                even = o[j * SWA_HALF:(j + 1) * SWA_HALF]
                odd = o[(j + 1) * SWA_HALF:(j + 2) * SWA_HALF]
                o_ref[rows, pair * LANES:(pair + 1) * LANES] = jnp.where(low, even, odd).astype(o_ref.dtype)

        tiles = [(g, hq) for g in range(SWA_KV_HEADS) for hq in range(2)]
        stage_a, stage_b = {}, {}
        for t in range(len(tiles) + 2):
            if t < len(tiles):
                stage_a[t] = scores(*tiles[t])
            if 0 <= t - 1 < len(tiles):
                stage_b[t - 1] = probs(*stage_a.pop(t - 1), tiles[t - 1][0])
            if 0 <= t - 2 < len(tiles):
                outputs(*stage_b.pop(t - 2), *tiles[t - 2])


def _swa(q, k_dup, v_dup, sinks):
    rows, nq = q.shape
    dup = k_dup.shape[1]
    first_block = META_ROW0 // SWA_BLOCK
    blk = pl.BlockSpec((SWA_BLOCK, dup), lambda n: (n, 0))
    prev = pl.BlockSpec((SWA_BLOCK, dup), lambda n: (jnp.maximum(n - 1, 0), 0))
    meta = pl.BlockSpec((SWA_BLOCK, dup), lambda n: (first_block, 0))
    return pl.pallas_call(
        _swa_kernel,
        grid=(rows // SWA_BLOCK,),
        in_specs=[
            pl.BlockSpec(memory_space=pltpu.SMEM),
            pl.BlockSpec((SWA_BLOCK, nq), lambda n: (n, 0)),
            prev, blk, meta, prev, blk, meta,
        ],
        out_specs=pl.BlockSpec((SWA_BLOCK, nq), lambda n: (n, 0)),
        out_shape=jax.ShapeDtypeStruct((rows, nq), _BF16),
        compiler_params=_params("parallel"),
        name="swa",
    )(sinks, q, k_dup, k_dup, k_dup, v_dup, v_dup, v_dup)


def kernel(x, meta_tokens, norm_mix, norm_mlp, w_mlp_up, w_mlp_down, a_w_in, a_w_gate_up, a_b_gate, a_norm_out, a_w_out, kv_norm, w_kv, b_w_q, b_sinks, b_w_out, norm_final):
    batch, seq, d = x.shape
    assert batch == 1 and d == D_MODEL and seq % TM == 0
    assert norm_mix.shape[0] == 2 and a_w_in.shape[0] == 1 and b_w_q.shape[0] == 1

    head = jnp.concatenate([jnp.zeros((META_ROW0, d), x.dtype), meta_tokens.astype(x.dtype)], axis=0)
    x2 = x[0]
    bf = lambda w: w.astype(_BF16)
    n_main = GLA_IN - GLA_RANK
    w_in = bf(a_w_in[0])
    w_r = jnp.pad(w_in[:, n_main:], ((0, 0), (0, RANK_PAD - GLA_RANK)))
    w_gate = jnp.pad(bf(a_w_gate_up[0]), ((0, RANK_PAD - GLA_RANK), (0, 0)))
    w_up, w_down = bf(w_mlp_up), bf(w_mlp_down)

    proj, r = _in_proj(head, x2, norm_mix[0], w_in[:, :n_main], w_r, tn=1024)
    o = _gla(proj, r, w_gate, a_b_gate[0])
    h = _gla_out(o, proj, a_norm_out[0], bf(a_w_out[0]), head, x2)
    h = _mlp(h, norm_mlp[0], w_up, w_down, 0, tf=1024)

    q, k_dup, v_dup = _qkv(h, norm_mix[1], kv_norm, bf(b_w_q[0]), bf(w_kv))
    o = _swa(q, k_dup, v_dup, b_sinks[0])
    h = _matmul_residual(o, bf(b_w_out[0]), h)
    return _mlp(h, norm_mlp[1], w_up, w_down, 1, tf=1024, final_gain=norm_final)[None]
```

```python
import functools
import math

import jax
import jax.numpy as jnp
from jax import lax
from jax.experimental import pallas as pl
from jax.experimental.pallas import tpu as pltpu

D_MODEL = 2048
N_META = 16
D_FF = 4 * D_MODEL
RMS_EPS = 1e-6
NEG_INF = -1e30

GLA_HEADS = 4
GLA_DK = 256
GLA_DV = 512
GLA_QK = GLA_HEADS * GLA_DK
GLA_VD = GLA_HEADS * GLA_DV
GLA_RANK = 16
GLA_GATE_NORM = 16.0
GLA_IN = 2 * GLA_QK + 2 * GLA_VD + GLA_RANK

SWA_HEAD_DIM = 64
SWA_Q_HEADS = 32
SWA_KV_HEADS = 4
SWA_GROUP = SWA_Q_HEADS // SWA_KV_HEADS
SWA_WINDOW = 128
ROPE_THETA = 10000.0

LANES = 128
FRONT = 512
META_ROW0 = FRONT - N_META
TM = 512
IN_PROJ_TN = 1024
MLP_TF = 1024
GLA_CHUNK = 256
GLA_SUB = 64
SWA_BLOCK = 128
SWA_HALF = SWA_BLOCK // 2
SWA_KEYS = 2 * SWA_BLOCK
LOG2E = math.log2(math.e)
RANK_PAD = LANES
VMEM_LIMIT = 52 * 1024 * 1024

_F32 = jnp.float32
_BF16 = jnp.bfloat16


def _dot(a, b):
    return jnp.dot(a, b, preferred_element_type=_F32)


def _dot_nt(a, b):
    return lax.dot_general(a, b, (((1,), (1,)), ((), ())), preferred_element_type=_F32)


def _dot_tn(a, b):
    return lax.dot_general(a, b, (((0,), (0,)), ((), ())), preferred_element_type=_F32)


def _params(*semantics):
    return pltpu.CompilerParams(dimension_semantics=semantics, vmem_limit_bytes=VMEM_LIMIT)


def _rms_scale(x):
    return lax.rsqrt(jnp.mean(x * x, axis=-1, keepdims=True) + RMS_EPS)


def _stream_tile(head_ref, x_ref, i):
    return jnp.where(i == 0, head_ref[...], x_ref[...])


def _stream_specs(d, grid_rank):
    if grid_rank == 1:
        return [pl.BlockSpec((TM, d), lambda i: (0, 0)),
                pl.BlockSpec((TM, d), lambda i: (jnp.maximum(i - 1, 0), 0))]
    return [pl.BlockSpec((TM, d), lambda i, j: (0, 0)),
            pl.BlockSpec((TM, d), lambda i, j: (jnp.maximum(i - 1, 0), 0))]


def _in_proj_kernel(head_ref, x_ref, g_ref, w_ref, wr_ref, o_ref, r_ref, xn_ref):
    @pl.when(pl.program_id(1) == 0)
    def _():
        x = _stream_tile(head_ref, x_ref, pl.program_id(0))
        xn = (x * _rms_scale(x) * g_ref[...]).astype(_BF16)
        xn_ref[...] = xn
        r_ref[...] = _dot(xn, wr_ref[...]).astype(r_ref.dtype)

    o_ref[...] = _dot(xn_ref[...], w_ref[...]).astype(o_ref.dtype)


def _in_proj(head, x, gain, w, w_r):
    d = x.shape[1]
    rows = head.shape[0] + x.shape[0]
    n_tiles, _, tn = w.shape
    n = n_tiles * tn
    nr = w_r.shape[1]
    return pl.pallas_call(
        _in_proj_kernel,
        grid=(rows // TM, n_tiles),
        in_specs=_stream_specs(d, 2) + [
            pl.BlockSpec((1, d), lambda i, j: (0, 0)),
            pl.BlockSpec((None, d, tn), lambda i, j: (j, 0, 0)),
            pl.BlockSpec((d, nr), lambda i, j: (0, 0)),
        ],
        out_specs=[pl.BlockSpec((TM, tn), lambda i, j: (i, j)),
                   pl.BlockSpec((TM, nr), lambda i, j: (i, 0))],
        out_shape=[jax.ShapeDtypeStruct((rows, n), _BF16),
                   jax.ShapeDtypeStruct((rows, nr), _BF16)],
        scratch_shapes=[pltpu.VMEM((TM, d), _BF16)],
        compiler_params=_params("parallel", "arbitrary"),
        name="in_proj",
    )(head, x, gain.reshape(1, d), w, w_r)


def _matmul_residual_kernel(a_ref, w_ref, h_ref, o_ref):
    o_ref[...] = h_ref[...] + _dot(a_ref[...], w_ref[...])


def _matmul_residual(a, w, h):
    rows, k = a.shape
    d = w.shape[1]
    return pl.pallas_call(
        _matmul_residual_kernel,
        grid=(rows // TM,),
        in_specs=[
            pl.BlockSpec((TM, k), lambda i: (i, 0)),
            pl.BlockSpec((k, d), lambda i: (0, 0)),
            pl.BlockSpec((TM, d), lambda i: (i, 0)),
        ],
        out_specs=pl.BlockSpec((TM, d), lambda i: (i, 0)),
        out_shape=jax.ShapeDtypeStruct((rows, d), _F32),
        compiler_params=_params("parallel"),
        name="matmul_residual",
    )(a, w, h)


def _mlp_kernel(h_ref, g_ref, wu_ref, wd_ref, gf_ref, o_ref, xn_ref, *, final):
    j = pl.program_id(1)

    @pl.when(j == 0)
    def _():
        x = h_ref[...]
        xn_ref[...] = (x * _rms_scale(x) * g_ref[...]).astype(_BF16)
        o_ref[...] = x

    u = jnp.maximum(_dot(xn_ref[...], wu_ref[...]), 0.0)
    o_ref[...] += _dot((u * u).astype(_BF16), wd_ref[...])

    if final:
        @pl.when(j == pl.num_programs(1) - 1)
        def _():
            y = o_ref[...]
            o_ref[...] = y * _rms_scale(y) * gf_ref[...]


def _mlp(h, gain, w_up, w_down, final_gain=None):
    rows, d = h.shape
    n_tiles, _, tf = w_up.shape
    final = final_gain is not None
    skip = FRONT // TM if final else 0
    out_rows = rows - skip * TM
    gf = (final_gain if final else gain).reshape(1, d)
    return pl.pallas_call(
        functools.partial(_mlp_kernel, final=final),
        grid=(out_rows // TM, n_tiles),
        in_specs=[
            pl.BlockSpec((TM, d), lambda i, j: (i + skip, 0)),
            pl.BlockSpec((1, d), lambda i, j: (0, 0)),
            pl.BlockSpec((None, d, tf), lambda i, j: (j, 0, 0)),
            pl.BlockSpec((tf, d), lambda i, j: (j, 0)),
            pl.BlockSpec((1, d), lambda i, j: (0, 0)),
        ],
        out_specs=pl.BlockSpec((TM, d), lambda i, j: (i, 0)),
        out_shape=jax.ShapeDtypeStruct((out_rows, d), _F32),
        scratch_shapes=[pltpu.VMEM((TM, d), _BF16)],
        compiler_params=_params("parallel", "arbitrary"),
        name="mlp_final" if final else "mlp",
    )(h, gain.reshape(1, d), w_up, w_down, gf)


def _cast_weights(wu_src_ref, wd_src_ref, wu_ref, wd_ref):
    for t in range(wu_ref.shape[0]):
        wu_ref[t] = wu_src_ref[:, t * MLP_TF:(t + 1) * MLP_TF].astype(_BF16)
    wd_ref[...] = wd_src_ref[...].astype(_BF16)


def _cast_weights_specs(w_up, w_down, layer, steps):
    _, d, ff = w_up.shape
    used = 1 << (steps.bit_length() - 1)
    ru, rd = d // used, ff // used
    clamp = lambda s: jnp.minimum(s, used - 1)
    in_specs = [pl.BlockSpec((None, ru, ff), lambda s: (layer, clamp(s), 0)),
                pl.BlockSpec((None, rd, d), lambda s: (layer, clamp(s), 0))]
    out_specs = [pl.BlockSpec((ff // MLP_TF, ru, MLP_TF), lambda s: (0, clamp(s), 0)),
                 pl.BlockSpec((rd, d), lambda s: (clamp(s), 0))]
    out_shapes = [jax.ShapeDtypeStruct((ff // MLP_TF, d, MLP_TF), _BF16),
                  jax.ShapeDtypeStruct((ff, d), _BF16)]
    return in_specs, out_specs, out_shapes


def _gla_kernel(q_ref, k_ref, v_ref, r_ref, wg_ref, bg_ref, wu_src_ref, wd_src_ref,
                o_ref, wu_ref, wd_ref, s_ref, b_ref):
    step = pl.program_id(0)
    _cast_weights(wu_src_ref, wd_src_ref, wu_ref, wd_ref)
    c = GLA_CHUNK
    nsub = c // GLA_SUB

    @pl.when(step == 0)
    def _():
        s_ref[...] = jnp.zeros_like(s_ref)
        b_ref[0] = jnp.zeros((c, GLA_QK), _F32)

    b_all = b_ref[step % 2]

    causal = [lax.broadcasted_iota(jnp.int32, (GLA_SUB, c), 1)
              <= lax.broadcasted_iota(jnp.int32, (GLA_SUB, c), 0) + i * GLA_SUB for i in range(nsub)]

    def rows_of(t, i):
        return t[i * GLA_SUB:(i + 1) * GLA_SUB]

    def scaled(t, e):
        return t * jnp.exp2(e).astype(_BF16)

    q_shift = math.log2(GLA_DK ** -0.5)

    def attention_weights(h):
        ks = slice(h * GLA_DK, (h + 1) * GLA_DK)
        b = b_all[:, ks]
        beta = [jnp.zeros((1, GLA_DK), _F32)] + [b[i * GLA_SUB - 1:i * GLA_SUB, :] for i in range(1, nsub + 1)]
        q_exp = jnp.concatenate([rows_of(b, i) - beta[i] for i in range(nsub)], axis=0)
        k_exp = jnp.concatenate([beta[i + 1] - rows_of(b, i) for i in range(nsub)], axis=0)
        qs = q_ref[:, ks] * jnp.exp2(q_exp + q_shift).astype(_BF16)
        kd = k_ref[:, ks] * jnp.exp2(k_exp).astype(_BF16)
        a_rows = []
        for i in range(nsub):
            kt = jnp.concatenate([scaled(rows_of(kd, j), beta[i] - beta[j + 1]) for j in range(i + 1)]
                                 + [rows_of(kd, j) for j in range(i + 1, nsub)], axis=0)
            a_rows.append(jnp.where(causal[i], _dot_nt(rows_of(qs, i), kt), 0.0).astype(_BF16))
        qe = jnp.concatenate([scaled(rows_of(qs, i), beta[i]) for i in range(nsub)], axis=0)
        k_dec = jnp.concatenate([scaled(rows_of(kd, j), beta[nsub] - beta[j + 1]) for j in range(nsub)], axis=0)
        decay = jnp.exp2(jnp.broadcast_to(beta[nsub], (LANES, GLA_DK))).T
        return jnp.concatenate([jnp.concatenate(a_rows, axis=0), qe], axis=1), k_dec, decay

    def outputs(h, lhs, k_dec, decay):
        vs = slice(h * GLA_DV, (h + 1) * GLA_DV)
        vh = v_ref[:, vs]
        state = s_ref[h]
        o_ref[:, vs] = _dot(lhs, jnp.concatenate([vh, state.astype(_BF16)], axis=0)).astype(o_ref.dtype)
        s_ref[h] = state * jnp.concatenate([decay] * (GLA_DV // LANES), axis=1) + _dot_tn(k_dec, vh)

    pending = None
    for h in range(GLA_HEADS):
        ready = attention_weights(h)
        if pending is not None:
            outputs(h - 1, *pending)
        pending = ready
    outputs(GLA_HEADS - 1, *pending)

    x = _dot(r_ref[...], wg_ref[...]) + bg_ref[...]
    gk = (jnp.minimum(x, 0.0) - jnp.log1p(jnp.exp(-jnp.abs(x)))) * (LOG2E / GLA_GATE_NORM)
    row = (step + 1) * c + lax.broadcasted_iota(jnp.int32, (c, 1), 0)
    gk = jnp.where(row >= META_ROW0, gk, 0.0)
    tri = (lax.broadcasted_iota(jnp.int32, (c, c), 0)
           >= lax.broadcasted_iota(jnp.int32, (c, c), 1)).astype(_BF16)
    hi = gk.astype(_BF16)
    lo = (gk - hi.astype(_F32)).astype(_BF16)
    b_ref[(step + 1) % 2] = _dot(tri, hi) + _dot(tri, lo)


def _gla(proj, r, w_gate, b_gate, w_up, w_down, layer):
    rows = proj.shape[0]
    c = GLA_CHUNK
    steps = rows // c
    last = steps - 1
    cast_in, cast_out, cast_shapes = _cast_weights_specs(w_up, w_down, layer, steps)
    return pl.pallas_call(
        _gla_kernel,
        grid=(steps,),
        in_specs=[
            pl.BlockSpec((c, GLA_QK), lambda s: (s, 0)),
            pl.BlockSpec((c, GLA_QK), lambda s: (s, 1)),
            pl.BlockSpec((c, GLA_VD), lambda s: (s, 1)),
            pl.BlockSpec((c, RANK_PAD), lambda s: (jnp.minimum(s + 1, last), 0)),
            pl.BlockSpec((RANK_PAD, GLA_QK), lambda s: (0, 0)),
            pl.BlockSpec((1, GLA_QK), lambda s: (0, 0)),
        ] + cast_in,
        out_specs=[pl.BlockSpec((c, GLA_VD), lambda s: (s, 0))] + cast_out,
        out_shape=[jax.ShapeDtypeStruct((rows, GLA_VD), _BF16)] + cast_shapes,
        scratch_shapes=[pltpu.VMEM((GLA_HEADS, GLA_DK, GLA_DV), _F32),
                        pltpu.VMEM((2, c, GLA_QK), _F32)],
        compiler_params=_params("arbitrary"),
        name="gla",
    )(proj, proj, proj, r, w_gate, b_gate.reshape(1, GLA_QK), w_up, w_down)


def _gla_out_kernel(o_ref, g_ref, no_ref, w_ref, head_ref, x_ref, out_ref):
    parts = []
    for h in range(GLA_HEADS):
        vs = slice(h * GLA_DV, (h + 1) * GLA_DV)
        o = o_ref[:, vs].astype(_F32)
        hg = 0.5 * g_ref[:, vs].astype(_F32)
        silu = hg * jnp.tanh(hg) + hg
        parts.append((o * _rms_scale(o) * no_ref[...] * silu).astype(_BF16))
    y = jnp.concatenate(parts, axis=1)
    out_ref[...] = _stream_tile(head_ref, x_ref, pl.program_id(0)) + _dot(y, w_ref[...])


def _gla_out(o, proj, norm_out, w, head, x):
    rows, k = o.shape
    d = w.shape[1]
    return pl.pallas_call(
        _gla_out_kernel,
        grid=(rows // TM,),
        in_specs=[
            pl.BlockSpec((TM, k), lambda i: (i, 0)),
            pl.BlockSpec((TM, GLA_VD), lambda i: (i, 2)),
            pl.BlockSpec((1, GLA_DV), lambda i: (0, 0)),
            pl.BlockSpec((k, d), lambda i: (0, 0)),
        ] + _stream_specs(d, 1),
        out_specs=pl.BlockSpec((TM, d), lambda i: (i, 0)),
        out_shape=jax.ShapeDtypeStruct((rows, d), _F32),
        compiler_params=_params("parallel"),
        name="gla_out",
    )(o, proj, norm_out.reshape(1, GLA_DV), w, head, x)


def _swap_halves(x, width):
    lane = lax.broadcasted_iota(jnp.int32, x.shape, 1)
    first = (lane % (2 * width)) < width
    return jnp.where(first, pltpu.roll(x, LANES - width, 1), pltpu.roll(x, width, 1))


def _qkv_kernel(h_ref, gq_ref, gkv_ref, wq_ref, wkv_ref, q_ref, k_ref, v_ref, tab_ref):
    half = SWA_HEAD_DIM // 2
    i = pl.program_id(0)

    def inv_freq(shape):
        lane = lax.broadcasted_iota(jnp.int32, shape, 1)
        return jnp.exp((lane % half).astype(_F32) * (-math.log(ROPE_THETA) / half))

    @pl.when(i == 0)
    def _():
        off = lax.broadcasted_iota(jnp.int32, (TM, LANES), 0).astype(_F32) * inv_freq((TM, LANES))
        tab_ref[0] = jnp.cos(off)
        tab_ref[1] = jnp.sin(off)

    x = h_ref[...]
    xhat = x * _rms_scale(x)
    q = _dot((xhat * gq_ref[...]).astype(_BF16), wq_ref[...])
    kv = _dot((xhat * gkv_ref[...]).astype(_BF16), wkv_ref[...])

    base = (i * TM - META_ROW0).astype(_F32) * inv_freq((8, LANES))[:1]
    cos_b, sin_b = jnp.cos(base), jnp.sin(base)
    cos = cos_b * tab_ref[0] - sin_b * tab_ref[1]
    sin = sin_b * tab_ref[0] + cos_b * tab_ref[1]
    lane = lax.broadcasted_iota(jnp.int32, (TM, LANES), 1)
    sin = jnp.where((lane % SWA_HEAD_DIM) < half, -sin, sin)

    def rope(t):
        return t * cos + _swap_halves(t, half) * sin

    qscale = SWA_HEAD_DIM ** -0.5 * LOG2E
    for p in range(SWA_Q_HEADS // 2):
        sl = slice(p * LANES, (p + 1) * LANES)
        q_ref[:, sl] = (rope(q[:, sl]) * qscale).astype(q_ref.dtype)

    low = (lane % LANES) < SWA_HEAD_DIM
    kv_width = SWA_KV_HEADS * SWA_HEAD_DIM
    for p in range(SWA_KV_HEADS // 2):
        sl = slice(p * LANES, (p + 1) * LANES)
        for t, o_ref in ((rope(kv[:, sl]), k_ref), (kv[:, kv_width + p * LANES:kv_width + (p + 1) * LANES], v_ref)):
            swapped = pltpu.roll(t, SWA_HEAD_DIM, 1)
            o_ref[:, (2 * p) * LANES:(2 * p + 1) * LANES] = jnp.where(low, t, swapped).astype(o_ref.dtype)
            o_ref[:, (2 * p + 1) * LANES:(2 * p + 2) * LANES] = jnp.where(low, swapped, t).astype(o_ref.dtype)


def _qkv(h, gain_q, gain_kv, w_q, w_kv):
    rows, d = h.shape
    nq = w_q.shape[1]
    nkv = w_kv.shape[1]
    dup = SWA_KV_HEADS * LANES
    return pl.pallas_call(
        _qkv_kernel,
        grid=(rows // TM,),
        in_specs=[
            pl.BlockSpec((TM, d), lambda i: (i, 0)),
            pl.BlockSpec((1, d), lambda i: (0, 0)),
            pl.BlockSpec((1, d), lambda i: (0, 0)),
            pl.BlockSpec((d, nq), lambda i: (0, 0)),
            pl.BlockSpec((d, nkv), lambda i: (0, 0)),
        ],
        out_specs=[
            pl.BlockSpec((TM, nq), lambda i: (i, 0)),
            pl.BlockSpec((TM, dup), lambda i: (i, 0)),
            pl.BlockSpec((TM, dup), lambda i: (i, 0)),
        ],
        out_shape=[
            jax.ShapeDtypeStruct((rows, nq), _BF16),
            jax.ShapeDtypeStruct((rows, dup), _BF16),
            jax.ShapeDtypeStruct((rows, dup), _BF16),
        ],
        scratch_shapes=[pltpu.VMEM((2, TM, LANES), _F32)],
        compiler_params=_params("arbitrary"),
        name="qkv_rope",
    )(h, gain_q.reshape(1, d), gain_kv.reshape(1, d), w_q, w_kv)


def _swa_kernel(sink_ref, q_ref, kp_ref, kc_ref, km_ref, vp_ref, vc_ref, vm_ref, wu_src_ref, wd_src_ref,
                o_ref, wu_ref, wd_ref):
    n = pl.program_id(0)
    _cast_weights(wu_src_ref, wd_src_ref, wu_ref, wd_ref)
    first_block = META_ROW0 // SWA_BLOCK
    meta_lo = META_ROW0 % SWA_BLOCK

    @pl.when(n < first_block)
    def _():
        o_ref[...] = jnp.zeros_like(o_ref)

    @pl.when(n >= first_block)
    def _():
        kb = jnp.concatenate([km_ref[meta_lo:, :], kp_ref[...], kc_ref[...], km_ref[meta_lo:, :]], axis=0)
        vb = jnp.concatenate([vm_ref[meta_lo:, :], vp_ref[...], vc_ref[...], vm_ref[meta_lo:, :]], axis=0)
        lane = lax.broadcasted_iota(jnp.int32, (SWA_HALF, LANES), 1)
        low = lane < SWA_HEAD_DIM
        high = lane >= SWA_HEAD_DIM

        qi0 = lax.broadcasted_iota(jnp.int32, (SWA_HALF, SWA_KEYS), 0)
        col = lax.broadcasted_iota(jnp.int32, (SWA_HALF, SWA_KEYS), 1)
        band_w = SWA_KEYS - N_META
        caps = []
        for hq in range(2):
            qi = qi0 + hq * SWA_HALF
            if hq == 0:
                is_meta = col < N_META
                kj = col - N_META
                m = col
                start = 0
            else:
                is_meta = col >= band_w
                kj = col + (2 * SWA_BLOCK - band_w)
                m = col - band_w
                start = 2 * N_META + 2 * SWA_BLOCK - SWA_KEYS
            band_ok = (kj > qi) & (kj <= qi + SWA_WINDOW) & ((n - 1) * SWA_BLOCK + kj >= FRONT)
            meta_ok = (META_ROW0 + m) <= n * SWA_BLOCK + qi
            ok = (is_meta & meta_ok) | (jnp.logical_not(is_meta) & band_ok)
            caps.append((jnp.where(ok, jnp.inf, NEG_INF).astype(_F32), start))

        ones = jnp.ones((SWA_KEYS, LANES), _BF16)

        def sink_tile(g):
            return jnp.concatenate(
                [jnp.full((SWA_HALF, LANES), sink_ref[g * SWA_GROUP + j] * LOG2E, _F32)
                 for j in range(SWA_GROUP)], axis=0)

        def scores(g, hq):
            cap, start = caps[hq]
            rows = slice(hq * SWA_HALF, (hq + 1) * SWA_HALF)
            parts = []
            for j in range(SWA_GROUP):
                pair = (g * SWA_GROUP + j) // 2
                t = q_ref[rows, pair * LANES:(pair + 1) * LANES]
                parts.append(jnp.where(low if j % 2 == 0 else high, t, jnp.zeros_like(t)))
            lhs = jnp.concatenate(parts, axis=0)
            s = _dot_nt(lhs, kb[start:start + SWA_KEYS, g * LANES:(g + 1) * LANES])
            s = jnp.concatenate(
                [jnp.minimum(s[j * SWA_HALF:(j + 1) * SWA_HALF], cap) for j in range(SWA_GROUP)], axis=0)
            row_max = jnp.max(s, axis=-1, keepdims=True)
            m = jnp.maximum(jnp.broadcast_to(row_max, (SWA_GROUP * SWA_HALF, LANES)), sink_tile(g))
            return s, m

        def probs(s, m, g):
            p = jnp.concatenate([jnp.exp2(s[:, c * LANES:(c + 1) * LANES] - m)
                                 for c in range(SWA_KEYS // LANES)], axis=1).astype(_BF16)
            return p, jnp.exp2(sink_tile(g) - m)

        def outputs(p, e, g, hq):
            _, start = caps[hq]
            rows = slice(hq * SWA_HALF, (hq + 1) * SWA_HALF)
            vw = jnp.concatenate([vb[start:start + SWA_KEYS, g * LANES:(g + 1) * LANES], ones], axis=1)
            od = _dot(p, vw)
            o = od[:, :LANES] / (od[:, LANES:] + e)
            for j in range(0, SWA_GROUP, 2):
                pair = (g * SWA_GROUP + j) // 2
                even = o[j * SWA_HALF:(j + 1) * SWA_HALF]
                odd = o[(j + 1) * SWA_HALF:(j + 2) * SWA_HALF]
                o_ref[rows, pair * LANES:(pair + 1) * LANES] = jnp.where(low, even, odd).astype(o_ref.dtype)

        tiles = [(g, hq) for g in range(SWA_KV_HEADS) for hq in range(2)]
        stage_a, stage_b = {}, {}
        for t in range(len(tiles) + 2):
            if t < len(tiles):
                stage_a[t] = scores(*tiles[t])
            if 0 <= t - 1 < len(tiles):
                stage_b[t - 1] = probs(*stage_a.pop(t - 1), tiles[t - 1][0])
            if 0 <= t - 2 < len(tiles):
                outputs(*stage_b.pop(t - 2), *tiles[t - 2])


def _swa(q, k_dup, v_dup, sinks, w_up, w_down, layer):
    rows, nq = q.shape
    steps = rows // SWA_BLOCK
    cast_in, cast_out, cast_shapes = _cast_weights_specs(w_up, w_down, layer, steps)
    dup = k_dup.shape[1]
    first_block = META_ROW0 // SWA_BLOCK
    blk = pl.BlockSpec((SWA_BLOCK, dup), lambda n: (n, 0))
    prev = pl.BlockSpec((SWA_BLOCK, dup), lambda n: (jnp.maximum(n - 1, 0), 0))
    meta = pl.BlockSpec((SWA_BLOCK, dup), lambda n: (first_block, 0))
    return pl.pallas_call(
        _swa_kernel,
        grid=(steps,),
        in_specs=[
            pl.BlockSpec(memory_space=pltpu.SMEM),
            pl.BlockSpec((SWA_BLOCK, nq), lambda n: (n, 0)),
            prev, blk, meta, prev, blk, meta,
        ] + cast_in,
        out_specs=[pl.BlockSpec((SWA_BLOCK, nq), lambda n: (n, 0))] + cast_out,
        out_shape=[jax.ShapeDtypeStruct((rows, nq), _BF16)] + cast_shapes,
        compiler_params=_params("arbitrary"),
        name="swa",
    )(sinks, q, k_dup, k_dup, k_dup, v_dup, v_dup, v_dup, w_up, w_down)


def kernel(x, meta_tokens, norm_mix, norm_mlp, w_mlp_up, w_mlp_down, a_w_in, a_w_gate_up, a_b_gate, a_norm_out, a_w_out, kv_norm, w_kv, b_w_q, b_sinks, b_w_out, norm_final):
    batch, seq, d = x.shape
    assert batch == 1 and d == D_MODEL and seq % TM == 0
    assert norm_mix.shape[0] == 2 and a_w_in.shape[0] == 1 and b_w_q.shape[0] == 1

    head = jnp.concatenate([jnp.zeros((META_ROW0, d), x.dtype), meta_tokens.astype(x.dtype)], axis=0)
    x2 = x[0]
    bf = lambda w: w.astype(_BF16)
    n_main = GLA_IN - GLA_RANK
    w_in = bf(a_w_in[0])
    w_in_tiles = w_in[:, :n_main].reshape(d, n_main // IN_PROJ_TN, IN_PROJ_TN).transpose(1, 0, 2)
    w_r = jnp.pad(w_in[:, n_main:], ((0, 0), (0, RANK_PAD - GLA_RANK)))
    w_gate = jnp.pad(bf(a_w_gate_up[0]), ((0, RANK_PAD - GLA_RANK), (0, 0)))

    proj, r = _in_proj(head, x2, norm_mix[0], w_in_tiles, w_r)
    o, w_up, w_down = _gla(proj, r, w_gate, a_b_gate[0], w_mlp_up, w_mlp_down, 0)
    h = _gla_out(o, proj, a_norm_out[0], bf(a_w_out[0]), head, x2)
    h = _mlp(h, norm_mlp[0], w_up, w_down)

    q, k_dup, v_dup = _qkv(h, norm_mix[1], kv_norm, bf(b_w_q[0]), bf(w_kv))
    o, w_up, w_down = _swa(q, k_dup, v_dup, b_sinks[0], w_mlp_up, w_mlp_down, 1)
    h = _matmul_residual(o, bf(b_w_out[0]), h)
    return _mlp(h, norm_mlp[1], w_up, w_down, final_gain=norm_final)[None]
```

```python
import functools
import math

import jax
import jax.numpy as jnp
from jax import lax
from jax.experimental import pallas as pl
from jax.experimental.pallas import tpu as pltpu

D_MODEL = 2048
N_META = 16
D_FF = 4 * D_MODEL
RMS_EPS = 1e-6
NEG_INF = -1e30

GLA_HEADS = 4
GLA_DK = 256
GLA_DV = 512
GLA_QK = GLA_HEADS * GLA_DK
GLA_VD = GLA_HEADS * GLA_DV
GLA_RANK = 16
GLA_GATE_NORM = 16.0
GLA_IN = 2 * GLA_QK + 2 * GLA_VD + GLA_RANK

SWA_HEAD_DIM = 64
SWA_Q_HEADS = 32
SWA_KV_HEADS = 4
SWA_GROUP = SWA_Q_HEADS // SWA_KV_HEADS
SWA_WINDOW = 128
ROPE_THETA = 10000.0

LANES = 128
BF16_ROWS = 16
FRONT = 512
META_ROW0 = FRONT - N_META
TM = 512
IN_PROJ_TN = 1024
MLP_TF = 1024
GLA_CHUNK = 256
GLA_SUB = 64
SWA_BLOCK = 128
SWA_HALF = SWA_BLOCK // 2
SWA_KEYS = 2 * SWA_BLOCK
LOG2E = math.log2(math.e)
RANK_PAD = LANES
VMEM_LIMIT = 52 * 1024 * 1024

_F32 = jnp.float32
_BF16 = jnp.bfloat16


def _dot(a, b):
    return jnp.dot(a, b, preferred_element_type=_F32)


def _dot_nt(a, b):
    return lax.dot_general(a, b, (((1,), (1,)), ((), ())), preferred_element_type=_F32)


def _dot_tn(a, b):
    return lax.dot_general(a, b, (((0,), (0,)), ((), ())), preferred_element_type=_F32)


def _params(*semantics):
    return pltpu.CompilerParams(dimension_semantics=semantics, vmem_limit_bytes=VMEM_LIMIT)


def _rms_scale(x):
    return lax.rsqrt(jnp.mean(x * x, axis=-1, keepdims=True) + RMS_EPS)


def _normed(x, g_ref):
    return (x * _rms_scale(x) * g_ref[...]).astype(_BF16)


def _norm_next_rows(src_ref, g_ref, xn_ref, slot, j, n_steps):
    slab = -(-TM // (n_steps * BF16_ROWS)) * BF16_ROWS
    start = pl.multiple_of(jnp.minimum(j * slab, TM - slab), BF16_ROWS)
    xn_ref[slot, pl.ds(start, slab), :] = _normed(src_ref[pl.ds(start, slab), :], g_ref)


def _stream_tile(head_ref, x_ref, i):
    return jnp.where(i == 0, head_ref[...], x_ref[...])


def _stream_specs(d, grid_rank):
    if grid_rank == 1:
        return [pl.BlockSpec((TM, d), lambda i: (0, 0)),
                pl.BlockSpec((TM, d), lambda i: (jnp.maximum(i - 1, 0), 0))]
    return [pl.BlockSpec((TM, d), lambda i, j: (0, 0)),
            pl.BlockSpec((TM, d), lambda i, j: (jnp.maximum(i - 1, 0), 0))]


def _in_proj_kernel(head_ref, xnext_ref, g_ref, w_ref, wr_ref, o_ref, r_ref, xn_ref, *, n_tiles):
    i, j = pl.program_id(0), pl.program_id(1)
    slot = i % 2

    @pl.when((i == 0) & (j == 0))
    def _():
        xn_ref[0] = _normed(head_ref[...], g_ref)

    def step(first):
        xn = xn_ref[slot]
        o_ref[...] = _dot(xn, w_ref[...]).astype(o_ref.dtype)
        if first:
            r_ref[...] = _dot(xn, wr_ref[...]).astype(r_ref.dtype)
        _norm_next_rows(xnext_ref, g_ref, xn_ref, 1 - slot, j, n_tiles)

    pl.when(j == 0)(functools.partial(step, True))
    pl.when(j > 0)(functools.partial(step, False))


def _in_proj(head, x, gain, w, n, w_r):
    d = x.shape[1]
    rows = head.shape[0] + x.shape[0]
    tn = IN_PROJ_TN
    n_tiles = n // tn
    nr = w_r.shape[1]
    last_x = x.shape[0] // TM - 1
    return pl.pallas_call(
        functools.partial(_in_proj_kernel, n_tiles=n_tiles),
        grid=(rows // TM, n_tiles),
        in_specs=[
            pl.BlockSpec((TM, d), lambda i, j: (0, 0)),
            pl.BlockSpec((TM, d), lambda i, j: (jnp.minimum(i, last_x), 0)),
            pl.BlockSpec((1, d), lambda i, j: (0, 0)),
            pl.BlockSpec((d, tn), lambda i, j: (0, j)),
            pl.BlockSpec((d, nr), lambda i, j: (0, 0)),
        ],
        out_specs=[pl.BlockSpec((TM, tn), lambda i, j: (i, j)),
                   pl.BlockSpec((TM, nr), lambda i, j: (i, 0))],
        out_shape=[jax.ShapeDtypeStruct((rows, n), _BF16),
                   jax.ShapeDtypeStruct((rows, nr), _BF16)],
        scratch_shapes=[pltpu.VMEM((2, TM, d), _BF16)],
        compiler_params=_params("arbitrary", "arbitrary"),
        name="in_proj",
    )(head, x, gain.reshape(1, d), w, w_r)


def _matmul_residual_kernel(a_ref, w_ref, h_ref, o_ref):
    o_ref[...] = h_ref[...] + _dot(a_ref[...], w_ref[...])


def _matmul_residual(a, w, h):
    rows, k = a.shape
    d = w.shape[1]
    return pl.pallas_call(
        _matmul_residual_kernel,
        grid=(rows // TM,),
        in_specs=[
            pl.BlockSpec((TM, k), lambda i: (i, 0)),
            pl.BlockSpec((k, d), lambda i: (0, 0)),
            pl.BlockSpec((TM, d), lambda i: (i, 0)),
        ],
        out_specs=pl.BlockSpec((TM, d), lambda i: (i, 0)),
        out_shape=jax.ShapeDtypeStruct((rows, d), _F32),
        compiler_params=_params("parallel"),
        name="matmul_residual",
    )(a, w, h)


def _mlp_kernel(h_ref, hnext_ref, g_ref, wu_ref, wd_ref, gf_ref, o_ref, xn_ref, *, n_tiles, final):
    i, j = pl.program_id(0), pl.program_id(1)
    slot = i % 2

    @pl.when((i == 0) & (j == 0))
    def _():
        xn_ref[0] = _normed(h_ref[...], g_ref)

    def step(first):
        u = jnp.maximum(_dot(xn_ref[slot], wu_ref[...]), 0.0)
        part = _dot((u * u).astype(_BF16), wd_ref[...])
        if first:
            o_ref[...] = h_ref[...] + part
        else:
            o_ref[...] += part
        _norm_next_rows(hnext_ref, g_ref, xn_ref, 1 - slot, j, n_tiles)

    pl.when(j == 0)(functools.partial(step, True))
    pl.when(j > 0)(functools.partial(step, False))

    if final:
        @pl.when(j == n_tiles - 1)
        def _():
            y = o_ref[...]
            o_ref[...] = y * _rms_scale(y) * gf_ref[...]


def _mlp(h, gain, w_up, w_down, final_gain=None):
    rows, d = h.shape
    n_tiles, _, tf = w_up.shape
    final = final_gain is not None
    skip = FRONT // TM if final else 0
    out_rows = rows - skip * TM
    last = rows // TM - 1
    gf = (final_gain if final else gain).reshape(1, d)
    return pl.pallas_call(
        functools.partial(_mlp_kernel, n_tiles=n_tiles, final=final),
        grid=(out_rows // TM, n_tiles),
        in_specs=[
            pl.BlockSpec((TM, d), lambda i, j: (i + skip, 0)),
            pl.BlockSpec((TM, d), lambda i, j: (jnp.minimum(i + skip + 1, last), 0)),
            pl.BlockSpec((1, d), lambda i, j: (0, 0)),
            pl.BlockSpec((None, d, tf), lambda i, j: (j, 0, 0)),
            pl.BlockSpec((tf, d), lambda i, j: (j, 0)),
            pl.BlockSpec((1, d), lambda i, j: (0, 0)),
        ],
        out_specs=pl.BlockSpec((TM, d), lambda i, j: (i, 0)),
        out_shape=jax.ShapeDtypeStruct((out_rows, d), _F32),
        scratch_shapes=[pltpu.VMEM((2, TM, d), _BF16)],
        compiler_params=_params("arbitrary", "arbitrary"),
        name="mlp_final" if final else "mlp",
    )(h, h, gain.reshape(1, d), w_up, w_down, gf)


def _cast_weights(wu_src_ref, wd_src_ref, wu_ref, wd_ref):
    for t in range(wu_ref.shape[0]):
        wu_ref[t] = wu_src_ref[:, t * MLP_TF:(t + 1) * MLP_TF].astype(_BF16)
    wd_ref[...] = wd_src_ref[...].astype(_BF16)


def _cast_weights_specs(w_up, w_down, layer, steps):
    _, d, ff = w_up.shape
    used = 1 << (steps.bit_length() - 1)
    ru, rd = d // used, ff // used
    clamp = lambda s: jnp.minimum(s, used - 1)
    in_specs = [pl.BlockSpec((None, ru, ff), lambda s: (layer, clamp(s), 0)),
                pl.BlockSpec((None, rd, d), lambda s: (layer, clamp(s), 0))]
    out_specs = [pl.BlockSpec((ff // MLP_TF, ru, MLP_TF), lambda s: (0, clamp(s), 0)),
                 pl.BlockSpec((rd, d), lambda s: (clamp(s), 0))]
    out_shapes = [jax.ShapeDtypeStruct((ff // MLP_TF, d, MLP_TF), _BF16),
                  jax.ShapeDtypeStruct((ff, d), _BF16)]
    return in_specs, out_specs, out_shapes


def _gla_kernel(q_ref, k_ref, v_ref, r_ref, wg_ref, bg_ref, wu_src_ref, wd_src_ref,
                o_ref, wu_ref, wd_ref, s_ref, b_ref):
    step = pl.program_id(0)
    _cast_weights(wu_src_ref, wd_src_ref, wu_ref, wd_ref)
    c = GLA_CHUNK
    nsub = c // GLA_SUB

    @pl.when(step == 0)
    def _():
        s_ref[...] = jnp.zeros_like(s_ref)
        b_ref[0] = jnp.zeros((c, GLA_QK), _F32)

    b_all = b_ref[step % 2]

    causal = [lax.broadcasted_iota(jnp.int32, (GLA_SUB, c), 1)
              <= lax.broadcasted_iota(jnp.int32, (GLA_SUB, c), 0) + i * GLA_SUB for i in range(nsub)]

    def rows_of(t, i):
        return t[i * GLA_SUB:(i + 1) * GLA_SUB]

    def scaled(t, e):
        return t * jnp.exp2(e).astype(_BF16)

    q_shift = math.log2(GLA_DK ** -0.5)

    def attention_weights(h):
        ks = slice(h * GLA_DK, (h + 1) * GLA_DK)
        b = b_all[:, ks]
        beta = [jnp.zeros((1, GLA_DK), _F32)] + [b[i * GLA_SUB - 1:i * GLA_SUB, :] for i in range(1, nsub + 1)]
        q_exp = jnp.concatenate([rows_of(b, i) - beta[i] for i in range(nsub)], axis=0)
        k_exp = jnp.concatenate([beta[i + 1] - rows_of(b, i) for i in range(nsub)], axis=0)
        qs = q_ref[:, ks] * jnp.exp2(q_exp + q_shift).astype(_BF16)
        kd = k_ref[:, ks] * jnp.exp2(k_exp).astype(_BF16)
        a_rows = []
        for i in range(nsub):
            kt = jnp.concatenate([scaled(rows_of(kd, j), beta[i] - beta[j + 1]) for j in range(i + 1)]
                                 + [rows_of(kd, j) for j in range(i + 1, nsub)], axis=0)
            a_rows.append(jnp.where(causal[i], _dot_nt(rows_of(qs, i), kt), 0.0).astype(_BF16))
        qe = jnp.concatenate([scaled(rows_of(qs, i), beta[i]) for i in range(nsub)], axis=0)
        k_dec = jnp.concatenate([scaled(rows_of(kd, j), beta[nsub] - beta[j + 1]) for j in range(nsub)], axis=0)
        decay = jnp.exp2(jnp.broadcast_to(beta[nsub], (LANES, GLA_DK))).T
        return jnp.concatenate([jnp.concatenate(a_rows, axis=0), qe], axis=1), k_dec, decay

    def outputs(h, lhs, k_dec, decay):
        vs = slice(h * GLA_DV, (h + 1) * GLA_DV)
        vh = v_ref[:, vs]
        state = s_ref[h]
        o_ref[:, vs] = _dot(lhs, jnp.concatenate([vh, state.astype(_BF16)], axis=0)).astype(o_ref.dtype)
        s_ref[h] = state * jnp.concatenate([decay] * (GLA_DV // LANES), axis=1) + _dot_tn(k_dec, vh)

    pending = None
    for h in range(GLA_HEADS):
        ready = attention_weights(h)
        if pending is not None:
            outputs(h - 1, *pending)
        pending = ready
    outputs(GLA_HEADS - 1, *pending)

    x = _dot(r_ref[...], wg_ref[...]) + bg_ref[...]
    gk = (jnp.minimum(x, 0.0) - jnp.log1p(jnp.exp(-jnp.abs(x)))) * (LOG2E / GLA_GATE_NORM)
    row = (step + 1) * c + lax.broadcasted_iota(jnp.int32, (c, 1), 0)
    gk = jnp.where(row >= META_ROW0, gk, 0.0)
    tri = (lax.broadcasted_iota(jnp.int32, (c, c), 0)
           >= lax.broadcasted_iota(jnp.int32, (c, c), 1)).astype(_BF16)
    hi = gk.astype(_BF16)
    lo = (gk - hi.astype(_F32)).astype(_BF16)
    b_ref[(step + 1) % 2] = _dot(tri, hi) + _dot(tri, lo)


def _gla(proj, r, w_gate, b_gate, w_up, w_down, layer):
    rows = proj.shape[0]
    c = GLA_CHUNK
    steps = rows // c
    last = steps - 1
    cast_in, cast_out, cast_shapes = _cast_weights_specs(w_up, w_down, layer, steps)
    return pl.pallas_call(
        _gla_kernel,
        grid=(steps,),
        in_specs=[
            pl.BlockSpec((c, GLA_QK), lambda s: (s, 0)),
            pl.BlockSpec((c, GLA_QK), lambda s: (s, 1)),
            pl.BlockSpec((c, GLA_VD), lambda s: (s, 1)),
            pl.BlockSpec((c, RANK_PAD), lambda s: (jnp.minimum(s + 1, last), 0)),
            pl.BlockSpec((RANK_PAD, GLA_QK), lambda s: (0, 0)),
            pl.BlockSpec((1, GLA_QK), lambda s: (0, 0)),
        ] + cast_in,
        out_specs=[pl.BlockSpec((c, GLA_VD), lambda s: (s, 0))] + cast_out,
        out_shape=[jax.ShapeDtypeStruct((rows, GLA_VD), _BF16)] + cast_shapes,
        scratch_shapes=[pltpu.VMEM((GLA_HEADS, GLA_DK, GLA_DV), _F32),
                        pltpu.VMEM((2, c, GLA_QK), _F32)],
        compiler_params=_params("arbitrary"),
        name="gla",
    )(proj, proj, proj, r, w_gate, b_gate.reshape(1, GLA_QK), w_up, w_down)


def _gla_out_kernel(o_ref, g_ref, no_ref, w_ref, head_ref, x_ref, out_ref):
    parts = []
    for h in range(GLA_HEADS):
        vs = slice(h * GLA_DV, (h + 1) * GLA_DV)
        o = o_ref[:, vs].astype(_F32)
        hg = 0.5 * g_ref[:, vs].astype(_F32)
        silu = hg * jnp.tanh(hg) + hg
        parts.append((o * _rms_scale(o) * no_ref[...] * silu).astype(_BF16))
    y = jnp.concatenate(parts, axis=1)
    out_ref[...] = _stream_tile(head_ref, x_ref, pl.program_id(0)) + _dot(y, w_ref[...])


def _gla_out(o, proj, norm_out, w, head, x):
    rows, k = o.shape
    d = w.shape[1]
    return pl.pallas_call(
        _gla_out_kernel,
        grid=(rows // TM,),
        in_specs=[
            pl.BlockSpec((TM, k), lambda i: (i, 0)),
            pl.BlockSpec((TM, GLA_VD), lambda i: (i, 2)),
            pl.BlockSpec((1, GLA_DV), lambda i: (0, 0)),
            pl.BlockSpec((k, d), lambda i: (0, 0)),
        ] + _stream_specs(d, 1),
        out_specs=pl.BlockSpec((TM, d), lambda i: (i, 0)),
        out_shape=jax.ShapeDtypeStruct((rows, d), _F32),
        compiler_params=_params("parallel"),
        name="gla_out",
    )(o, proj, norm_out.reshape(1, GLA_DV), w, head, x)


def _swap_halves(x, width):
    lane = lax.broadcasted_iota(jnp.int32, x.shape, 1)
    first = (lane % (2 * width)) < width
    return jnp.where(first, pltpu.roll(x, LANES - width, 1), pltpu.roll(x, width, 1))


def _qkv_kernel(h_ref, gq_ref, gkv_ref, wq_ref, wkv_ref, q_ref, k_ref, v_ref, tab_ref):
    half = SWA_HEAD_DIM // 2
    i = pl.program_id(0)

    def inv_freq(shape):
        lane = lax.broadcasted_iota(jnp.int32, shape, 1)
        return jnp.exp((lane % half).astype(_F32) * (-math.log(ROPE_THETA) / half))

    @pl.when(i == 0)
    def _():
        off = lax.broadcasted_iota(jnp.int32, (TM, LANES), 0).astype(_F32) * inv_freq((TM, LANES))
        tab_ref[0] = jnp.cos(off)
        tab_ref[1] = jnp.sin(off)

    x = h_ref[...]
    xhat = x * _rms_scale(x)
    q = _dot((xhat * gq_ref[...]).astype(_BF16), wq_ref[...])
    kv = _dot((xhat * gkv_ref[...]).astype(_BF16), wkv_ref[...])

    base = (i * TM - META_ROW0).astype(_F32) * inv_freq((8, LANES))[:1]
    cos_b, sin_b = jnp.cos(base), jnp.sin(base)
    cos = cos_b * tab_ref[0] - sin_b * tab_ref[1]
    sin = sin_b * tab_ref[0] + cos_b * tab_ref[1]
    lane = lax.broadcasted_iota(jnp.int32, (TM, LANES), 1)
    sin = jnp.where((lane % SWA_HEAD_DIM) < half, -sin, sin)

    def rope(t):
        return t * cos + _swap_halves(t, half) * sin

    qscale = SWA_HEAD_DIM ** -0.5 * LOG2E
    for p in range(SWA_Q_HEADS // 2):
        sl = slice(p * LANES, (p + 1) * LANES)
        q_ref[:, sl] = (rope(q[:, sl]) * qscale).astype(q_ref.dtype)

    low = (lane % LANES) < SWA_HEAD_DIM
    kv_width = SWA_KV_HEADS * SWA_HEAD_DIM
    for p in range(SWA_KV_HEADS // 2):
        sl = slice(p * LANES, (p + 1) * LANES)
        for t, o_ref in ((rope(kv[:, sl]), k_ref), (kv[:, kv_width + p * LANES:kv_width + (p + 1) * LANES], v_ref)):
            swapped = pltpu.roll(t, SWA_HEAD_DIM, 1)
            o_ref[:, (2 * p) * LANES:(2 * p + 1) * LANES] = jnp.where(low, t, swapped).astype(o_ref.dtype)
            o_ref[:, (2 * p + 1) * LANES:(2 * p + 2) * LANES] = jnp.where(low, swapped, t).astype(o_ref.dtype)


def _qkv(h, gain_q, gain_kv, w_q, w_kv):
    rows, d = h.shape
    nq = w_q.shape[1]
    nkv = w_kv.shape[1]
    dup = SWA_KV_HEADS * LANES
    return pl.pallas_call(
        _qkv_kernel,
        grid=(rows // TM,),
        in_specs=[
            pl.BlockSpec((TM, d), lambda i: (i, 0)),
            pl.BlockSpec((1, d), lambda i: (0, 0)),
            pl.BlockSpec((1, d), lambda i: (0, 0)),
            pl.BlockSpec((d, nq), lambda i: (0, 0)),
            pl.BlockSpec((d, nkv), lambda i: (0, 0)),
        ],
        out_specs=[
            pl.BlockSpec((TM, nq), lambda i: (i, 0)),
            pl.BlockSpec((TM, dup), lambda i: (i, 0)),
            pl.BlockSpec((TM, dup), lambda i: (i, 0)),
        ],
        out_shape=[
            jax.ShapeDtypeStruct((rows, nq), _BF16),
            jax.ShapeDtypeStruct((rows, dup), _BF16),
            jax.ShapeDtypeStruct((rows, dup), _BF16),
        ],
        scratch_shapes=[pltpu.VMEM((2, TM, LANES), _F32)],
        compiler_params=_params("arbitrary"),
        name="qkv_rope",
    )(h, gain_q.reshape(1, d), gain_kv.reshape(1, d), w_q, w_kv)


def _swa_kernel(sink_ref, q_ref, kp_ref, kc_ref, km_ref, vp_ref, vc_ref, vm_ref, wu_src_ref, wd_src_ref,
                o_ref, wu_ref, wd_ref):
    n = pl.program_id(0)
    _cast_weights(wu_src_ref, wd_src_ref, wu_ref, wd_ref)
    first_block = META_ROW0 // SWA_BLOCK
    meta_lo = META_ROW0 % SWA_BLOCK

    @pl.when(n < first_block)
    def _():
        o_ref[...] = jnp.zeros_like(o_ref)

    @pl.when(n >= first_block)
    def _():
        kb = jnp.concatenate([km_ref[meta_lo:, :], kp_ref[...], kc_ref[...], km_ref[meta_lo:, :]], axis=0)
        vb = jnp.concatenate([vm_ref[meta_lo:, :], vp_ref[...], vc_ref[...], vm_ref[meta_lo:, :]], axis=0)
        lane = lax.broadcasted_iota(jnp.int32, (SWA_HALF, LANES), 1)
        low = lane < SWA_HEAD_DIM
        high = lane >= SWA_HEAD_DIM

        qi0 = lax.broadcasted_iota(jnp.int32, (SWA_HALF, SWA_KEYS), 0)
        col = lax.broadcasted_iota(jnp.int32, (SWA_HALF, SWA_KEYS), 1)
        band_w = SWA_KEYS - N_META
        caps = []
        for hq in range(2):
            qi = qi0 + hq * SWA_HALF
            if hq == 0:
                is_meta = col < N_META
                kj = col - N_META
                m = col
                start = 0
            else:
                is_meta = col >= band_w
                kj = col + (2 * SWA_BLOCK - band_w)
                m = col - band_w
                start = 2 * N_META + 2 * SWA_BLOCK - SWA_KEYS
            band_ok = (kj > qi) & (kj <= qi + SWA_WINDOW) & ((n - 1) * SWA_BLOCK + kj >= FRONT)
            meta_ok = (META_ROW0 + m) <= n * SWA_BLOCK + qi
            ok = (is_meta & meta_ok) | (jnp.logical_not(is_meta) & band_ok)
            caps.append((jnp.where(ok, jnp.inf, NEG_INF).astype(_F32), start))

        ones = jnp.ones((SWA_KEYS, LANES), _BF16)

        def sink_tile(g):
            return jnp.concatenate(
                [jnp.full((SWA_HALF, LANES), sink_ref[g * SWA_GROUP + j] * LOG2E, _F32)
                 for j in range(SWA_GROUP)], axis=0)

        def scores(g, hq):
            cap, start = caps[hq]
            rows = slice(hq * SWA_HALF, (hq + 1) * SWA_HALF)
            parts = []
            for j in range(SWA_GROUP):
                pair = (g * SWA_GROUP + j) // 2
                t = q_ref[rows, pair * LANES:(pair + 1) * LANES]
                parts.append(jnp.where(low if j % 2 == 0 else high, t, jnp.zeros_like(t)))
            lhs = jnp.concatenate(parts, axis=0)
            s = _dot_nt(lhs, kb[start:start + SWA_KEYS, g * LANES:(g + 1) * LANES])
            s = jnp.concatenate(
                [jnp.minimum(s[j * SWA_HALF:(j + 1) * SWA_HALF], cap) for j in range(SWA_GROUP)], axis=0)
            row_max = jnp.max(s, axis=-1, keepdims=True)
            m = jnp.maximum(jnp.broadcast_to(row_max, (SWA_GROUP * SWA_HALF, LANES)), sink_tile(g))
            return s, m

        def probs(s, m, g):
            p = jnp.concatenate([jnp.exp2(s[:, c * LANES:(c + 1) * LANES] - m)
                                 for c in range(SWA_KEYS // LANES)], axis=1).astype(_BF16)
            return p, jnp.exp2(sink_tile(g) - m)

        def outputs(p, e, g, hq):
            _, start = caps[hq]
            rows = slice(hq * SWA_HALF, (hq + 1) * SWA_HALF)
            vw = jnp.concatenate([vb[start:start + SWA_KEYS, g * LANES:(g + 1) * LANES], ones], axis=1)
            od = _dot(p, vw)
            o = od[:, :LANES] / (od[:, LANES:] + e)
            for j in range(0, SWA_GROUP, 2):
                pair = (g * SWA_GROUP + j) // 2
                even = o[j * SWA_HALF:(j + 1) * SWA_HALF]
                odd = o[(j + 1) * SWA_HALF:(j + 2) * SWA_HALF]
                o_ref[rows, pair * LANES:(pair + 1) * LANES] = jnp.where(low, even, odd).astype(o_ref.dtype)

        tiles = [(g, hq) for g in range(SWA_KV_HEADS) for hq in range(2)]
        stage_a, stage_b = {}, {}
        for t in range(len(tiles) + 2):
            if t < len(tiles):
                stage_a[t] = scores(*tiles[t])
            if 0 <= t - 1 < len(tiles):
                stage_b[t - 1] = probs(*stage_a.pop(t - 1), tiles[t - 1][0])
            if 0 <= t - 2 < len(tiles):
                outputs(*stage_b.pop(t - 2), *tiles[t - 2])


def _swa(q, k_dup, v_dup, sinks, w_up, w_down, layer):
    rows, nq = q.shape
    steps = rows // SWA_BLOCK
    cast_in, cast_out, cast_shapes = _cast_weights_specs(w_up, w_down, layer, steps)
    dup = k_dup.shape[1]
    first_block = META_ROW0 // SWA_BLOCK
    blk = pl.BlockSpec((SWA_BLOCK, dup), lambda n: (n, 0))
    prev = pl.BlockSpec((SWA_BLOCK, dup), lambda n: (jnp.maximum(n - 1, 0), 0))
    meta = pl.BlockSpec((SWA_BLOCK, dup), lambda n: (first_block, 0))
    return pl.pallas_call(
        _swa_kernel,
        grid=(steps,),
        in_specs=[
            pl.BlockSpec(memory_space=pltpu.SMEM),
            pl.BlockSpec((SWA_BLOCK, nq), lambda n: (n, 0)),
            prev, blk, meta, prev, blk, meta,
        ] + cast_in,
        out_specs=[pl.BlockSpec((SWA_BLOCK, nq), lambda n: (n, 0))] + cast_out,
        out_shape=[jax.ShapeDtypeStruct((rows, nq), _BF16)] + cast_shapes,
        compiler_params=_params("arbitrary"),
        name="swa",
    )(sinks, q, k_dup, k_dup, k_dup, v_dup, v_dup, v_dup, w_up, w_down)


def kernel(x, meta_tokens, norm_mix, norm_mlp, w_mlp_up, w_mlp_down, a_w_in, a_w_gate_up, a_b_gate, a_norm_out, a_w_out, kv_norm, w_kv, b_w_q, b_sinks, b_w_out, norm_final):
    batch, seq, d = x.shape
    assert batch == 1 and d == D_MODEL and seq % TM == 0
    assert norm_mix.shape[0] == 2 and a_w_in.shape[0] == 1 and b_w_q.shape[0] == 1

    head = jnp.concatenate([jnp.zeros((META_ROW0, d), x.dtype), meta_tokens.astype(x.dtype)], axis=0)
    x2 = x[0]
    bf = lambda w: w.astype(_BF16)
    n_main = GLA_IN - GLA_RANK
    w_in = bf(a_w_in[0])
    w_r = jnp.pad(w_in[:, n_main:], ((0, 0), (0, RANK_PAD - GLA_RANK)))
    w_gate = jnp.pad(bf(a_w_gate_up[0]), ((0, RANK_PAD - GLA_RANK), (0, 0)))

    proj, r = _in_proj(head, x2, norm_mix[0], w_in, n_main, w_r)
    o, w_up, w_down = _gla(proj, r, w_gate, a_b_gate[0], w_mlp_up, w_mlp_down, 0)
    h = _gla_out(o, proj, a_norm_out[0], bf(a_w_out[0]), head, x2)
    h = _mlp(h, norm_mlp[0], w_up, w_down)

    q, k_dup, v_dup = _qkv(h, norm_mix[1], kv_norm, bf(b_w_q[0]), bf(w_kv))
    o, w_up, w_down = _swa(q, k_dup, v_dup, b_sinks[0], w_mlp_up, w_mlp_down, 1)
    h = _matmul_residual(o, bf(b_w_out[0]), h)
    return _mlp(h, norm_mlp[1], w_up, w_down, final_gain=norm_final)[None]
```

```python
import functools
import math

import jax
import jax.numpy as jnp
from jax import lax
from jax.experimental import pallas as pl
from jax.experimental.pallas import tpu as pltpu

D_MODEL = 2048
N_META = 16
D_FF = 4 * D_MODEL
RMS_EPS = 1e-6
NEG_INF = -1e30

GLA_HEADS = 4
GLA_DK = 256
GLA_DV = 512
GLA_QK = GLA_HEADS * GLA_DK
GLA_VD = GLA_HEADS * GLA_DV
GLA_RANK = 16
GLA_GATE_NORM = 16.0
GLA_IN = 2 * GLA_QK + 2 * GLA_VD + GLA_RANK

SWA_HEAD_DIM = 64
SWA_Q_HEADS = 32
SWA_KV_HEADS = 4
SWA_GROUP = SWA_Q_HEADS // SWA_KV_HEADS
SWA_WINDOW = 128
ROPE_THETA = 10000.0

LANES = 128
BF16_ROWS = 16
FRONT = 512
META_ROW0 = FRONT - N_META
TM = 512
IN_PROJ_TN = 1024
MLP_TF = 1024
GLA_CHUNK = 256
GLA_SUB = 64
SWA_BLOCK = 128
SWA_HALF = SWA_BLOCK // 2
SWA_KEYS = 2 * SWA_BLOCK
LOG2E = math.log2(math.e)
RANK_PAD = LANES
VMEM_LIMIT = 52 * 1024 * 1024

_F32 = jnp.float32
_BF16 = jnp.bfloat16


def _dot(a, b):
    return jnp.dot(a, b, preferred_element_type=_F32)


def _dot_nt(a, b):
    return lax.dot_general(a, b, (((1,), (1,)), ((), ())), preferred_element_type=_F32)


def _dot_tn(a, b):
    return lax.dot_general(a, b, (((0,), (0,)), ((), ())), preferred_element_type=_F32)


def _params(*semantics):
    return pltpu.CompilerParams(dimension_semantics=semantics, vmem_limit_bytes=VMEM_LIMIT)


def _rms_scale(x):
    return lax.rsqrt(jnp.mean(x * x, axis=-1, keepdims=True) + RMS_EPS)


def _normed(x, g_ref):
    return (x * _rms_scale(x) * g_ref[...]).astype(_BF16)


def _norm_next_rows(src_ref, g_ref, xn_ref, slot, j, n_steps):
    slab = -(-TM // (n_steps * BF16_ROWS)) * BF16_ROWS
    start = pl.multiple_of(jnp.minimum(j * slab, TM - slab), BF16_ROWS)
    xn_ref[slot, pl.ds(start, slab), :] = _normed(src_ref[pl.ds(start, slab), :], g_ref)


def _stream_tile(head_ref, x_ref, i):
    return jnp.where(i == 0, head_ref[...], x_ref[...])


def _stream_specs(d, grid_rank):
    if grid_rank == 1:
        return [pl.BlockSpec((TM, d), lambda i: (0, 0)),
                pl.BlockSpec((TM, d), lambda i: (jnp.maximum(i - 1, 0), 0))]
    return [pl.BlockSpec((TM, d), lambda i, j: (0, 0)),
            pl.BlockSpec((TM, d), lambda i, j: (jnp.maximum(i - 1, 0), 0))]


def _in_proj_kernel(head_ref, xnext_ref, g_ref, w_ref, wr_ref, o_ref, r_ref, xn_ref, *, n_tiles):
    i, j = pl.program_id(0), pl.program_id(1)
    slot = i % 2

    @pl.when((i == 0) & (j == 0))
    def _():
        xn_ref[0] = _normed(head_ref[...], g_ref)

    def step(first):
        xn = xn_ref[slot]
        o_ref[...] = _dot(xn, w_ref[...]).astype(o_ref.dtype)
        if first:
            r_ref[...] = _dot(xn, wr_ref[...]).astype(r_ref.dtype)
        _norm_next_rows(xnext_ref, g_ref, xn_ref, 1 - slot, j, n_tiles)

    pl.when(j == 0)(functools.partial(step, True))
    pl.when(j > 0)(functools.partial(step, False))


def _in_proj(head, x, gain, w, n, w_r):
    d = x.shape[1]
    rows = head.shape[0] + x.shape[0]
    tn = IN_PROJ_TN
    n_tiles = n // tn
    nr = w_r.shape[1]
    last_x = x.shape[0] // TM - 1
    return pl.pallas_call(
        functools.partial(_in_proj_kernel, n_tiles=n_tiles),
        grid=(rows // TM, n_tiles),
        in_specs=[
            pl.BlockSpec((TM, d), lambda i, j: (0, 0)),
            pl.BlockSpec((TM, d), lambda i, j: (jnp.minimum(i, last_x), 0)),
            pl.BlockSpec((1, d), lambda i, j: (0, 0)),
            pl.BlockSpec((d, tn), lambda i, j: (0, j)),
            pl.BlockSpec((d, nr), lambda i, j: (0, 0)),
        ],
        out_specs=[pl.BlockSpec((TM, tn), lambda i, j: (i, j)),
                   pl.BlockSpec((TM, nr), lambda i, j: (i, 0))],
        out_shape=[jax.ShapeDtypeStruct((rows, n), _BF16),
                   jax.ShapeDtypeStruct((rows, nr), _BF16)],
        scratch_shapes=[pltpu.VMEM((2, TM, d), _BF16)],
        compiler_params=_params("arbitrary", "arbitrary"),
        name="in_proj",
    )(head, x, gain.reshape(1, d), w, w_r)


def _matmul_residual_kernel(a_ref, w_ref, h_ref, o_ref):
    o_ref[...] = h_ref[...] + _dot(a_ref[...], w_ref[...])


def _matmul_residual(a, w, h):
    rows, k = a.shape
    d = w.shape[1]
    return pl.pallas_call(
        _matmul_residual_kernel,
        grid=(rows // TM,),
        in_specs=[
            pl.BlockSpec((TM, k), lambda i: (i, 0)),
            pl.BlockSpec((k, d), lambda i: (0, 0)),
            pl.BlockSpec((TM, d), lambda i: (i, 0)),
        ],
        out_specs=pl.BlockSpec((TM, d), lambda i: (i, 0)),
        out_shape=jax.ShapeDtypeStruct((rows, d), _F32),
        compiler_params=_params("parallel"),
        name="matmul_residual",
    )(a, w, h)


def _mlp_kernel(hnext_ref, g_ref, wu_ref, wd_ref, gf_ref, o_ref, xn_ref, res_ref, *, n_tiles, final):
    t = pl.program_id(0)
    s = jnp.maximum(t - 1, 0)
    j = s % n_tiles
    slot = (s // n_tiles) % 2

    @pl.when(t == 0)
    def _():
        x = hnext_ref[...]
        xn_ref[0] = _normed(x, g_ref)
        res_ref[...] = x

    def step(first):
        u = jnp.maximum(_dot(xn_ref[slot], wu_ref[...]), 0.0)
        part = _dot((u * u).astype(_BF16), wd_ref[...])
        if first:
            o_ref[...] = res_ref[...] + part
        else:
            o_ref[...] += part
        slab = TM // n_tiles
        rows = pl.ds(pl.multiple_of(j * slab, slab), slab)
        x = hnext_ref[rows, :]
        res_ref[rows, :] = x
        xn_ref[1 - slot, rows, :] = _normed(x, g_ref)

    pl.when((t > 0) & (j == 0))(functools.partial(step, True))
    pl.when((t > 0) & (j > 0))(functools.partial(step, False))

    if final:
        @pl.when((t > 0) & (j == n_tiles - 1))
        def _():
            y = o_ref[...]
            o_ref[...] = y * _rms_scale(y) * gf_ref[...]


def _mlp(h, gain, w_up, w_down, final_gain=None):
    rows, d = h.shape
    n_tiles, _, tf = w_up.shape
    assert TM % (n_tiles * BF16_ROWS) == 0
    final = final_gain is not None
    skip = FRONT // TM if final else 0
    out_rows = rows - skip * TM
    last = rows // TM - 1
    gf = (final_gain if final else gain).reshape(1, d)
    hidden = lambda t: jnp.maximum(t - 1, 0) % n_tiles
    row_tile = lambda t: jnp.maximum(t - 1, 0) // n_tiles
    staged = lambda t: jnp.minimum((t + n_tiles - 1) // n_tiles + skip, last)
    return pl.pallas_call(
        functools.partial(_mlp_kernel, n_tiles=n_tiles, final=final),
        grid=(out_rows // TM * n_tiles + 1,),
        in_specs=[
            pl.BlockSpec((TM, d), lambda t: (staged(t), 0)),
            pl.BlockSpec((1, d), lambda t: (0, 0)),
            pl.BlockSpec((None, d, tf), lambda t: (hidden(t), 0, 0)),
            pl.BlockSpec((tf, d), lambda t: (hidden(t), 0)),
            pl.BlockSpec((1, d), lambda t: (0, 0)),
        ],
        out_specs=pl.BlockSpec((TM, d), lambda t: (row_tile(t), 0)),
        out_shape=jax.ShapeDtypeStruct((out_rows, d), _F32),
        scratch_shapes=[pltpu.VMEM((2, TM, d), _BF16), pltpu.VMEM((TM, d), _F32)],
        compiler_params=_params("arbitrary"),
        name="mlp_final" if final else "mlp",
    )(h, gain.reshape(1, d), w_up, w_down, gf)


def _cast_weights(wu_src_ref, wd_src_ref, wu_ref, wd_ref):
    for t in range(wu_ref.shape[0]):
        wu_ref[t] = wu_src_ref[:, t * MLP_TF:(t + 1) * MLP_TF].astype(_BF16)
    wd_ref[...] = wd_src_ref[...].astype(_BF16)


def _cast_weights_specs(w_up, w_down, layer, steps):
    _, d, ff = w_up.shape
    used = 1 << (steps.bit_length() - 1)
    ru, rd = d // used, ff // used
    clamp = lambda s: jnp.minimum(s, used - 1)
    in_specs = [pl.BlockSpec((None, ru, ff), lambda s: (layer, clamp(s), 0)),
                pl.BlockSpec((None, rd, d), lambda s: (layer, clamp(s), 0))]
    out_specs = [pl.BlockSpec((ff // MLP_TF, ru, MLP_TF), lambda s: (0, clamp(s), 0)),
                 pl.BlockSpec((rd, d), lambda s: (clamp(s), 0))]
    out_shapes = [jax.ShapeDtypeStruct((ff // MLP_TF, d, MLP_TF), _BF16),
                  jax.ShapeDtypeStruct((ff, d), _BF16)]
    return in_specs, out_specs, out_shapes


def _gla_kernel(q_ref, k_ref, v_ref, r_ref, wg_ref, bg_ref, wu_src_ref, wd_src_ref,
                o_ref, wu_ref, wd_ref, s_ref, b_ref):
    step = pl.program_id(0)
    _cast_weights(wu_src_ref, wd_src_ref, wu_ref, wd_ref)
    c = GLA_CHUNK
    nsub = c // GLA_SUB

    @pl.when(step == 0)
    def _():
        s_ref[...] = jnp.zeros_like(s_ref)
        b_ref[0] = jnp.zeros((c, GLA_QK), _F32)

    b_all = b_ref[step % 2]

    causal = [lax.broadcasted_iota(jnp.int32, (GLA_SUB, c), 1)
              <= lax.broadcasted_iota(jnp.int32, (GLA_SUB, c), 0) + i * GLA_SUB for i in range(nsub)]

    def rows_of(t, i):
        return t[i * GLA_SUB:(i + 1) * GLA_SUB]

    def scaled(t, e):
        return t * jnp.exp2(e).astype(_BF16)

    q_shift = math.log2(GLA_DK ** -0.5)

    def attention_weights(h):
        ks = slice(h * GLA_DK, (h + 1) * GLA_DK)
        b = b_all[:, ks]
        beta = [jnp.zeros((1, GLA_DK), _F32)] + [b[i * GLA_SUB - 1:i * GLA_SUB, :] for i in range(1, nsub + 1)]
        q_exp = jnp.concatenate([rows_of(b, i) - beta[i] for i in range(nsub)], axis=0)
        k_exp = jnp.concatenate([beta[i + 1] - rows_of(b, i) for i in range(nsub)], axis=0)
        qs = q_ref[:, ks] * jnp.exp2(q_exp + q_shift).astype(_BF16)
        kd = k_ref[:, ks] * jnp.exp2(k_exp).astype(_BF16)
        a_rows = []
        for i in range(nsub):
            kt = jnp.concatenate([scaled(rows_of(kd, j), beta[i] - beta[j + 1]) for j in range(i + 1)]
                                 + [rows_of(kd, j) for j in range(i + 1, nsub)], axis=0)
            a_rows.append(jnp.where(causal[i], _dot_nt(rows_of(qs, i), kt), 0.0).astype(_BF16))
        qe = jnp.concatenate([scaled(rows_of(qs, i), beta[i]) for i in range(nsub)], axis=0)
        k_dec = jnp.concatenate([scaled(rows_of(kd, j), beta[nsub] - beta[j + 1]) for j in range(nsub)], axis=0)
        decay = jnp.exp2(jnp.broadcast_to(beta[nsub], (LANES, GLA_DK))).T
        return jnp.concatenate([jnp.concatenate(a_rows, axis=0), qe], axis=1), k_dec, decay

    def outputs(h, lhs, k_dec, decay):
        vs = slice(h * GLA_DV, (h + 1) * GLA_DV)
        vh = v_ref[:, vs]
        state = s_ref[h]
        o_ref[:, vs] = _dot(lhs, jnp.concatenate([vh, state.astype(_BF16)], axis=0)).astype(o_ref.dtype)
        s_ref[h] = state * jnp.concatenate([decay] * (GLA_DV // LANES), axis=1) + _dot_tn(k_dec, vh)

    pending = None
    for h in range(GLA_HEADS):
        ready = attention_weights(h)
        if pending is not None:
            outputs(h - 1, *pending)
        pending = ready
    outputs(GLA_HEADS - 1, *pending)

    x = _dot(r_ref[...], wg_ref[...]) + bg_ref[...]
    gk = (jnp.minimum(x, 0.0) - jnp.log1p(jnp.exp(-jnp.abs(x)))) * (LOG2E / GLA_GATE_NORM)
    row = (step + 1) * c + lax.broadcasted_iota(jnp.int32, (c, 1), 0)
    gk = jnp.where(row >= META_ROW0, gk, 0.0)
    tri = (lax.broadcasted_iota(jnp.int32, (c, c), 0)
           >= lax.broadcasted_iota(jnp.int32, (c, c), 1)).astype(_BF16)
    hi = gk.astype(_BF16)
    lo = (gk - hi.astype(_F32)).astype(_BF16)
    b_ref[(step + 1) % 2] = _dot(tri, hi) + _dot(tri, lo)


def _gla(proj, r, w_gate, b_gate, w_up, w_down, layer):
    rows = proj.shape[0]
    c = GLA_CHUNK
    steps = rows // c
    last = steps - 1
    cast_in, cast_out, cast_shapes = _cast_weights_specs(w_up, w_down, layer, steps)
    return pl.pallas_call(
        _gla_kernel,
        grid=(steps,),
        in_specs=[
            pl.BlockSpec((c, GLA_QK), lambda s: (s, 0)),
            pl.BlockSpec((c, GLA_QK), lambda s: (s, 1)),
            pl.BlockSpec((c, GLA_VD), lambda s: (s, 1)),
            pl.BlockSpec((c, RANK_PAD), lambda s: (jnp.minimum(s + 1, last), 0)),
            pl.BlockSpec((RANK_PAD, GLA_QK), lambda s: (0, 0)),
            pl.BlockSpec((1, GLA_QK), lambda s: (0, 0)),
        ] + cast_in,
        out_specs=[pl.BlockSpec((c, GLA_VD), lambda s: (s, 0))] + cast_out,
        out_shape=[jax.ShapeDtypeStruct((rows, GLA_VD), _BF16)] + cast_shapes,
        scratch_shapes=[pltpu.VMEM((GLA_HEADS, GLA_DK, GLA_DV), _F32),
                        pltpu.VMEM((2, c, GLA_QK), _F32)],
        compiler_params=_params("arbitrary"),
        name="gla",
    )(proj, proj, proj, r, w_gate, b_gate.reshape(1, GLA_QK), w_up, w_down)


def _gla_out_kernel(o_ref, g_ref, no_ref, w_ref, head_ref, x_ref, out_ref):
    parts = []
    for h in range(GLA_HEADS):
        vs = slice(h * GLA_DV, (h + 1) * GLA_DV)
        o = o_ref[:, vs].astype(_F32)
        hg = 0.5 * g_ref[:, vs].astype(_F32)
        silu = hg * jnp.tanh(hg) + hg
        parts.append((o * _rms_scale(o) * no_ref[...] * silu).astype(_BF16))
    y = jnp.concatenate(parts, axis=1)
    out_ref[...] = _stream_tile(head_ref, x_ref, pl.program_id(0)) + _dot(y, w_ref[...])


def _gla_out(o, proj, norm_out, w, head, x):
    rows, k = o.shape
    d = w.shape[1]
    return pl.pallas_call(
        _gla_out_kernel,
        grid=(rows // TM,),
        in_specs=[
            pl.BlockSpec((TM, k), lambda i: (i, 0)),
            pl.BlockSpec((TM, GLA_VD), lambda i: (i, 2)),
            pl.BlockSpec((1, GLA_DV), lambda i: (0, 0)),
            pl.BlockSpec((k, d), lambda i: (0, 0)),
        ] + _stream_specs(d, 1),
        out_specs=pl.BlockSpec((TM, d), lambda i: (i, 0)),
        out_shape=jax.ShapeDtypeStruct((rows, d), _F32),
        compiler_params=_params("parallel"),
        name="gla_out",
    )(o, proj, norm_out.reshape(1, GLA_DV), w, head, x)


def _swap_halves(x, width):
    lane = lax.broadcasted_iota(jnp.int32, x.shape, 1)
    first = (lane % (2 * width)) < width
    return jnp.where(first, pltpu.roll(x, LANES - width, 1), pltpu.roll(x, width, 1))


def _qkv_kernel(h_ref, gq_ref, gkv_ref, wq_ref, wkv_ref, q_ref, k_ref, v_ref, tab_ref):
    half = SWA_HEAD_DIM // 2
    i = pl.program_id(0)

    def inv_freq(shape):
        lane = lax.broadcasted_iota(jnp.int32, shape, 1)
        return jnp.exp((lane % half).astype(_F32) * (-math.log(ROPE_THETA) / half))

    @pl.when(i == 0)
    def _():
        off = lax.broadcasted_iota(jnp.int32, (TM, LANES), 0).astype(_F32) * inv_freq((TM, LANES))
        tab_ref[0] = jnp.cos(off)
        tab_ref[1] = jnp.sin(off)

    x = h_ref[...]
    xhat = x * _rms_scale(x)
    q = _dot((xhat * gq_ref[...]).astype(_BF16), wq_ref[...])
    kv = _dot((xhat * gkv_ref[...]).astype(_BF16), wkv_ref[...])

    base = (i * TM - META_ROW0).astype(_F32) * inv_freq((8, LANES))[:1]
    cos_b, sin_b = jnp.cos(base), jnp.sin(base)
    cos = cos_b * tab_ref[0] - sin_b * tab_ref[1]
    sin = sin_b * tab_ref[0] + cos_b * tab_ref[1]
    lane = lax.broadcasted_iota(jnp.int32, (TM, LANES), 1)
    sin = jnp.where((lane % SWA_HEAD_DIM) < half, -sin, sin)

    def rope(t):
        return t * cos + _swap_halves(t, half) * sin

    qscale = SWA_HEAD_DIM ** -0.5 * LOG2E
    for p in range(SWA_Q_HEADS // 2):
        sl = slice(p * LANES, (p + 1) * LANES)
        q_ref[:, sl] = (rope(q[:, sl]) * qscale).astype(q_ref.dtype)

    low = (lane % LANES) < SWA_HEAD_DIM
    kv_width = SWA_KV_HEADS * SWA_HEAD_DIM
    for p in range(SWA_KV_HEADS // 2):
        sl = slice(p * LANES, (p + 1) * LANES)
        for t, o_ref in ((rope(kv[:, sl]), k_ref), (kv[:, kv_width + p * LANES:kv_width + (p + 1) * LANES], v_ref)):
            swapped = pltpu.roll(t, SWA_HEAD_DIM, 1)
            o_ref[:, (2 * p) * LANES:(2 * p + 1) * LANES] = jnp.where(low, t, swapped).astype(o_ref.dtype)
            o_ref[:, (2 * p + 1) * LANES:(2 * p + 2) * LANES] = jnp.where(low, swapped, t).astype(o_ref.dtype)


def _qkv(h, gain_q, gain_kv, w_q, w_kv):
    rows, d = h.shape
    nq = w_q.shape[1]
    nkv = w_kv.shape[1]
    dup = SWA_KV_HEADS * LANES
    return pl.pallas_call(
        _qkv_kernel,
        grid=(rows // TM,),
        in_specs=[
            pl.BlockSpec((TM, d), lambda i: (i, 0)),
            pl.BlockSpec((1, d), lambda i: (0, 0)),
            pl.BlockSpec((1, d), lambda i: (0, 0)),
            pl.BlockSpec((d, nq), lambda i: (0, 0)),
            pl.BlockSpec((d, nkv), lambda i: (0, 0)),
        ],
        out_specs=[
            pl.BlockSpec((TM, nq), lambda i: (i, 0)),
            pl.BlockSpec((TM, dup), lambda i: (i, 0)),
            pl.BlockSpec((TM, dup), lambda i: (i, 0)),
        ],
        out_shape=[
            jax.ShapeDtypeStruct((rows, nq), _BF16),
            jax.ShapeDtypeStruct((rows, dup), _BF16),
            jax.ShapeDtypeStruct((rows, dup), _BF16),
        ],
        scratch_shapes=[pltpu.VMEM((2, TM, LANES), _F32)],
        compiler_params=_params("arbitrary"),
        name="qkv_rope",
    )(h, gain_q.reshape(1, d), gain_kv.reshape(1, d), w_q, w_kv)


def _swa_kernel(sink_ref, q_ref, kp_ref, kc_ref, km_ref, vp_ref, vc_ref, vm_ref, wu_src_ref, wd_src_ref,
                o_ref, wu_ref, wd_ref):
    n = pl.program_id(0)
    _cast_weights(wu_src_ref, wd_src_ref, wu_ref, wd_ref)
    first_block = META_ROW0 // SWA_BLOCK
    meta_lo = META_ROW0 % SWA_BLOCK

    @pl.when(n < first_block)
    def _():
        o_ref[...] = jnp.zeros_like(o_ref)

    @pl.when(n >= first_block)
    def _():
        kb = jnp.concatenate([km_ref[meta_lo:, :], kp_ref[...], kc_ref[...], km_ref[meta_lo:, :]], axis=0)
        vb = jnp.concatenate([vm_ref[meta_lo:, :], vp_ref[...], vc_ref[...], vm_ref[meta_lo:, :]], axis=0)
        lane = lax.broadcasted_iota(jnp.int32, (SWA_HALF, LANES), 1)
        low = lane < SWA_HEAD_DIM
        high = lane >= SWA_HEAD_DIM

        qi0 = lax.broadcasted_iota(jnp.int32, (SWA_HALF, SWA_KEYS), 0)
        col = lax.broadcasted_iota(jnp.int32, (SWA_HALF, SWA_KEYS), 1)
        band_w = SWA_KEYS - N_META
        caps = []
        for hq in range(2):
            qi = qi0 + hq * SWA_HALF
            if hq == 0:
                is_meta = col < N_META
                kj = col - N_META
                m = col
                start = 0
            else:
                is_meta = col >= band_w
                kj = col + (2 * SWA_BLOCK - band_w)
                m = col - band_w
                start = 2 * N_META + 2 * SWA_BLOCK - SWA_KEYS
            band_ok = (kj > qi) & (kj <= qi + SWA_WINDOW) & ((n - 1) * SWA_BLOCK + kj >= FRONT)
            meta_ok = (META_ROW0 + m) <= n * SWA_BLOCK + qi
            ok = (is_meta & meta_ok) | (jnp.logical_not(is_meta) & band_ok)
            caps.append((jnp.where(ok, jnp.inf, NEG_INF).astype(_F32), start))

        ones = jnp.ones((SWA_KEYS, LANES), _BF16)

        def sink_tile(g):
            return jnp.concatenate(
                [jnp.full((SWA_HALF, LANES), sink_ref[g * SWA_GROUP + j] * LOG2E, _F32)
                 for j in range(SWA_GROUP)], axis=0)

        def scores(g, hq):
            cap, start = caps[hq]
            rows = slice(hq * SWA_HALF, (hq + 1) * SWA_HALF)
            parts = []
            for j in range(SWA_GROUP):
                pair = (g * SWA_GROUP + j) // 2
                t = q_ref[rows, pair * LANES:(pair + 1) * LANES]
                parts.append(jnp.where(low if j % 2 == 0 else high, t, jnp.zeros_like(t)))
            lhs = jnp.concatenate(parts, axis=0)
            s = _dot_nt(lhs, kb[start:start + SWA_KEYS, g * LANES:(g + 1) * LANES])
            s = jnp.concatenate(
                [jnp.minimum(s[j * SWA_HALF:(j + 1) * SWA_HALF], cap) for j in range(SWA_GROUP)], axis=0)
            row_max = jnp.max(s, axis=-1, keepdims=True)
            m = jnp.maximum(jnp.broadcast_to(row_max, (SWA_GROUP * SWA_HALF, LANES)), sink_tile(g))
            return s, m

        def probs(s, m, g):
            p = jnp.concatenate([jnp.exp2(s[:, c * LANES:(c + 1) * LANES] - m)
                                 for c in range(SWA_KEYS // LANES)], axis=1).astype(_BF16)
            return p, jnp.exp2(sink_tile(g) - m)

        def outputs(p, e, g, hq):
            _, start = caps[hq]
            rows = slice(hq * SWA_HALF, (hq + 1) * SWA_HALF)
            vw = jnp.concatenate([vb[start:start + SWA_KEYS, g * LANES:(g + 1) * LANES], ones], axis=1)
            od = _dot(p, vw)
            o = od[:, :LANES] / (od[:, LANES:] + e)
            for j in range(0, SWA_GROUP, 2):
                pair = (g * SWA_GROUP + j) // 2
                even = o[j * SWA_HALF:(j + 1) * SWA_HALF]
                odd = o[(j + 1) * SWA_HALF:(j + 2) * SWA_HALF]
                o_ref[rows, pair * LANES:(pair + 1) * LANES] = jnp.where(low, even, odd).astype(o_ref.dtype)

        tiles = [(g, hq) for g in range(SWA_KV_HEADS) for hq in range(2)]
        stage_a, stage_b = {}, {}
        for t in range(len(tiles) + 2):
            if t < len(tiles):
                stage_a[t] = scores(*tiles[t])
            if 0 <= t - 1 < len(tiles):
                stage_b[t - 1] = probs(*stage_a.pop(t - 1), tiles[t - 1][0])
            if 0 <= t - 2 < len(tiles):
                outputs(*stage_b.pop(t - 2), *tiles[t - 2])


def _swa(q, k_dup, v_dup, sinks, w_up, w_down, layer):
    rows, nq = q.shape
    steps = rows // SWA_BLOCK
    cast_in, cast_out, cast_shapes = _cast_weights_specs(w_up, w_down, layer, steps)
    dup = k_dup.shape[1]
    first_block = META_ROW0 // SWA_BLOCK
    blk = pl.BlockSpec((SWA_BLOCK, dup), lambda n: (n, 0))
    prev = pl.BlockSpec((SWA_BLOCK, dup), lambda n: (jnp.maximum(n - 1, 0), 0))
    meta = pl.BlockSpec((SWA_BLOCK, dup), lambda n: (first_block, 0))
    return pl.pallas_call(
        _swa_kernel,
        grid=(steps,),
        in_specs=[
            pl.BlockSpec(memory_space=pltpu.SMEM),
            pl.BlockSpec((SWA_BLOCK, nq), lambda n: (n, 0)),
            prev, blk, meta, prev, blk, meta,
        ] + cast_in,
        out_specs=[pl.BlockSpec((SWA_BLOCK, nq), lambda n: (n, 0))] + cast_out,
        out_shape=[jax.ShapeDtypeStruct((rows, nq), _BF16)] + cast_shapes,
        compiler_params=_params("arbitrary"),
        name="swa",
    )(sinks, q, k_dup, k_dup, k_dup, v_dup, v_dup, v_dup, w_up, w_down)


def kernel(x, meta_tokens, norm_mix, norm_mlp, w_mlp_up, w_mlp_down, a_w_in, a_w_gate_up, a_b_gate, a_norm_out, a_w_out, kv_norm, w_kv, b_w_q, b_sinks, b_w_out, norm_final):
    batch, seq, d = x.shape
    assert batch == 1 and d == D_MODEL and seq % TM == 0
    assert norm_mix.shape[0] == 2 and a_w_in.shape[0] == 1 and b_w_q.shape[0] == 1

    head = jnp.concatenate([jnp.zeros((META_ROW0, d), x.dtype), meta_tokens.astype(x.dtype)], axis=0)
    x2 = x[0]
    bf = lambda w: w.astype(_BF16)
    n_main = GLA_IN - GLA_RANK
    w_in = bf(a_w_in[0])
    w_r = jnp.pad(w_in[:, n_main:], ((0, 0), (0, RANK_PAD - GLA_RANK)))
    w_gate = jnp.pad(bf(a_w_gate_up[0]), ((0, RANK_PAD - GLA_RANK), (0, 0)))

    proj, r = _in_proj(head, x2, norm_mix[0], w_in, n_main, w_r)
    o, w_up, w_down = _gla(proj, r, w_gate, a_b_gate[0], w_mlp_up, w_mlp_down, 0)
    h = _gla_out(o, proj, a_norm_out[0], bf(a_w_out[0]), head, x2)
    h = _mlp(h, norm_mlp[0], w_up, w_down)

    q, k_dup, v_dup = _qkv(h, norm_mix[1], kv_norm, bf(b_w_q[0]), bf(w_kv))
    o, w_up, w_down = _swa(q, k_dup, v_dup, b_sinks[0], w_mlp_up, w_mlp_down, 1)
    h = _matmul_residual(o, bf(b_w_out[0]), h)
    return _mlp(h, norm_mlp[1], w_up, w_down, final_gain=norm_final)[None]
```

```python
import functools
import math

import jax
import jax.numpy as jnp
from jax import lax
from jax.experimental import pallas as pl
from jax.experimental.pallas import tpu as pltpu

D_MODEL = 2048
N_META = 16
D_FF = 4 * D_MODEL
RMS_EPS = 1e-6
NEG_INF = -1e30

GLA_HEADS = 4
GLA_DK = 256
GLA_DV = 512
GLA_QK = GLA_HEADS * GLA_DK
GLA_VD = GLA_HEADS * GLA_DV
GLA_RANK = 16
GLA_GATE_NORM = 16.0
GLA_IN = 2 * GLA_QK + 2 * GLA_VD + GLA_RANK

SWA_HEAD_DIM = 64
SWA_Q_HEADS = 32
SWA_KV_HEADS = 4
SWA_GROUP = SWA_Q_HEADS // SWA_KV_HEADS
SWA_WINDOW = 128
ROPE_THETA = 10000.0

LANES = 128
BF16_ROWS = 16
FRONT = 512
META_ROW0 = FRONT - N_META
TM = 512
IN_PROJ_TN = 1024
MLP_TF = 1024
GLA_CHUNK = 256
GLA_SUB = 64
SWA_BLOCK = 128
SWA_HALF = SWA_BLOCK // 2
SWA_KEYS = 2 * SWA_BLOCK
LOG2E = math.log2(math.e)
RANK_PAD = LANES
VMEM_LIMIT = 52 * 1024 * 1024

_F32 = jnp.float32
_BF16 = jnp.bfloat16


def _dot(a, b):
    return jnp.dot(a, b, preferred_element_type=_F32)


def _dot_nt(a, b):
    return lax.dot_general(a, b, (((1,), (1,)), ((), ())), preferred_element_type=_F32)


def _dot_tn(a, b):
    return lax.dot_general(a, b, (((0,), (0,)), ((), ())), preferred_element_type=_F32)


def _params(*semantics):
    return pltpu.CompilerParams(dimension_semantics=semantics, vmem_limit_bytes=VMEM_LIMIT)


def _rms_scale(x):
    return lax.rsqrt(jnp.mean(x * x, axis=-1, keepdims=True) + RMS_EPS)


def _normed(x, g_ref):
    return (x * _rms_scale(x) * g_ref[...]).astype(_BF16)


def _norm_next_rows(src_ref, g_ref, xn_ref, slot, j, n_steps):
    slab = -(-TM // (n_steps * BF16_ROWS)) * BF16_ROWS
    start = pl.multiple_of(jnp.minimum(j * slab, TM - slab), BF16_ROWS)
    xn_ref[slot, pl.ds(start, slab), :] = _normed(src_ref[pl.ds(start, slab), :], g_ref)


def _stream_tile(head_ref, x_ref, i):
    return jnp.where(i == 0, head_ref[...], x_ref[...])


def _stream_specs(d, grid_rank):
    if grid_rank == 1:
        return [pl.BlockSpec((TM, d), lambda i: (0, 0)),
                pl.BlockSpec((TM, d), lambda i: (jnp.maximum(i - 1, 0), 0))]
    return [pl.BlockSpec((TM, d), lambda i, j: (0, 0)),
            pl.BlockSpec((TM, d), lambda i, j: (jnp.maximum(i - 1, 0), 0))]


def _in_proj_kernel(head_ref, xnext_ref, g_ref, w_ref, wr_ref, o_ref, r_ref, xn_ref, *, n_tiles):
    i, j = pl.program_id(0), pl.program_id(1)
    slot = i % 2

    @pl.when((i == 0) & (j == 0))
    def _():
        xn_ref[0] = _normed(head_ref[...], g_ref)

    def step(first):
        xn = xn_ref[slot]
        cols = pl.ds(pl.multiple_of(j * IN_PROJ_TN, IN_PROJ_TN), IN_PROJ_TN)
        o_ref[...] = _dot(xn, w_ref[:, cols]).astype(o_ref.dtype)
        if first:
            r_ref[...] = _dot(xn, wr_ref[...]).astype(r_ref.dtype)
        _norm_next_rows(xnext_ref, g_ref, xn_ref, 1 - slot, j, n_tiles)

    pl.when(j == 0)(functools.partial(step, True))
    pl.when(j > 0)(functools.partial(step, False))


def _in_proj(head, x, gain, w, n, w_r):
    d = x.shape[1]
    rows = head.shape[0] + x.shape[0]
    tn = IN_PROJ_TN
    n_tiles = n // tn
    nr = w_r.shape[1]
    last_x = x.shape[0] // TM - 1
    return pl.pallas_call(
        functools.partial(_in_proj_kernel, n_tiles=n_tiles),
        grid=(rows // TM, n_tiles),
        in_specs=[
            pl.BlockSpec((TM, d), lambda i, j: (0, 0)),
            pl.BlockSpec((TM, d), lambda i, j: (jnp.minimum(i, last_x), 0)),
            pl.BlockSpec((1, d), lambda i, j: (0, 0)),
            pl.BlockSpec(w.shape, lambda i, j: (0, 0), pipeline_mode=pl.Buffered(1)),
            pl.BlockSpec((d, nr), lambda i, j: (0, 0)),
        ],
        out_specs=[pl.BlockSpec((TM, tn), lambda i, j: (i, j)),
                   pl.BlockSpec((TM, nr), lambda i, j: (i, 0))],
        out_shape=[jax.ShapeDtypeStruct((rows, n), _BF16),
                   jax.ShapeDtypeStruct((rows, nr), _BF16)],
        scratch_shapes=[pltpu.VMEM((2, TM, d), _BF16)],
        compiler_params=_params("arbitrary", "arbitrary"),
        name="in_proj",
    )(head, x, gain.reshape(1, d), w, w_r)


def _matmul_residual_kernel(a_ref, w_ref, h_ref, o_ref):
    o_ref[...] = h_ref[...] + _dot(a_ref[...], w_ref[...])


def _matmul_residual(a, w, h):
    rows, k = a.shape
    d = w.shape[1]
    return pl.pallas_call(
        _matmul_residual_kernel,
        grid=(rows // TM,),
        in_specs=[
            pl.BlockSpec((TM, k), lambda i: (i, 0)),
            pl.BlockSpec((k, d), lambda i: (0, 0)),
            pl.BlockSpec((TM, d), lambda i: (i, 0)),
        ],
        out_specs=pl.BlockSpec((TM, d), lambda i: (i, 0)),
        out_shape=jax.ShapeDtypeStruct((rows, d), _F32),
        compiler_params=_params("parallel"),
        name="matmul_residual",
    )(a, w, h)


def _mlp_kernel(hnext_ref, g_ref, wu_ref, wd_ref, gf_ref, o_ref, xn_ref, res_ref, *, n_tiles, final):
    t = pl.program_id(0)
    s = jnp.maximum(t - 1, 0)
    j = s % n_tiles
    slot = (s // n_tiles) % 2

    @pl.when(t == 0)
    def _():
        x = hnext_ref[...]
        xn_ref[0] = _normed(x, g_ref)
        res_ref[...] = x

    def step(first):
        u = jnp.maximum(_dot(xn_ref[slot], wu_ref[...]), 0.0)
        part = _dot((u * u).astype(_BF16), wd_ref[...])
        if first:
            o_ref[...] = res_ref[...] + part
        else:
            o_ref[...] += part
        slab = TM // n_tiles
        rows = pl.ds(pl.multiple_of(j * slab, slab), slab)
        x = hnext_ref[rows, :]
        res_ref[rows, :] = x
        xn_ref[1 - slot, rows, :] = _normed(x, g_ref)

    pl.when((t > 0) & (j == 0))(functools.partial(step, True))
    pl.when((t > 0) & (j > 0))(functools.partial(step, False))

    if final:
        @pl.when((t > 0) & (j == n_tiles - 1))
        def _():
            y = o_ref[...]
            o_ref[...] = y * _rms_scale(y) * gf_ref[...]


def _mlp(h, gain, w_up, w_down, final_gain=None):
    rows, d = h.shape
    n_tiles, _, tf = w_up.shape
    assert TM % (n_tiles * BF16_ROWS) == 0
    final = final_gain is not None
    skip = FRONT // TM if final else 0
    out_rows = rows - skip * TM
    last = rows // TM - 1
    gf = (final_gain if final else gain).reshape(1, d)
    hidden = lambda t: jnp.maximum(t - 1, 0) % n_tiles
    row_tile = lambda t: jnp.maximum(t - 1, 0) // n_tiles
    staged = lambda t: jnp.minimum((t + n_tiles - 1) // n_tiles + skip, last)
    return pl.pallas_call(
        functools.partial(_mlp_kernel, n_tiles=n_tiles, final=final),
        grid=(out_rows // TM * n_tiles + 1,),
        in_specs=[
            pl.BlockSpec((TM, d), lambda t: (staged(t), 0)),
            pl.BlockSpec((1, d), lambda t: (0, 0)),
            pl.BlockSpec((None, d, tf), lambda t: (hidden(t), 0, 0)),
            pl.BlockSpec((tf, d), lambda t: (hidden(t), 0)),
            pl.BlockSpec((1, d), lambda t: (0, 0)),
        ],
        out_specs=pl.BlockSpec((TM, d), lambda t: (row_tile(t), 0)),
        out_shape=jax.ShapeDtypeStruct((out_rows, d), _F32),
        scratch_shapes=[pltpu.VMEM((2, TM, d), _BF16), pltpu.VMEM((TM, d), _F32)],
        compiler_params=_params("arbitrary"),
        name="mlp_final" if final else "mlp",
    )(h, gain.reshape(1, d), w_up, w_down, gf)


def _cast_weights(wu_src_ref, wd_src_ref, wu_ref, wd_ref):
    for t in range(wu_ref.shape[0]):
        wu_ref[t] = wu_src_ref[:, t * MLP_TF:(t + 1) * MLP_TF].astype(_BF16)
    wd_ref[...] = wd_src_ref[...].astype(_BF16)


def _cast_weights_specs(w_up, w_down, layer, steps):
    _, d, ff = w_up.shape
    used = 1 << (steps.bit_length() - 1)
    ru, rd = d // used, ff // used
    clamp = lambda s: jnp.minimum(s, used - 1)
    in_specs = [pl.BlockSpec((None, ru, ff), lambda s: (layer, clamp(s), 0)),
                pl.BlockSpec((None, rd, d), lambda s: (layer, clamp(s), 0))]
    out_specs = [pl.BlockSpec((ff // MLP_TF, ru, MLP_TF), lambda s: (0, clamp(s), 0)),
                 pl.BlockSpec((rd, d), lambda s: (clamp(s), 0))]
    out_shapes = [jax.ShapeDtypeStruct((ff // MLP_TF, d, MLP_TF), _BF16),
                  jax.ShapeDtypeStruct((ff, d), _BF16)]
    return in_specs, out_specs, out_shapes


def _gla_kernel(q_ref, k_ref, v_ref, r_ref, wg_ref, bg_ref, wu_src_ref, wd_src_ref,
                o_ref, wu_ref, wd_ref, s_ref, b_ref):
    step = pl.program_id(0)
    _cast_weights(wu_src_ref, wd_src_ref, wu_ref, wd_ref)
    c = GLA_CHUNK
    nsub = c // GLA_SUB

    @pl.when(step == 0)
    def _():
        s_ref[...] = jnp.zeros_like(s_ref)
        b_ref[0] = jnp.zeros((c, GLA_QK), _F32)

    b_all = b_ref[step % 2]

    causal = [lax.broadcasted_iota(jnp.int32, (GLA_SUB, c), 1)
              <= lax.broadcasted_iota(jnp.int32, (GLA_SUB, c), 0) + i * GLA_SUB for i in range(nsub)]

    def rows_of(t, i):
        return t[i * GLA_SUB:(i + 1) * GLA_SUB]

    def scaled(t, e):
        return t * jnp.exp2(e).astype(_BF16)

    q_shift = math.log2(GLA_DK ** -0.5)

    def attention_weights(h):
        ks = slice(h * GLA_DK, (h + 1) * GLA_DK)
        b = b_all[:, ks]
        beta = [jnp.zeros((1, GLA_DK), _F32)] + [b[i * GLA_SUB - 1:i * GLA_SUB, :] for i in range(1, nsub + 1)]
        q_exp = jnp.concatenate([rows_of(b, i) - beta[i] for i in range(nsub)], axis=0)
        k_exp = jnp.concatenate([beta[i + 1] - rows_of(b, i) for i in range(nsub)], axis=0)
        qs = q_ref[:, ks] * jnp.exp2(q_exp + q_shift).astype(_BF16)
        kd = k_ref[:, ks] * jnp.exp2(k_exp).astype(_BF16)
        a_rows = []
        for i in range(nsub):
            kt = jnp.concatenate([scaled(rows_of(kd, j), beta[i] - beta[j + 1]) for j in range(i + 1)]
                                 + [rows_of(kd, j) for j in range(i + 1, nsub)], axis=0)
            a_rows.append(jnp.where(causal[i], _dot_nt(rows_of(qs, i), kt), 0.0).astype(_BF16))
        qe = jnp.concatenate([scaled(rows_of(qs, i), beta[i]) for i in range(nsub)], axis=0)
        k_dec = jnp.concatenate([scaled(rows_of(kd, j), beta[nsub] - beta[j + 1]) for j in range(nsub)], axis=0)
        decay = jnp.exp2(jnp.broadcast_to(beta[nsub], (LANES, GLA_DK))).T
        return jnp.concatenate([jnp.concatenate(a_rows, axis=0), qe], axis=1), k_dec, decay

    def outputs(h, lhs, k_dec, decay):
        vs = slice(h * GLA_DV, (h + 1) * GLA_DV)
        vh = v_ref[:, vs]
        state = s_ref[h]
        o_ref[:, vs] = _dot(lhs, jnp.concatenate([vh, state.astype(_BF16)], axis=0)).astype(o_ref.dtype)
        s_ref[h] = state * jnp.concatenate([decay] * (GLA_DV // LANES), axis=1) + _dot_tn(k_dec, vh)

    pending = None
    for h in range(GLA_HEADS):
        ready = attention_weights(h)
        if pending is not None:
            outputs(h - 1, *pending)
        pending = ready
    outputs(GLA_HEADS - 1, *pending)

    x = _dot(r_ref[...], wg_ref[...]) + bg_ref[...]
    gk = (jnp.minimum(x, 0.0) - jnp.log1p(jnp.exp(-jnp.abs(x)))) * (LOG2E / GLA_GATE_NORM)
    row = (step + 1) * c + lax.broadcasted_iota(jnp.int32, (c, 1), 0)
    gk = jnp.where(row >= META_ROW0, gk, 0.0)
    tri = (lax.broadcasted_iota(jnp.int32, (c, c), 0)
           >= lax.broadcasted_iota(jnp.int32, (c, c), 1)).astype(_BF16)
    hi = gk.astype(_BF16)
    lo = (gk - hi.astype(_F32)).astype(_BF16)
    b_ref[(step + 1) % 2] = _dot(tri, hi) + _dot(tri, lo)


def _gla(proj, r, w_gate, b_gate, w_up, w_down, layer):
    rows = proj.shape[0]
    c = GLA_CHUNK
    steps = rows // c
    last = steps - 1
    cast_in, cast_out, cast_shapes = _cast_weights_specs(w_up, w_down, layer, steps)
    return pl.pallas_call(
        _gla_kernel,
        grid=(steps,),
        in_specs=[
            pl.BlockSpec((c, GLA_QK), lambda s: (s, 0)),
            pl.BlockSpec((c, GLA_QK), lambda s: (s, 1)),
            pl.BlockSpec((c, GLA_VD), lambda s: (s, 1)),
            pl.BlockSpec((c, RANK_PAD), lambda s: (jnp.minimum(s + 1, last), 0)),
            pl.BlockSpec((RANK_PAD, GLA_QK), lambda s: (0, 0)),
            pl.BlockSpec((1, GLA_QK), lambda s: (0, 0)),
        ] + cast_in,
        out_specs=[pl.BlockSpec((c, GLA_VD), lambda s: (s, 0))] + cast_out,
        out_shape=[jax.ShapeDtypeStruct((rows, GLA_VD), _BF16)] + cast_shapes,
        scratch_shapes=[pltpu.VMEM((GLA_HEADS, GLA_DK, GLA_DV), _F32),
                        pltpu.VMEM((2, c, GLA_QK), _F32)],
        compiler_params=_params("arbitrary"),
        name="gla",
    )(proj, proj, proj, r, w_gate, b_gate.reshape(1, GLA_QK), w_up, w_down)


def _gla_out_kernel(o_ref, g_ref, no_ref, w_ref, head_ref, x_ref, out_ref):
    parts = []
    for h in range(GLA_HEADS):
        vs = slice(h * GLA_DV, (h + 1) * GLA_DV)
        o = o_ref[:, vs].astype(_F32)
        hg = 0.5 * g_ref[:, vs].astype(_F32)
        silu = hg * jnp.tanh(hg) + hg
        parts.append((o * _rms_scale(o) * no_ref[...] * silu).astype(_BF16))
    y = jnp.concatenate(parts, axis=1)
    out_ref[...] = _stream_tile(head_ref, x_ref, pl.program_id(0)) + _dot(y, w_ref[...])


def _gla_out(o, proj, norm_out, w, head, x):
    rows, k = o.shape
    d = w.shape[1]
    return pl.pallas_call(
        _gla_out_kernel,
        grid=(rows // TM,),
        in_specs=[
            pl.BlockSpec((TM, k), lambda i: (i, 0)),
            pl.BlockSpec((TM, GLA_VD), lambda i: (i, 2)),
            pl.BlockSpec((1, GLA_DV), lambda i: (0, 0)),
            pl.BlockSpec((k, d), lambda i: (0, 0)),
        ] + _stream_specs(d, 1),
        out_specs=pl.BlockSpec((TM, d), lambda i: (i, 0)),
        out_shape=jax.ShapeDtypeStruct((rows, d), _F32),
        compiler_params=_params("parallel"),
        name="gla_out",
    )(o, proj, norm_out.reshape(1, GLA_DV), w, head, x)


def _swap_halves(x, width):
    lane = lax.broadcasted_iota(jnp.int32, x.shape, 1)
    first = (lane % (2 * width)) < width
    return jnp.where(first, pltpu.roll(x, LANES - width, 1), pltpu.roll(x, width, 1))


def _qkv_kernel(h_ref, gq_ref, gkv_ref, wq_ref, wkv_ref, q_ref, k_ref, v_ref, tab_ref):
    half = SWA_HEAD_DIM // 2
    i = pl.program_id(0)

    def inv_freq(shape):
        lane = lax.broadcasted_iota(jnp.int32, shape, 1)
        return jnp.exp((lane % half).astype(_F32) * (-math.log(ROPE_THETA) / half))

    @pl.when(i == 0)
    def _():
        off = lax.broadcasted_iota(jnp.int32, (TM, LANES), 0).astype(_F32) * inv_freq((TM, LANES))
        tab_ref[0] = jnp.cos(off)
        tab_ref[1] = jnp.sin(off)

    x = h_ref[...]
    xhat = x * _rms_scale(x)
    q = _dot((xhat * gq_ref[...]).astype(_BF16), wq_ref[...])
    kv = _dot((xhat * gkv_ref[...]).astype(_BF16), wkv_ref[...])

    base = (i * TM - META_ROW0).astype(_F32) * inv_freq((8, LANES))[:1]
    cos_b, sin_b = jnp.cos(base), jnp.sin(base)
    cos = cos_b * tab_ref[0] - sin_b * tab_ref[1]
    sin = sin_b * tab_ref[0] + cos_b * tab_ref[1]
    lane = lax.broadcasted_iota(jnp.int32, (TM, LANES), 1)
    sin = jnp.where((lane % SWA_HEAD_DIM) < half, -sin, sin)

    def rope(t):
        return t * cos + _swap_halves(t, half) * sin

    qscale = SWA_HEAD_DIM ** -0.5 * LOG2E
    for p in range(SWA_Q_HEADS // 2):
        sl = slice(p * LANES, (p + 1) * LANES)
        q_ref[:, sl] = (rope(q[:, sl]) * qscale).astype(q_ref.dtype)

    low = (lane % LANES) < SWA_HEAD_DIM
    kv_width = SWA_KV_HEADS * SWA_HEAD_DIM
    for p in range(SWA_KV_HEADS // 2):
        sl = slice(p * LANES, (p + 1) * LANES)
        for t, o_ref in ((rope(kv[:, sl]), k_ref), (kv[:, kv_width + p * LANES:kv_width + (p + 1) * LANES], v_ref)):
            swapped = pltpu.roll(t, SWA_HEAD_DIM, 1)
            o_ref[:, (2 * p) * LANES:(2 * p + 1) * LANES] = jnp.where(low, t, swapped).astype(o_ref.dtype)
            o_ref[:, (2 * p + 1) * LANES:(2 * p + 2) * LANES] = jnp.where(low, swapped, t).astype(o_ref.dtype)


def _qkv(h, gain_q, gain_kv, w_q, w_kv):
    rows, d = h.shape
    nq = w_q.shape[1]
    nkv = w_kv.shape[1]
    dup = SWA_KV_HEADS * LANES
    return pl.pallas_call(
        _qkv_kernel,
        grid=(rows // TM,),
        in_specs=[
            pl.BlockSpec((TM, d), lambda i: (i, 0)),
            pl.BlockSpec((1, d), lambda i: (0, 0)),
            pl.BlockSpec((1, d), lambda i: (0, 0)),
            pl.BlockSpec((d, nq), lambda i: (0, 0)),
            pl.BlockSpec((d, nkv), lambda i: (0, 0)),
        ],
        out_specs=[
            pl.BlockSpec((TM, nq), lambda i: (i, 0)),
            pl.BlockSpec((TM, dup), lambda i: (i, 0)),
            pl.BlockSpec((TM, dup), lambda i: (i, 0)),
        ],
        out_shape=[
            jax.ShapeDtypeStruct((rows, nq), _BF16),
            jax.ShapeDtypeStruct((rows, dup), _BF16),
            jax.ShapeDtypeStruct((rows, dup), _BF16),
        ],
        scratch_shapes=[pltpu.VMEM((2, TM, LANES), _F32)],
        compiler_params=_params("arbitrary"),
        name="qkv_rope",
    )(h, gain_q.reshape(1, d), gain_kv.reshape(1, d), w_q, w_kv)


def _swa_kernel(sink_ref, q_ref, kp_ref, kc_ref, km_ref, vp_ref, vc_ref, vm_ref, wu_src_ref, wd_src_ref,
                o_ref, wu_ref, wd_ref):
    n = pl.program_id(0)
    _cast_weights(wu_src_ref, wd_src_ref, wu_ref, wd_ref)
    first_block = META_ROW0 // SWA_BLOCK
    meta_lo = META_ROW0 % SWA_BLOCK

    @pl.when(n < first_block)
    def _():
        o_ref[...] = jnp.zeros_like(o_ref)

    @pl.when(n >= first_block)
    def _():
        kb = jnp.concatenate([km_ref[meta_lo:, :], kp_ref[...], kc_ref[...], km_ref[meta_lo:, :]], axis=0)
        vb = jnp.concatenate([vm_ref[meta_lo:, :], vp_ref[...], vc_ref[...], vm_ref[meta_lo:, :]], axis=0)
        lane = lax.broadcasted_iota(jnp.int32, (SWA_HALF, LANES), 1)
        low = lane < SWA_HEAD_DIM
        high = lane >= SWA_HEAD_DIM

        qi0 = lax.broadcasted_iota(jnp.int32, (SWA_HALF, SWA_KEYS), 0)
        col = lax.broadcasted_iota(jnp.int32, (SWA_HALF, SWA_KEYS), 1)
        band_w = SWA_KEYS - N_META
        caps = []
        for hq in range(2):
            qi = qi0 + hq * SWA_HALF
            if hq == 0:
                is_meta = col < N_META
                kj = col - N_META
                m = col
                start = 0
            else:
                is_meta = col >= band_w
                kj = col + (2 * SWA_BLOCK - band_w)
                m = col - band_w
                start = 2 * N_META + 2 * SWA_BLOCK - SWA_KEYS
            band_ok = (kj > qi) & (kj <= qi + SWA_WINDOW) & ((n - 1) * SWA_BLOCK + kj >= FRONT)
            meta_ok = (META_ROW0 + m) <= n * SWA_BLOCK + qi
            ok = (is_meta & meta_ok) | (jnp.logical_not(is_meta) & band_ok)
            caps.append((jnp.where(ok, jnp.inf, NEG_INF).astype(_F32), start))

        ones = jnp.ones((SWA_KEYS, LANES), _BF16)

        def sink_tile(g):
            return jnp.concatenate(
                [jnp.full((SWA_HALF, LANES), sink_ref[g * SWA_GROUP + j] * LOG2E, _F32)
                 for j in range(SWA_GROUP)], axis=0)

        def scores(g, hq):
            cap, start = caps[hq]
            rows = slice(hq * SWA_HALF, (hq + 1) * SWA_HALF)
            parts = []
            for j in range(SWA_GROUP):
                pair = (g * SWA_GROUP + j) // 2
                t = q_ref[rows, pair * LANES:(pair + 1) * LANES]
                parts.append(jnp.where(low if j % 2 == 0 else high, t, jnp.zeros_like(t)))
            lhs = jnp.concatenate(parts, axis=0)
            s = _dot_nt(lhs, kb[start:start + SWA_KEYS, g * LANES:(g + 1) * LANES])
            s = jnp.concatenate(
                [jnp.minimum(s[j * SWA_HALF:(j + 1) * SWA_HALF], cap) for j in range(SWA_GROUP)], axis=0)
            row_max = jnp.max(s, axis=-1, keepdims=True)
            m = jnp.maximum(jnp.broadcast_to(row_max, (SWA_GROUP * SWA_HALF, LANES)), sink_tile(g))
            return s, m

        def probs(s, m, g):
            p = jnp.concatenate([jnp.exp2(s[:, c * LANES:(c + 1) * LANES] - m)
                                 for c in range(SWA_KEYS // LANES)], axis=1).astype(_BF16)
            return p, jnp.exp2(sink_tile(g) - m)

        def outputs(p, e, g, hq):
            _, start = caps[hq]
            rows = slice(hq * SWA_HALF, (hq + 1) * SWA_HALF)
            vw = jnp.concatenate([vb[start:start + SWA_KEYS, g * LANES:(g + 1) * LANES], ones], axis=1)
            od = _dot(p, vw)
            o = od[:, :LANES] / (od[:, LANES:] + e)
            for j in range(0, SWA_GROUP, 2):
                pair = (g * SWA_GROUP + j) // 2
                even = o[j * SWA_HALF:(j + 1) * SWA_HALF]
                odd = o[(j + 1) * SWA_HALF:(j + 2) * SWA_HALF]
                o_ref[rows, pair * LANES:(pair + 1) * LANES] = jnp.where(low, even, odd).astype(o_ref.dtype)

        tiles = [(g, hq) for g in range(SWA_KV_HEADS) for hq in range(2)]
        stage_a, stage_b = {}, {}
        for t in range(len(tiles) + 2):
            if t < len(tiles):
                stage_a[t] = scores(*tiles[t])
            if 0 <= t - 1 < len(tiles):
                stage_b[t - 1] = probs(*stage_a.pop(t - 1), tiles[t - 1][0])
            if 0 <= t - 2 < len(tiles):
                outputs(*stage_b.pop(t - 2), *tiles[t - 2])


def _swa(q, k_dup, v_dup, sinks, w_up, w_down, layer):
    rows, nq = q.shape
    steps = rows // SWA_BLOCK
    cast_in, cast_out, cast_shapes = _cast_weights_specs(w_up, w_down, layer, steps)
    dup = k_dup.shape[1]
    first_block = META_ROW0 // SWA_BLOCK
    blk = pl.BlockSpec((SWA_BLOCK, dup), lambda n: (n, 0))
    prev = pl.BlockSpec((SWA_BLOCK, dup), lambda n: (jnp.maximum(n - 1, 0), 0))
    meta = pl.BlockSpec((SWA_BLOCK, dup), lambda n: (first_block, 0))
    return pl.pallas_call(
        _swa_kernel,
        grid=(steps,),
        in_specs=[
            pl.BlockSpec(memory_space=pltpu.SMEM),
            pl.BlockSpec((SWA_BLOCK, nq), lambda n: (n, 0)),
            prev, blk, meta, prev, blk, meta,
        ] + cast_in,
        out_specs=[pl.BlockSpec((SWA_BLOCK, nq), lambda n: (n, 0))] + cast_out,
        out_shape=[jax.ShapeDtypeStruct((rows, nq), _BF16)] + cast_shapes,
        compiler_params=_params("arbitrary"),
        name="swa",
    )(sinks, q, k_dup, k_dup, k_dup, v_dup, v_dup, v_dup, w_up, w_down)


def kernel(x, meta_tokens, norm_mix, norm_mlp, w_mlp_up, w_mlp_down, a_w_in, a_w_gate_up, a_b_gate, a_norm_out, a_w_out, kv_norm, w_kv, b_w_q, b_sinks, b_w_out, norm_final):
    batch, seq, d = x.shape
    assert batch == 1 and d == D_MODEL and seq % TM == 0
    assert norm_mix.shape[0] == 2 and a_w_in.shape[0] == 1 and b_w_q.shape[0] == 1

    head = jnp.concatenate([jnp.zeros((META_ROW0, d), x.dtype), meta_tokens.astype(x.dtype)], axis=0)
    x2 = x[0]
    bf = lambda w: w.astype(_BF16)
    n_main = GLA_IN - GLA_RANK
    w_in = bf(a_w_in[0])
    w_r = jnp.pad(w_in[:, n_main:], ((0, 0), (0, RANK_PAD - GLA_RANK)))
    w_gate = jnp.pad(bf(a_w_gate_up[0]), ((0, RANK_PAD - GLA_RANK), (0, 0)))

    proj, r = _in_proj(head, x2, norm_mix[0], w_in, n_main, w_r)
    o, w_up, w_down = _gla(proj, r, w_gate, a_b_gate[0], w_mlp_up, w_mlp_down, 0)
    h = _gla_out(o, proj, a_norm_out[0], bf(a_w_out[0]), head, x2)
    h = _mlp(h, norm_mlp[0], w_up, w_down)

    q, k_dup, v_dup = _qkv(h, norm_mix[1], kv_norm, bf(b_w_q[0]), bf(w_kv))
    o, w_up, w_down = _swa(q, k_dup, v_dup, b_sinks[0], w_mlp_up, w_mlp_down, 1)
    h = _matmul_residual(o, bf(b_w_out[0]), h)
    return _mlp(h, norm_mlp[1], w_up, w_down, final_gain=norm_final)[None]
```

```python
import functools
import math

import jax
import jax.numpy as jnp
from jax import lax
from jax.experimental import pallas as pl
from jax.experimental.pallas import tpu as pltpu

D_MODEL = 2048
N_META = 16
D_FF = 4 * D_MODEL
RMS_EPS = 1e-6
NEG_INF = -1e30

GLA_HEADS = 4
GLA_DK = 256
GLA_DV = 512
GLA_QK = GLA_HEADS * GLA_DK
GLA_VD = GLA_HEADS * GLA_DV
GLA_RANK = 16
GLA_GATE_NORM = 16.0
GLA_IN = 2 * GLA_QK + 2 * GLA_VD + GLA_RANK

SWA_HEAD_DIM = 64
SWA_Q_HEADS = 32
SWA_KV_HEADS = 4
SWA_GROUP = SWA_Q_HEADS // SWA_KV_HEADS
SWA_WINDOW = 128
ROPE_THETA = 10000.0

LANES = 128
BF16_ROWS = 16
FRONT = 512
META_ROW0 = FRONT - N_META
TM = 512
IN_PROJ_TN = 1024
MLP_TF = 1024
GLA_CHUNK = 256
GLA_SUB = 64
SWA_BLOCK = 128
SWA_HALF = SWA_BLOCK // 2
SWA_KEYS = 2 * SWA_BLOCK
LOG2E = math.log2(math.e)
RANK_PAD = LANES
MAX_CAST_STEPS = 128
VMEM_LIMIT = 56 * 1024 * 1024

_F32 = jnp.float32
_BF16 = jnp.bfloat16


def _dot(a, b):
    return jnp.dot(a, b, preferred_element_type=_F32)


def _dot_nt(a, b):
    return lax.dot_general(a, b, (((1,), (1,)), ((), ())), preferred_element_type=_F32)


def _dot_tn(a, b):
    return lax.dot_general(a, b, (((0,), (0,)), ((), ())), preferred_element_type=_F32)


def _params(*semantics):
    return pltpu.CompilerParams(dimension_semantics=semantics, vmem_limit_bytes=VMEM_LIMIT)


def _rms_scale(x):
    return lax.rsqrt(jnp.mean(x * x, axis=-1, keepdims=True) + RMS_EPS)


def _normed(x, g_ref):
    return (x * _rms_scale(x) * g_ref[...]).astype(_BF16)


def _norm_next_rows(src_ref, g_ref, xn_ref, slot, j, n_steps):
    slab = -(-TM // (n_steps * BF16_ROWS)) * BF16_ROWS
    start = pl.multiple_of(jnp.minimum(j * slab, TM - slab), BF16_ROWS)
    xn_ref[slot, pl.ds(start, slab), :] = _normed(src_ref[pl.ds(start, slab), :], g_ref)


def _stream_tile(head_ref, x_ref, i):
    return jnp.where(i == 0, head_ref[...], x_ref[...])


def _stream_specs(d, grid_rank):
    if grid_rank == 1:
        return [pl.BlockSpec((TM, d), lambda i: (0, 0)),
                pl.BlockSpec((TM, d), lambda i: (jnp.maximum(i - 1, 0), 0))]
    return [pl.BlockSpec((TM, d), lambda i, j: (0, 0)),
            pl.BlockSpec((TM, d), lambda i, j: (jnp.maximum(i - 1, 0), 0))]


def _cast_slabs(src_refs, dst_refs):
    (wu_src, wd_src, *small_src), (wu, wd, *small) = src_refs, dst_refs
    for t in range(wu.shape[0]):
        wu[t] = wu_src[:, t * MLP_TF:(t + 1) * MLP_TF].astype(_BF16)
    wd[...] = wd_src[...].astype(_BF16)
    for src, dst in zip(small_src, small):
        dst[...] = src[...].astype(_BF16)


def _cast_specs(w_up, w_down, small, n_tiles, cast_steps):
    layers, d, ff = w_up.shape
    per_layer = cast_steps // layers
    ru, rd = d // per_layer, ff // per_layer
    step = lambda i, j: jnp.minimum(i * n_tiles + j, cast_steps - 1)
    layer = lambda i, j: step(i, j) // per_layer
    slab = lambda i, j: step(i, j) % per_layer
    in_specs = [pl.BlockSpec((None, ru, ff), lambda i, j: (layer(i, j), slab(i, j), 0)),
                pl.BlockSpec((None, rd, d), lambda i, j: (layer(i, j), slab(i, j), 0))]
    out_specs = [pl.BlockSpec((None, ff // MLP_TF, ru, MLP_TF), lambda i, j: (layer(i, j), 0, slab(i, j), 0)),
                 pl.BlockSpec((None, rd, d), lambda i, j: (layer(i, j), slab(i, j), 0))]
    out_shapes = [jax.ShapeDtypeStruct((layers, ff // MLP_TF, d, MLP_TF), _BF16),
                  jax.ShapeDtypeStruct((layers, ff, d), _BF16)]
    for w in small:
        r, c = w.shape
        spec = pl.BlockSpec((r // cast_steps, c), lambda i, j: (step(i, j), 0))
        in_specs.append(spec)
        out_specs.append(spec)
        out_shapes.append(jax.ShapeDtypeStruct((r, c), _BF16))
    return in_specs, out_specs, out_shapes


def _in_proj_kernel(*refs, n_tiles, n_cast):
    head_ref, xnext_ref, g_ref, w_ref, wr_ref = refs[:5]
    cast_src = refs[5:5 + n_cast]
    o_ref, r_ref = refs[5 + n_cast:7 + n_cast]
    cast_dst = refs[7 + n_cast:7 + 2 * n_cast]
    xn_ref = refs[-1]
    i, j = pl.program_id(0), pl.program_id(1)
    slot = i % 2

    @pl.when((i == 0) & (j == 0))
    def _():
        xn_ref[0] = _normed(head_ref[...], g_ref)

    def step(first):
        xn = xn_ref[slot]
        cols = pl.ds(pl.multiple_of(j * IN_PROJ_TN, IN_PROJ_TN), IN_PROJ_TN)
        o_ref[...] = _dot(xn, w_ref[:, cols]).astype(o_ref.dtype)
        if first:
            r_ref[...] = _dot(xn, wr_ref[...]).astype(r_ref.dtype)
        _norm_next_rows(xnext_ref, g_ref, xn_ref, 1 - slot, j, n_tiles)
        _cast_slabs(cast_src, cast_dst)

    pl.when(j == 0)(functools.partial(step, True))
    pl.when(j > 0)(functools.partial(step, False))


def _in_proj(head, x, gain, w, n, w_r, w_up, w_down, small):
    d = x.shape[1]
    rows = head.shape[0] + x.shape[0]
    tn = IN_PROJ_TN
    n_tiles = n // tn
    nr = w_r.shape[1]
    last_x = x.shape[0] // TM - 1
    steps = rows // TM * n_tiles
    cast_steps = 1 << (min(steps, MAX_CAST_STEPS).bit_length() - 1)
    cast_in, cast_out, cast_shapes = _cast_specs(w_up, w_down, small, n_tiles, cast_steps)
    proj, r, *casts = pl.pallas_call(
        functools.partial(_in_proj_kernel, n_tiles=n_tiles, n_cast=len(cast_in)),
        grid=(rows // TM, n_tiles),
        in_specs=[
            pl.BlockSpec((TM, d), lambda i, j: (0, 0)),
            pl.BlockSpec((TM, d), lambda i, j: (jnp.minimum(i, last_x), 0)),
            pl.BlockSpec((1, d), lambda i, j: (0, 0)),
            pl.BlockSpec(w.shape, lambda i, j: (0, 0), pipeline_mode=pl.Buffered(1)),
            pl.BlockSpec((d, nr), lambda i, j: (0, 0)),
        ] + cast_in,
        out_specs=[pl.BlockSpec((TM, tn), lambda i, j: (i, j)),
                   pl.BlockSpec((TM, nr), lambda i, j: (i, 0))] + cast_out,
        out_shape=[jax.ShapeDtypeStruct((rows, n), _BF16),
                   jax.ShapeDtypeStruct((rows, nr), _BF16)] + cast_shapes,
        scratch_shapes=[pltpu.VMEM((2, TM, d), _BF16)],
        compiler_params=_params("arbitrary", "arbitrary"),
        name="in_proj",
    )(head, x, gain.reshape(1, d), w, w_r, w_up, w_down, *small)
    return proj, r, casts


def _matmul_residual_kernel(a_ref, w_ref, h_ref, o_ref):
    o_ref[...] = h_ref[...] + _dot(a_ref[...], w_ref[...])


def _matmul_residual(a, w, h):
    rows, k = a.shape
    d = w.shape[1]
    return pl.pallas_call(
        _matmul_residual_kernel,
        grid=(rows // TM,),
        in_specs=[
            pl.BlockSpec((TM, k), lambda i: (i, 0)),
            pl.BlockSpec((k, d), lambda i: (0, 0)),
            pl.BlockSpec((TM, d), lambda i: (i, 0)),
        ],
        out_specs=pl.BlockSpec((TM, d), lambda i: (i, 0)),
        out_shape=jax.ShapeDtypeStruct((rows, d), _F32),
        compiler_params=_params("parallel"),
        name="matmul_residual",
    )(a, w, h)


def _mlp_kernel(hnext_ref, g_ref, wu_ref, wd_ref, gf_ref, o_ref, xn_ref, res_ref, *, n_tiles, final):
    t = pl.program_id(0)
    s = jnp.maximum(t - 1, 0)
    j = s % n_tiles
    slot = (s // n_tiles) % 2

    @pl.when(t == 0)
    def _():
        x = hnext_ref[...]
        xn_ref[0] = _normed(x, g_ref)
        res_ref[...] = x

    def step(first):
        u = jnp.maximum(_dot(xn_ref[slot], wu_ref[...]), 0.0)
        part = _dot((u * u).astype(_BF16), wd_ref[...])
        if first:
            o_ref[...] = res_ref[...] + part
        else:
            o_ref[...] += part
        slab = TM // n_tiles
        rows = pl.ds(pl.multiple_of(j * slab, slab), slab)
        x = hnext_ref[rows, :]
        res_ref[rows, :] = x
        xn_ref[1 - slot, rows, :] = _normed(x, g_ref)

    pl.when((t > 0) & (j == 0))(functools.partial(step, True))
    pl.when((t > 0) & (j > 0))(functools.partial(step, False))

    if final:
        @pl.when((t > 0) & (j == n_tiles - 1))
        def _():
            y = o_ref[...]
            o_ref[...] = y * _rms_scale(y) * gf_ref[...]


def _mlp(h, gain, w_up, w_down, layer, final_gain=None):
    rows, d = h.shape
    _, n_tiles, _, tf = w_up.shape
    assert TM % (n_tiles * BF16_ROWS) == 0
    final = final_gain is not None
    skip = FRONT // TM if final else 0
    out_rows = rows - skip * TM
    last = rows // TM - 1
    gf = (final_gain if final else gain).reshape(1, d)
    hidden = lambda t: jnp.maximum(t - 1, 0) % n_tiles
    row_tile = lambda t: jnp.maximum(t - 1, 0) // n_tiles
    staged = lambda t: jnp.minimum((t + n_tiles - 1) // n_tiles + skip, last)
    return pl.pallas_call(
        functools.partial(_mlp_kernel, n_tiles=n_tiles, final=final),
        grid=(out_rows // TM * n_tiles + 1,),
        in_specs=[
            pl.BlockSpec((TM, d), lambda t: (staged(t), 0)),
            pl.BlockSpec((1, d), lambda t: (0, 0)),
            pl.BlockSpec((None, None, d, tf), lambda t: (layer, hidden(t), 0, 0)),
            pl.BlockSpec((None, tf, d), lambda t: (layer, hidden(t), 0)),
            pl.BlockSpec((1, d), lambda t: (0, 0)),
        ],
        out_specs=pl.BlockSpec((TM, d), lambda t: (row_tile(t), 0)),
        out_shape=jax.ShapeDtypeStruct((out_rows, d), _F32),
        scratch_shapes=[pltpu.VMEM((2, TM, d), _BF16), pltpu.VMEM((TM, d), _F32)],
        compiler_params=_params("arbitrary"),
        name="mlp_final" if final else "mlp",
    )(h, gain.reshape(1, d), w_up, w_down, gf)


def _gla_kernel(q_ref, k_ref, v_ref, r_ref, wg_ref, bg_ref, o_ref, s_ref, b_ref):
    step = pl.program_id(0)
    c = GLA_CHUNK
    nsub = c // GLA_SUB

    @pl.when(step == 0)
    def _():
        s_ref[...] = jnp.zeros_like(s_ref)
        b_ref[0] = jnp.zeros((c, GLA_QK), _F32)

    b_all = b_ref[step % 2]

    causal = [lax.broadcasted_iota(jnp.int32, (GLA_SUB, c), 1)
              <= lax.broadcasted_iota(jnp.int32, (GLA_SUB, c), 0) + i * GLA_SUB for i in range(nsub)]

    def rows_of(t, i):
        return t[i * GLA_SUB:(i + 1) * GLA_SUB]

    def scaled(t, e):
        return t * jnp.exp2(e).astype(_BF16)

    q_shift = math.log2(GLA_DK ** -0.5)

    def attention_weights(h):
        ks = slice(h * GLA_DK, (h + 1) * GLA_DK)
        b = b_all[:, ks]
        beta = [jnp.zeros((1, GLA_DK), _F32)] + [b[i * GLA_SUB - 1:i * GLA_SUB, :] for i in range(1, nsub + 1)]
        q_exp = jnp.concatenate([rows_of(b, i) - beta[i] for i in range(nsub)], axis=0)
        k_exp = jnp.concatenate([beta[i + 1] - rows_of(b, i) for i in range(nsub)], axis=0)
        qs = q_ref[:, ks] * jnp.exp2(q_exp + q_shift).astype(_BF16)
        kd = k_ref[:, ks] * jnp.exp2(k_exp).astype(_BF16)
        a_rows = []
        for i in range(nsub):
            kt = jnp.concatenate([scaled(rows_of(kd, j), beta[i] - beta[j + 1]) for j in range(i + 1)]
                                 + [rows_of(kd, j) for j in range(i + 1, nsub)], axis=0)
            a_rows.append(jnp.where(causal[i], _dot_nt(rows_of(qs, i), kt), 0.0).astype(_BF16))
        qe = jnp.concatenate([scaled(rows_of(qs, i), beta[i]) for i in range(nsub)], axis=0)
        k_dec = jnp.concatenate([scaled(rows_of(kd, j), beta[nsub] - beta[j + 1]) for j in range(nsub)], axis=0)
        decay = jnp.exp2(jnp.broadcast_to(beta[nsub], (LANES, GLA_DK))).T
        return jnp.concatenate([jnp.concatenate(a_rows, axis=0), qe], axis=1), k_dec, decay

    def outputs(h, lhs, k_dec, decay):
        vs = slice(h * GLA_DV, (h + 1) * GLA_DV)
        vh = v_ref[:, vs]
        state = s_ref[h]
        o_ref[:, vs] = _dot(lhs, jnp.concatenate([vh, state.astype(_BF16)], axis=0)).astype(o_ref.dtype)
        s_ref[h] = state * jnp.concatenate([decay] * (GLA_DV // LANES), axis=1) + _dot_tn(k_dec, vh)

    pending = None
    for h in range(GLA_HEADS):
        ready = attention_weights(h)
        if pending is not None:
            outputs(h - 1, *pending)
        pending = ready
    outputs(GLA_HEADS - 1, *pending)

    x = _dot(r_ref[...], wg_ref[...]) + bg_ref[...]
    gk = (jnp.minimum(x, 0.0) - jnp.log1p(jnp.exp(-jnp.abs(x)))) * (LOG2E / GLA_GATE_NORM)
    row = (step + 1) * c + lax.broadcasted_iota(jnp.int32, (c, 1), 0)
    gk = jnp.where(row >= META_ROW0, gk, 0.0)
    tri = (lax.broadcasted_iota(jnp.int32, (c, c), 0)
           >= lax.broadcasted_iota(jnp.int32, (c, c), 1)).astype(_BF16)
    hi = gk.astype(_BF16)
    lo = (gk - hi.astype(_F32)).astype(_BF16)
    b_ref[(step + 1) % 2] = _dot(tri, hi) + _dot(tri, lo)


def _gla(proj, r, w_gate, b_gate):
    rows = proj.shape[0]
    c = GLA_CHUNK
    steps = rows // c
    last = steps - 1
    return pl.pallas_call(
        _gla_kernel,
        grid=(steps,),
        in_specs=[
            pl.BlockSpec((c, GLA_QK), lambda s: (s, 0)),
            pl.BlockSpec((c, GLA_QK), lambda s: (s, 1)),
            pl.BlockSpec((c, GLA_VD), lambda s: (s, 1)),
            pl.BlockSpec((c, RANK_PAD), lambda s: (jnp.minimum(s + 1, last), 0)),
            pl.BlockSpec((RANK_PAD, GLA_QK), lambda s: (0, 0)),
            pl.BlockSpec((1, GLA_QK), lambda s: (0, 0)),
        ],
        out_specs=pl.BlockSpec((c, GLA_VD), lambda s: (s, 0)),
        out_shape=jax.ShapeDtypeStruct((rows, GLA_VD), _BF16),
        scratch_shapes=[pltpu.VMEM((GLA_HEADS, GLA_DK, GLA_DV), _F32),
                        pltpu.VMEM((2, c, GLA_QK), _F32)],
        compiler_params=_params("arbitrary"),
        name="gla",
    )(proj, proj, proj, r, w_gate, b_gate.reshape(1, GLA_QK))


def _gla_out_kernel(o_ref, g_ref, no_ref, w_ref, head_ref, x_ref, out_ref):
    parts = []
    for h in range(GLA_HEADS):
        vs = slice(h * GLA_DV, (h + 1) * GLA_DV)
        o = o_ref[:, vs].astype(_F32)
        hg = 0.5 * g_ref[:, vs].astype(_F32)
        silu = hg * jnp.tanh(hg) + hg
        parts.append((o * _rms_scale(o) * no_ref[...] * silu).astype(_BF16))
    y = jnp.concatenate(parts, axis=1)
    out_ref[...] = _stream_tile(head_ref, x_ref, pl.program_id(0)) + _dot(y, w_ref[...])


def _gla_out(o, proj, norm_out, w, head, x):
    rows, k = o.shape
    d = w.shape[1]
    return pl.pallas_call(
        _gla_out_kernel,
        grid=(rows // TM,),
        in_specs=[
            pl.BlockSpec((TM, k), lambda i: (i, 0)),
            pl.BlockSpec((TM, GLA_VD), lambda i: (i, 2)),
            pl.BlockSpec((1, GLA_DV), lambda i: (0, 0)),
            pl.BlockSpec((k, d), lambda i: (0, 0)),
        ] + _stream_specs(d, 1),
        out_specs=pl.BlockSpec((TM, d), lambda i: (i, 0)),
        out_shape=jax.ShapeDtypeStruct((rows, d), _F32),
        compiler_params=_params("parallel"),
        name="gla_out",
    )(o, proj, norm_out.reshape(1, GLA_DV), w, head, x)


def _swap_halves(x, width):
    lane = lax.broadcasted_iota(jnp.int32, x.shape, 1)
    first = (lane % (2 * width)) < width
    return jnp.where(first, pltpu.roll(x, LANES - width, 1), pltpu.roll(x, width, 1))


def _qkv_kernel(h_ref, gq_ref, gkv_ref, wq_ref, wkv_ref, q_ref, k_ref, v_ref, tab_ref):
    half = SWA_HEAD_DIM // 2
    i = pl.program_id(0)

    def inv_freq(shape):
        lane = lax.broadcasted_iota(jnp.int32, shape, 1)
        return jnp.exp((lane % half).astype(_F32) * (-math.log(ROPE_THETA) / half))

    @pl.when(i == 0)
    def _():
        off = lax.broadcasted_iota(jnp.int32, (TM, LANES), 0).astype(_F32) * inv_freq((TM, LANES))
        tab_ref[0] = jnp.cos(off)
        tab_ref[1] = jnp.sin(off)

    x = h_ref[...]
    xhat = x * _rms_scale(x)
    q = _dot((xhat * gq_ref[...]).astype(_BF16), wq_ref[...])
    kv = _dot((xhat * gkv_ref[...]).astype(_BF16), wkv_ref[...])

    base = (i * TM - META_ROW0).astype(_F32) * inv_freq((8, LANES))[:1]
    cos_b, sin_b = jnp.cos(base), jnp.sin(base)
    cos = cos_b * tab_ref[0] - sin_b * tab_ref[1]
    sin = sin_b * tab_ref[0] + cos_b * tab_ref[1]
    lane = lax.broadcasted_iota(jnp.int32, (TM, LANES), 1)
    sin = jnp.where((lane % SWA_HEAD_DIM) < half, -sin, sin)

    def rope(t):
        return t * cos + _swap_halves(t, half) * sin

    qscale = SWA_HEAD_DIM ** -0.5 * LOG2E
    for p in range(SWA_Q_HEADS // 2):
        sl = slice(p * LANES, (p + 1) * LANES)
        q_ref[:, sl] = (rope(q[:, sl]) * qscale).astype(q_ref.dtype)

    low = (lane % LANES) < SWA_HEAD_DIM
    kv_width = SWA_KV_HEADS * SWA_HEAD_DIM
    for p in range(SWA_KV_HEADS // 2):
        sl = slice(p * LANES, (p + 1) * LANES)
        for t, o_ref in ((rope(kv[:, sl]), k_ref), (kv[:, kv_width + p * LANES:kv_width + (p + 1) * LANES], v_ref)):
            swapped = pltpu.roll(t, SWA_HEAD_DIM, 1)
            o_ref[:, (2 * p) * LANES:(2 * p + 1) * LANES] = jnp.where(low, t, swapped).astype(o_ref.dtype)
            o_ref[:, (2 * p + 1) * LANES:(2 * p + 2) * LANES] = jnp.where(low, swapped, t).astype(o_ref.dtype)


def _qkv(h, gain_q, gain_kv, w_q, w_kv):
    rows, d = h.shape
    nq = w_q.shape[1]
    nkv = w_kv.shape[1]
    dup = SWA_KV_HEADS * LANES
    return pl.pallas_call(
        _qkv_kernel,
        grid=(rows // TM,),
        in_specs=[
            pl.BlockSpec((TM, d), lambda i: (i, 0)),
            pl.BlockSpec((1, d), lambda i: (0, 0)),
            pl.BlockSpec((1, d), lambda i: (0, 0)),
            pl.BlockSpec((d, nq), lambda i: (0, 0)),
            pl.BlockSpec((d, nkv), lambda i: (0, 0)),
        ],
        out_specs=[
            pl.BlockSpec((TM, nq), lambda i: (i, 0)),
            pl.BlockSpec((TM, dup), lambda i: (i, 0)),
            pl.BlockSpec((TM, dup), lambda i: (i, 0)),
        ],
        out_shape=[
            jax.ShapeDtypeStruct((rows, nq), _BF16),
            jax.ShapeDtypeStruct((rows, dup), _BF16),
            jax.ShapeDtypeStruct((rows, dup), _BF16),
        ],
        scratch_shapes=[pltpu.VMEM((2, TM, LANES), _F32)],
        compiler_params=_params("arbitrary"),
        name="qkv_rope",
    )(h, gain_q.reshape(1, d), gain_kv.reshape(1, d), w_q, w_kv)


def _swa_kernel(sink_ref, q_ref, kp_ref, kc_ref, km_ref, vp_ref, vc_ref, vm_ref, o_ref):
    n = pl.program_id(0)
    first_block = META_ROW0 // SWA_BLOCK
    meta_lo = META_ROW0 % SWA_BLOCK

    @pl.when(n < first_block)
    def _():
        o_ref[...] = jnp.zeros_like(o_ref)

    @pl.when(n >= first_block)
    def _():
        kb = jnp.concatenate([km_ref[meta_lo:, :], kp_ref[...], kc_ref[...], km_ref[meta_lo:, :]], axis=0)
        vb = jnp.concatenate([vm_ref[meta_lo:, :], vp_ref[...], vc_ref[...], vm_ref[meta_lo:, :]], axis=0)
        lane = lax.broadcasted_iota(jnp.int32, (SWA_HALF, LANES), 1)
        low = lane < SWA_HEAD_DIM
        high = lane >= SWA_HEAD_DIM

        qi0 = lax.broadcasted_iota(jnp.int32, (SWA_HALF, SWA_KEYS), 0)
        col = lax.broadcasted_iota(jnp.int32, (SWA_HALF, SWA_KEYS), 1)
        band_w = SWA_KEYS - N_META
        caps = []
        for hq in range(2):
            qi = qi0 + hq * SWA_HALF
            if hq == 0:
                is_meta = col < N_META
                kj = col - N_META
                m = col
                start = 0
            else:
                is_meta = col >= band_w
                kj = col + (2 * SWA_BLOCK - band_w)
                m = col - band_w
                start = 2 * N_META + 2 * SWA_BLOCK - SWA_KEYS
            band_ok = (kj > qi) & (kj <= qi + SWA_WINDOW) & ((n - 1) * SWA_BLOCK + kj >= FRONT)
            meta_ok = (META_ROW0 + m) <= n * SWA_BLOCK + qi
            ok = (is_meta & meta_ok) | (jnp.logical_not(is_meta) & band_ok)
            caps.append((jnp.where(ok, jnp.inf, NEG_INF).astype(_F32), start))

        ones = jnp.ones((SWA_KEYS, LANES), _BF16)

        def sink_tile(g):
            return jnp.concatenate(
                [jnp.full((SWA_HALF, LANES), sink_ref[g * SWA_GROUP + j] * LOG2E, _F32)
                 for j in range(SWA_GROUP)], axis=0)

        def scores(g, hq):
            cap, start = caps[hq]
            rows = slice(hq * SWA_HALF, (hq + 1) * SWA_HALF)
            parts = []
            for j in range(SWA_GROUP):
                pair = (g * SWA_GROUP + j) // 2
                t = q_ref[rows, pair * LANES:(pair + 1) * LANES]
                parts.append(jnp.where(low if j % 2 == 0 else high, t, jnp.zeros_like(t)))
            lhs = jnp.concatenate(parts, axis=0)
            s = _dot_nt(lhs, kb[start:start + SWA_KEYS, g * LANES:(g + 1) * LANES])
            s = jnp.concatenate(
                [jnp.minimum(s[j * SWA_HALF:(j + 1) * SWA_HALF], cap) for j in range(SWA_GROUP)], axis=0)
            row_max = jnp.max(s, axis=-1, keepdims=True)
            m = jnp.maximum(jnp.broadcast_to(row_max, (SWA_GROUP * SWA_HALF, LANES)), sink_tile(g))
            return s, m

        def probs(s, m, g):
            p = jnp.concatenate([jnp.exp2(s[:, c * LANES:(c + 1) * LANES] - m)
                                 for c in range(SWA_KEYS // LANES)], axis=1).astype(_BF16)
            return p, jnp.exp2(sink_tile(g) - m)

        def outputs(p, e, g, hq):
            _, start = caps[hq]
            rows = slice(hq * SWA_HALF, (hq + 1) * SWA_HALF)
            vw = jnp.concatenate([vb[start:start + SWA_KEYS, g * LANES:(g + 1) * LANES], ones], axis=1)
            od = _dot(p, vw)
            o = od[:, :LANES] / (od[:, LANES:] + e)
            for j in range(0, SWA_GROUP, 2):
                pair = (g * SWA_GROUP + j) // 2
                even = o[j * SWA_HALF:(j + 1) * SWA_HALF]
                odd = o[(j + 1) * SWA_HALF:(j + 2) * SWA_HALF]
                o_ref[rows, pair * LANES:(pair + 1) * LANES] = jnp.where(low, even, odd).astype(o_ref.dtype)

        tiles = [(g, hq) for g in range(SWA_KV_HEADS) for hq in range(2)]
        stage_a, stage_b = {}, {}
        for t in range(len(tiles) + 2):
            if t < len(tiles):
                stage_a[t] = scores(*tiles[t])
            if 0 <= t - 1 < len(tiles):
                stage_b[t - 1] = probs(*stage_a.pop(t - 1), tiles[t - 1][0])
            if 0 <= t - 2 < len(tiles):
                outputs(*stage_b.pop(t - 2), *tiles[t - 2])


def _swa(q, k_dup, v_dup, sinks):
    rows, nq = q.shape
    steps = rows // SWA_BLOCK
    dup = k_dup.shape[1]
    first_block = META_ROW0 // SWA_BLOCK
    blk = pl.BlockSpec((SWA_BLOCK, dup), lambda n: (n, 0))
    prev = pl.BlockSpec((SWA_BLOCK, dup), lambda n: (jnp.maximum(n - 1, 0), 0))
    meta = pl.BlockSpec((SWA_BLOCK, dup), lambda n: (first_block, 0))
    return pl.pallas_call(
        _swa_kernel,
        grid=(steps,),
        in_specs=[
            pl.BlockSpec(memory_space=pltpu.SMEM),
            pl.BlockSpec((SWA_BLOCK, nq), lambda n: (n, 0)),
            prev, blk, meta, prev, blk, meta,
        ],
        out_specs=pl.BlockSpec((SWA_BLOCK, nq), lambda n: (n, 0)),
        out_shape=jax.ShapeDtypeStruct((rows, nq), _BF16),
        compiler_params=_params("parallel"),
        name="swa",
    )(sinks, q, k_dup, k_dup, k_dup, v_dup, v_dup, v_dup)


def kernel(x, meta_tokens, norm_mix, norm_mlp, w_mlp_up, w_mlp_down, a_w_in, a_w_gate_up, a_b_gate, a_norm_out, a_w_out, kv_norm, w_kv, b_w_q, b_sinks, b_w_out, norm_final):
    batch, seq, d = x.shape
    assert batch == 1 and d == D_MODEL and seq % TM == 0
    assert norm_mix.shape[0] == 2 and a_w_in.shape[0] == 1 and b_w_q.shape[0] == 1

    head = jnp.concatenate([jnp.zeros((META_ROW0, d), x.dtype), meta_tokens.astype(x.dtype)], axis=0)
    x2 = x[0]
    bf = lambda w: w.astype(_BF16)
    n_main = GLA_IN - GLA_RANK
    w_in = bf(a_w_in[0])
    w_r = jnp.pad(w_in[:, n_main:], ((0, 0), (0, RANK_PAD - GLA_RANK)))
    w_gate = jnp.pad(bf(a_w_gate_up[0]), ((0, RANK_PAD - GLA_RANK), (0, 0)))

    proj, r, (w_up, w_down, w_out_a, w_q, w_out_b, w_kv_b) = _in_proj(
        head, x2, norm_mix[0], w_in, n_main, w_r, w_mlp_up, w_mlp_down,
        (a_w_out[0], b_w_q[0], b_w_out[0], w_kv))
    o = _gla(proj, r, w_gate, a_b_gate[0])
    h = _gla_out(o, proj, a_norm_out[0], w_out_a, head, x2)
    h = _mlp(h, norm_mlp[0], w_up, w_down, 0)

    q, k_dup, v_dup = _qkv(h, norm_mix[1], kv_norm, w_q, w_kv_b)
    o = _swa(q, k_dup, v_dup, b_sinks[0])
    h = _matmul_residual(o, w_out_b, h)
    return _mlp(h, norm_mlp[1], w_up, w_down, 1, final_gain=norm_final)[None]
```

```python
import functools
import math

import jax
import jax.numpy as jnp
from jax import lax
from jax.experimental import pallas as pl
from jax.experimental.pallas import tpu as pltpu

D_MODEL = 2048
N_META = 16
D_FF = 4 * D_MODEL
RMS_EPS = 1e-6
NEG_INF = -1e30

GLA_HEADS = 4
GLA_DK = 256
GLA_DV = 512
GLA_QK = GLA_HEADS * GLA_DK
GLA_VD = GLA_HEADS * GLA_DV
GLA_RANK = 16
GLA_GATE_NORM = 16.0
GLA_IN = 2 * GLA_QK + 2 * GLA_VD + GLA_RANK

SWA_HEAD_DIM = 64
SWA_Q_HEADS = 32
SWA_KV_HEADS = 4
SWA_GROUP = SWA_Q_HEADS // SWA_KV_HEADS
SWA_WINDOW = 128
ROPE_THETA = 10000.0

LANES = 128
BF16_ROWS = 16
FRONT = 512
META_ROW0 = FRONT - N_META
TM = 512
IN_PROJ_TN = 1024
MLP_TF = 1024
GLA_CHUNK = 256
GLA_SUB = 64
SWA_BLOCK = 128
SWA_HALF = SWA_BLOCK // 2
SWA_KEYS = 2 * SWA_BLOCK
LOG2E = math.log2(math.e)
RANK_PAD = LANES
MAX_CAST_STEPS = 128
VMEM_LIMIT = 58 * 1024 * 1024

_F32 = jnp.float32
_BF16 = jnp.bfloat16


def _dot(a, b):
    return jnp.dot(a, b, preferred_element_type=_F32)


def _dot_nt(a, b):
    return lax.dot_general(a, b, (((1,), (1,)), ((), ())), preferred_element_type=_F32)


def _dot_tn(a, b):
    return lax.dot_general(a, b, (((0,), (0,)), ((), ())), preferred_element_type=_F32)


def _params(*semantics):
    return pltpu.CompilerParams(dimension_semantics=semantics, vmem_limit_bytes=VMEM_LIMIT)


def _rms_scale(x):
    return lax.rsqrt(jnp.mean(x * x, axis=-1, keepdims=True) + RMS_EPS)


def _normed(x, g_ref):
    return (x * _rms_scale(x) * g_ref[...]).astype(_BF16)


def _norm_next_rows(src_ref, g_ref, xn_ref, slot, j, n_steps):
    slab = -(-TM // (n_steps * BF16_ROWS)) * BF16_ROWS
    start = pl.multiple_of(jnp.minimum(j * slab, TM - slab), BF16_ROWS)
    xn_ref[slot, pl.ds(start, slab), :] = _normed(src_ref[pl.ds(start, slab), :], g_ref)


def _stream_tile(head_ref, x_ref, i):
    return jnp.where(i == 0, head_ref[...], x_ref[...])


def _stream_specs(d, grid_rank):
    if grid_rank == 1:
        return [pl.BlockSpec((TM, d), lambda i: (0, 0)),
                pl.BlockSpec((TM, d), lambda i: (jnp.maximum(i - 1, 0), 0))]
    return [pl.BlockSpec((TM, d), lambda i, j: (0, 0)),
            pl.BlockSpec((TM, d), lambda i, j: (jnp.maximum(i - 1, 0), 0))]


def _cast_slabs(src_refs, dst_refs):
    (wu_src, wd_src, *small_src), (wu, wd, *small) = src_refs, dst_refs
    for t in range(wu.shape[0]):
        wu[t] = wu_src[:, t * MLP_TF:(t + 1) * MLP_TF].astype(_BF16)
    wd[...] = wd_src[...].astype(_BF16)
    for src, dst in zip(small_src, small):
        dst[...] = src[...].astype(_BF16)


def _cast_specs(w_up, w_down, small, n_tiles, cast_steps):
    layers, d, ff = w_up.shape
    per_layer = cast_steps // layers
    ru, rd = d // per_layer, ff // per_layer
    step = lambda i, j: jnp.minimum(i * n_tiles + j, cast_steps - 1)
    layer = lambda i, j: step(i, j) // per_layer
    slab = lambda i, j: step(i, j) % per_layer
    in_specs = [pl.BlockSpec((None, ru, ff), lambda i, j: (layer(i, j), slab(i, j), 0)),
                pl.BlockSpec((None, rd, d), lambda i, j: (layer(i, j), slab(i, j), 0))]
    out_specs = [pl.BlockSpec((None, ff // MLP_TF, ru, MLP_TF), lambda i, j: (layer(i, j), 0, slab(i, j), 0)),
                 pl.BlockSpec((None, rd, d), lambda i, j: (layer(i, j), slab(i, j), 0))]
    out_shapes = [jax.ShapeDtypeStruct((layers, ff // MLP_TF, d, MLP_TF), _BF16),
                  jax.ShapeDtypeStruct((layers, ff, d), _BF16)]
    for w in small:
        r, c = w.shape
        spec = pl.BlockSpec((r // cast_steps, c), lambda i, j: (step(i, j), 0))
        in_specs.append(spec)
        out_specs.append(spec)
        out_shapes.append(jax.ShapeDtypeStruct((r, c), _BF16))
    return in_specs, out_specs, out_shapes


def _in_proj_kernel(*refs, n_tiles, n_cast):
    head_ref, xnext_ref, g_ref, w_ref, wr_ref = refs[:5]
    cast_src = refs[5:5 + n_cast]
    o_ref, r_ref = refs[5 + n_cast:7 + n_cast]
    cast_dst = refs[7 + n_cast:7 + 2 * n_cast]
    xn_ref = refs[-1]
    i, j = pl.program_id(0), pl.program_id(1)
    slot = i % 2

    @pl.when((i == 0) & (j == 0))
    def _():
        xn_ref[0] = _normed(head_ref[...], g_ref)

    def step(first):
        xn = xn_ref[slot]
        cols = pl.ds(pl.multiple_of(j * IN_PROJ_TN, IN_PROJ_TN), IN_PROJ_TN)
        o_ref[...] = _dot(xn, w_ref[:, cols]).astype(o_ref.dtype)
        if first:
            r_ref[...] = _dot(xn, wr_ref[...]).astype(r_ref.dtype)
        _norm_next_rows(xnext_ref, g_ref, xn_ref, 1 - slot, j, n_tiles)
        _cast_slabs(cast_src, cast_dst)

    pl.when(j == 0)(functools.partial(step, True))
    pl.when(j > 0)(functools.partial(step, False))


def _in_proj(head, x, gain, w, n, w_r, w_up, w_down, small):
    d = x.shape[1]
    rows = head.shape[0] + x.shape[0]
    tn = IN_PROJ_TN
    n_tiles = n // tn
    nr = w_r.shape[1]
    last_x = x.shape[0] // TM - 1
    steps = rows // TM * n_tiles
    cast_steps = 1 << (min(steps, MAX_CAST_STEPS).bit_length() - 1)
    cast_in, cast_out, cast_shapes = _cast_specs(w_up, w_down, small, n_tiles, cast_steps)
    proj, r, *casts = pl.pallas_call(
        functools.partial(_in_proj_kernel, n_tiles=n_tiles, n_cast=len(cast_in)),
        grid=(rows // TM, n_tiles),
        in_specs=[
            pl.BlockSpec((TM, d), lambda i, j: (0, 0)),
            pl.BlockSpec((TM, d), lambda i, j: (jnp.minimum(i, last_x), 0)),
            pl.BlockSpec((1, d), lambda i, j: (0, 0)),
            pl.BlockSpec(w.shape, lambda i, j: (0, 0), pipeline_mode=pl.Buffered(1)),
            pl.BlockSpec((d, nr), lambda i, j: (0, 0)),
        ] + cast_in,
        out_specs=[pl.BlockSpec((TM, tn), lambda i, j: (i, j)),
                   pl.BlockSpec((TM, nr), lambda i, j: (i, 0))] + cast_out,
        out_shape=[jax.ShapeDtypeStruct((rows, n), _BF16),
                   jax.ShapeDtypeStruct((rows, nr), _BF16)] + cast_shapes,
        scratch_shapes=[pltpu.VMEM((2, TM, d), _BF16)],
        compiler_params=_params("arbitrary", "arbitrary"),
        name="in_proj",
    )(head, x, gain.reshape(1, d), w, w_r, w_up, w_down, *small)
    return proj, r, casts


def _matmul_residual_kernel(a_ref, w_ref, h_ref, o_ref):
    o_ref[...] = h_ref[...] + _dot(a_ref[...], w_ref[...])


def _matmul_residual(a, w, h):
    rows, k = a.shape
    d = w.shape[1]
    return pl.pallas_call(
        _matmul_residual_kernel,
        grid=(rows // TM,),
        in_specs=[
            pl.BlockSpec((TM, k), lambda i: (i, 0)),
            pl.BlockSpec((k, d), lambda i: (0, 0)),
            pl.BlockSpec((TM, d), lambda i: (i, 0)),
        ],
        out_specs=pl.BlockSpec((TM, d), lambda i: (i, 0)),
        out_shape=jax.ShapeDtypeStruct((rows, d), _F32),
        compiler_params=_params("parallel"),
        name="matmul_residual",
    )(a, w, h)


def _mlp_kernel(hnext_ref, g_ref, wu_ref, wd_ref, gf_ref, o_ref, xn_ref, res_ref, *, n_tiles, final):
    t = pl.program_id(0)
    s = jnp.maximum(t - 1, 0)
    j = s % n_tiles
    slot = (s // n_tiles) % 2

    @pl.when(t == 0)
    def _():
        x = hnext_ref[...]
        xn_ref[0] = _normed(x, g_ref)
        res_ref[...] = x

    def step(first):
        u = jnp.maximum(_dot(xn_ref[slot], wu_ref[...]), 0.0)
        part = _dot((u * u).astype(_BF16), wd_ref[...])
        if first:
            o_ref[...] = res_ref[...] + part
        else:
            o_ref[...] += part
        slab = TM // n_tiles
        rows = pl.ds(pl.multiple_of(j * slab, slab), slab)
        x = hnext_ref[rows, :]
        res_ref[rows, :] = x
        xn_ref[1 - slot, rows, :] = _normed(x, g_ref)

    pl.when((t > 0) & (j == 0))(functools.partial(step, True))
    pl.when((t > 0) & (j > 0))(functools.partial(step, False))

    if final:
        @pl.when((t > 0) & (j == n_tiles - 1))
        def _():
            y = o_ref[...]
            o_ref[...] = y * _rms_scale(y) * gf_ref[...]


def _mlp(h, gain, w_up, w_down, layer, final_gain=None):
    rows, d = h.shape
    _, n_tiles, _, tf = w_up.shape
    assert TM % (n_tiles * BF16_ROWS) == 0
    final = final_gain is not None
    skip = FRONT // TM if final else 0
    out_rows = rows - skip * TM
    last = rows // TM - 1
    gf = (final_gain if final else gain).reshape(1, d)
    hidden = lambda t: jnp.maximum(t - 1, 0) % n_tiles
    row_tile = lambda t: jnp.maximum(t - 1, 0) // n_tiles
    staged = lambda t: jnp.minimum((t + n_tiles - 1) // n_tiles + skip, last)
    return pl.pallas_call(
        functools.partial(_mlp_kernel, n_tiles=n_tiles, final=final),
        grid=(out_rows // TM * n_tiles + 1,),
        in_specs=[
            pl.BlockSpec((TM, d), lambda t: (staged(t), 0)),
            pl.BlockSpec((1, d), lambda t: (0, 0)),
            pl.BlockSpec((None, None, d, tf), lambda t: (layer, hidden(t), 0, 0)),
            pl.BlockSpec((None, tf, d), lambda t: (layer, hidden(t), 0)),
            pl.BlockSpec((1, d), lambda t: (0, 0)),
        ],
        out_specs=pl.BlockSpec((TM, d), lambda t: (row_tile(t), 0)),
        out_shape=jax.ShapeDtypeStruct((out_rows, d), _F32),
        scratch_shapes=[pltpu.VMEM((2, TM, d), _BF16), pltpu.VMEM((TM, d), _F32)],
        compiler_params=_params("arbitrary"),
        name="mlp_final" if final else "mlp",
    )(h, gain.reshape(1, d), w_up, w_down, gf)


def _gla_kernel(q_ref, k_ref, v_ref, r_ref, wg_ref, bg_ref, o_ref, s_ref, b_ref):
    step = pl.program_id(0)
    c = GLA_CHUNK
    nsub = c // GLA_SUB

    @pl.when(step == 0)
    def _():
        s_ref[...] = jnp.zeros_like(s_ref)
        b_ref[0] = jnp.zeros((c, GLA_QK), _F32)

    b_all = b_ref[step % 2]

    causal = [lax.broadcasted_iota(jnp.int32, (GLA_SUB, c), 1)
              <= lax.broadcasted_iota(jnp.int32, (GLA_SUB, c), 0) + i * GLA_SUB for i in range(nsub)]

    def rows_of(t, i):
        return t[i * GLA_SUB:(i + 1) * GLA_SUB]

    def scaled(t, e):
        return t * jnp.exp2(e).astype(_BF16)

    q_shift = math.log2(GLA_DK ** -0.5)

    def attention_weights(h):
        ks = slice(h * GLA_DK, (h + 1) * GLA_DK)
        b = b_all[:, ks]
        beta = [jnp.zeros((1, GLA_DK), _F32)] + [b[i * GLA_SUB - 1:i * GLA_SUB, :] for i in range(1, nsub + 1)]
        q_exp = jnp.concatenate([rows_of(b, i) - beta[i] for i in range(nsub)], axis=0)
        k_exp = jnp.concatenate([beta[i + 1] - rows_of(b, i) for i in range(nsub)], axis=0)
        qs = q_ref[:, ks] * jnp.exp2(q_exp + q_shift).astype(_BF16)
        kd = k_ref[:, ks] * jnp.exp2(k_exp).astype(_BF16)
        a_rows = []
        for i in range(nsub):
            kt = jnp.concatenate([scaled(rows_of(kd, j), beta[i] - beta[j + 1]) for j in range(i + 1)]
                                 + [rows_of(kd, j) for j in range(i + 1, nsub)], axis=0)
            a_rows.append(jnp.where(causal[i], _dot_nt(rows_of(qs, i), kt), 0.0).astype(_BF16))
        qe = jnp.concatenate([scaled(rows_of(qs, i), beta[i]) for i in range(nsub)], axis=0)
        k_dec = jnp.concatenate([scaled(rows_of(kd, j), beta[nsub] - beta[j + 1]) for j in range(nsub)], axis=0)
        decay = jnp.exp2(jnp.broadcast_to(beta[nsub], (LANES, GLA_DK))).T
        return jnp.concatenate([jnp.concatenate(a_rows, axis=0), qe], axis=1), k_dec, decay

    def outputs(h, lhs, k_dec, decay):
        vs = slice(h * GLA_DV, (h + 1) * GLA_DV)
        vh = v_ref[:, vs]
        state = s_ref[h]
        o_ref[:, vs] = _dot(lhs, jnp.concatenate([vh, state.astype(_BF16)], axis=0)).astype(o_ref.dtype)
        s_ref[h] = state * jnp.concatenate([decay] * (GLA_DV // LANES), axis=1) + _dot_tn(k_dec, vh)

    def gate_logits():
        return _dot(r_ref[...], wg_ref[...]) + bg_ref[...]

    def gate_rows(x, i):
        xs = rows_of(x, i)
        gk = (jnp.minimum(xs, 0.0) - jnp.log1p(jnp.exp(-jnp.abs(xs)))) * (LOG2E / GLA_GATE_NORM)
        row = (step + 1) * c + i * GLA_SUB + lax.broadcasted_iota(jnp.int32, (GLA_SUB, 1), 0)
        gk = jnp.where(row >= META_ROW0, gk, 0.0)
        hi = gk.astype(_BF16)
        return hi, (gk - hi.astype(_F32)).astype(_BF16)

    def cumulative(pairs):
        tri = (lax.broadcasted_iota(jnp.int32, (c, c), 0)
               >= lax.broadcasted_iota(jnp.int32, (c, c), 1)).astype(_BF16)
        hi = jnp.concatenate([p[0] for p in pairs], axis=0)
        lo = jnp.concatenate([p[1] for p in pairs], axis=0)
        return _dot(tri, hi) + _dot(tri, lo)

    assert nsub == GLA_HEADS
    x_gate = gate_logits()
    pairs = []
    pending = None
    for h in range(GLA_HEADS):
        ready = attention_weights(h)
        pairs.append(gate_rows(x_gate, h))
        if pending is not None:
            outputs(h - 1, *pending)
        pending = ready
    b_next = cumulative(pairs)
    outputs(GLA_HEADS - 1, *pending)
    b_ref[(step + 1) % 2] = b_next


def _gla(proj, r, w_gate, b_gate):
    rows = proj.shape[0]
    c = GLA_CHUNK
    steps = rows // c
    last = steps - 1
    return pl.pallas_call(
        _gla_kernel,
        grid=(steps,),
        in_specs=[
            pl.BlockSpec((c, GLA_QK), lambda s: (s, 0)),
            pl.BlockSpec((c, GLA_QK), lambda s: (s, 1)),
            pl.BlockSpec((c, GLA_VD), lambda s: (s, 1)),
            pl.BlockSpec((c, RANK_PAD), lambda s: (jnp.minimum(s + 1, last), 0)),
            pl.BlockSpec((RANK_PAD, GLA_QK), lambda s: (0, 0)),
            pl.BlockSpec((1, GLA_QK), lambda s: (0, 0)),
        ],
        out_specs=pl.BlockSpec((c, GLA_VD), lambda s: (s, 0)),
        out_shape=jax.ShapeDtypeStruct((rows, GLA_VD), _BF16),
        scratch_shapes=[pltpu.VMEM((GLA_HEADS, GLA_DK, GLA_DV), _F32),
                        pltpu.VMEM((2, c, GLA_QK), _F32)],
        compiler_params=_params("arbitrary"),
        name="gla",
    )(proj, proj, proj, r, w_gate, b_gate.reshape(1, GLA_QK))


def _gla_out_kernel(o_ref, g_ref, no_ref, w_ref, head_ref, x_ref, out_ref):
    parts = []
    for h in range(GLA_HEADS):
        vs = slice(h * GLA_DV, (h + 1) * GLA_DV)
        o = o_ref[:, vs].astype(_F32)
        hg = 0.5 * g_ref[:, vs].astype(_F32)
        silu = hg * jnp.tanh(hg) + hg
        parts.append((o * _rms_scale(o) * no_ref[...] * silu).astype(_BF16))
    y = jnp.concatenate(parts, axis=1)
    out_ref[...] = _stream_tile(head_ref, x_ref, pl.program_id(0)) + _dot(y, w_ref[...])


def _gla_out(o, proj, norm_out, w, head, x):
    rows, k = o.shape
    d = w.shape[1]
    return pl.pallas_call(
        _gla_out_kernel,
        grid=(rows // TM,),
        in_specs=[
            pl.BlockSpec((TM, k), lambda i: (i, 0)),
            pl.BlockSpec((TM, GLA_VD), lambda i: (i, 2)),
            pl.BlockSpec((1, GLA_DV), lambda i: (0, 0)),
            pl.BlockSpec((k, d), lambda i: (0, 0)),
        ] + _stream_specs(d, 1),
        out_specs=pl.BlockSpec((TM, d), lambda i: (i, 0)),
        out_shape=jax.ShapeDtypeStruct((rows, d), _F32),
        compiler_params=_params("parallel"),
        name="gla_out",
    )(o, proj, norm_out.reshape(1, GLA_DV), w, head, x)


def _swap_halves(x, width):
    lane = lax.broadcasted_iota(jnp.int32, x.shape, 1)
    first = (lane % (2 * width)) < width
    return jnp.where(first, pltpu.roll(x, LANES - width, 1), pltpu.roll(x, width, 1))


def _qkv_kernel(h_ref, gq_ref, gkv_ref, wq_ref, wkv_ref, q_ref, k_ref, v_ref, tab_ref):
    half = SWA_HEAD_DIM // 2
    i = pl.program_id(0)

    def inv_freq(shape):
        lane = lax.broadcasted_iota(jnp.int32, shape, 1)
        return jnp.exp((lane % half).astype(_F32) * (-math.log(ROPE_THETA) / half))

    @pl.when(i == 0)
    def _():
        off = lax.broadcasted_iota(jnp.int32, (TM, LANES), 0).astype(_F32) * inv_freq((TM, LANES))
        tab_ref[0] = jnp.cos(off)
        tab_ref[1] = jnp.sin(off)

    x = h_ref[...]
    xhat = x * _rms_scale(x)
    q = _dot((xhat * gq_ref[...]).astype(_BF16), wq_ref[...])
    kv = _dot((xhat * gkv_ref[...]).astype(_BF16), wkv_ref[...])

    base = (i * TM - META_ROW0).astype(_F32) * inv_freq((8, LANES))[:1]
    cos_b, sin_b = jnp.cos(base), jnp.sin(base)
    cos = cos_b * tab_ref[0] - sin_b * tab_ref[1]
    sin = sin_b * tab_ref[0] + cos_b * tab_ref[1]
    lane = lax.broadcasted_iota(jnp.int32, (TM, LANES), 1)
    sin = jnp.where((lane % SWA_HEAD_DIM) < half, -sin, sin)

    def rope(t):
        return t * cos + _swap_halves(t, half) * sin

    qscale = SWA_HEAD_DIM ** -0.5 * LOG2E
    for p in range(SWA_Q_HEADS // 2):
        sl = slice(p * LANES, (p + 1) * LANES)
        q_ref[:, sl] = (rope(q[:, sl]) * qscale).astype(q_ref.dtype)

    low = (lane % LANES) < SWA_HEAD_DIM
    kv_width = SWA_KV_HEADS * SWA_HEAD_DIM
    for p in range(SWA_KV_HEADS // 2):
        sl = slice(p * LANES, (p + 1) * LANES)
        for t, o_ref in ((rope(kv[:, sl]), k_ref), (kv[:, kv_width + p * LANES:kv_width + (p + 1) * LANES], v_ref)):
            swapped = pltpu.roll(t, SWA_HEAD_DIM, 1)
            o_ref[:, (2 * p) * LANES:(2 * p + 1) * LANES] = jnp.where(low, t, swapped).astype(o_ref.dtype)
            o_ref[:, (2 * p + 1) * LANES:(2 * p + 2) * LANES] = jnp.where(low, swapped, t).astype(o_ref.dtype)


def _qkv(h, gain_q, gain_kv, w_q, w_kv):
    rows, d = h.shape
    nq = w_q.shape[1]
    nkv = w_kv.shape[1]
    dup = SWA_KV_HEADS * LANES
    return pl.pallas_call(
        _qkv_kernel,
        grid=(rows // TM,),
        in_specs=[
            pl.BlockSpec((TM, d), lambda i: (i, 0)),
            pl.BlockSpec((1, d), lambda i: (0, 0)),
            pl.BlockSpec((1, d), lambda i: (0, 0)),
            pl.BlockSpec((d, nq), lambda i: (0, 0)),
            pl.BlockSpec((d, nkv), lambda i: (0, 0)),
        ],
        out_specs=[
            pl.BlockSpec((TM, nq), lambda i: (i, 0)),
            pl.BlockSpec((TM, dup), lambda i: (i, 0)),
            pl.BlockSpec((TM, dup), lambda i: (i, 0)),
        ],
        out_shape=[
            jax.ShapeDtypeStruct((rows, nq), _BF16),
            jax.ShapeDtypeStruct((rows, dup), _BF16),
            jax.ShapeDtypeStruct((rows, dup), _BF16),
        ],
        scratch_shapes=[pltpu.VMEM((2, TM, LANES), _F32)],
        compiler_params=_params("arbitrary"),
        name="qkv_rope",
    )(h, gain_q.reshape(1, d), gain_kv.reshape(1, d), w_q, w_kv)


def _swa_kernel(sink_ref, q_ref, kp_ref, kc_ref, km_ref, vp_ref, vc_ref, vm_ref, o_ref):
    n = pl.program_id(0)
    first_block = META_ROW0 // SWA_BLOCK
    meta_lo = META_ROW0 % SWA_BLOCK

    @pl.when(n < first_block)
    def _():
        o_ref[...] = jnp.zeros_like(o_ref)

    @pl.when(n >= first_block)
    def _():
        kb = jnp.concatenate([km_ref[meta_lo:, :], kp_ref[...], kc_ref[...], km_ref[meta_lo:, :]], axis=0)
        vb = jnp.concatenate([vm_ref[meta_lo:, :], vp_ref[...], vc_ref[...], vm_ref[meta_lo:, :]], axis=0)
        lane = lax.broadcasted_iota(jnp.int32, (SWA_HALF, LANES), 1)
        low = lane < SWA_HEAD_DIM
        high = lane >= SWA_HEAD_DIM

        qi0 = lax.broadcasted_iota(jnp.int32, (SWA_HALF, SWA_KEYS), 0)
        col = lax.broadcasted_iota(jnp.int32, (SWA_HALF, SWA_KEYS), 1)
        band_w = SWA_KEYS - N_META
        caps = []
        for hq in range(2):
            qi = qi0 + hq * SWA_HALF
            if hq == 0:
                is_meta = col < N_META
                kj = col - N_META
                m = col
                start = 0
            else:
                is_meta = col >= band_w
                kj = col + (2 * SWA_BLOCK - band_w)
                m = col - band_w
                start = 2 * N_META + 2 * SWA_BLOCK - SWA_KEYS
            band_ok = (kj > qi) & (kj <= qi + SWA_WINDOW) & ((n - 1) * SWA_BLOCK + kj >= FRONT)
            meta_ok = (META_ROW0 + m) <= n * SWA_BLOCK + qi
            ok = (is_meta & meta_ok) | (jnp.logical_not(is_meta) & band_ok)
            caps.append((jnp.where(ok, jnp.inf, NEG_INF).astype(_F32), start))

        ones = jnp.ones((SWA_KEYS, LANES), _BF16)

        def sink_tile(g):
            return jnp.concatenate(
                [jnp.full((SWA_HALF, LANES), sink_ref[g * SWA_GROUP + j] * LOG2E, _F32)
                 for j in range(SWA_GROUP)], axis=0)

        def scores(g, hq):
            cap, start = caps[hq]
            rows = slice(hq * SWA_HALF, (hq + 1) * SWA_HALF)
            parts = []
            for j in range(SWA_GROUP):
                pair = (g * SWA_GROUP + j) // 2
                t = q_ref[rows, pair * LANES:(pair + 1) * LANES]
                parts.append(jnp.where(low if j % 2 == 0 else high, t, jnp.zeros_like(t)))
            lhs = jnp.concatenate(parts, axis=0)
            s = _dot_nt(lhs, kb[start:start + SWA_KEYS, g * LANES:(g + 1) * LANES])
            s = jnp.concatenate(
                [jnp.minimum(s[j * SWA_HALF:(j + 1) * SWA_HALF], cap) for j in range(SWA_GROUP)], axis=0)
            row_max = jnp.max(s, axis=-1, keepdims=True)
            m = jnp.maximum(jnp.broadcast_to(row_max, (SWA_GROUP * SWA_HALF, LANES)), sink_tile(g))
            return s, m

        def probs(s, m, g):
            p = jnp.concatenate([jnp.exp2(s[:, c * LANES:(c + 1) * LANES] - m)
                                 for c in range(SWA_KEYS // LANES)], axis=1).astype(_BF16)
            return p, jnp.exp2(sink_tile(g) - m)

        def outputs(p, e, g, hq):
            _, start = caps[hq]
            rows = slice(hq * SWA_HALF, (hq + 1) * SWA_HALF)
            vw = jnp.concatenate([vb[start:start + SWA_KEYS, g * LANES:(g + 1) * LANES], ones], axis=1)
            od = _dot(p, vw)
            o = od[:, :LANES] / (od[:, LANES:] + e)
            for j in range(0, SWA_GROUP, 2):
                pair = (g * SWA_GROUP + j) // 2
                even = o[j * SWA_HALF:(j + 1) * SWA_HALF]
                odd = o[(j + 1) * SWA_HALF:(j + 2) * SWA_HALF]
                o_ref[rows, pair * LANES:(pair + 1) * LANES] = jnp.where(low, even, odd).astype(o_ref.dtype)

        tiles = [(g, hq) for g in range(SWA_KV_HEADS) for hq in range(2)]
        stage_a, stage_b = {}, {}
        for t in range(len(tiles) + 2):
            if t < len(tiles):
                stage_a[t] = scores(*tiles[t])
            if 0 <= t - 1 < len(tiles):
                stage_b[t - 1] = probs(*stage_a.pop(t - 1), tiles[t - 1][0])
            if 0 <= t - 2 < len(tiles):
                outputs(*stage_b.pop(t - 2), *tiles[t - 2])


def _swa(q, k_dup, v_dup, sinks):
    rows, nq = q.shape
    steps = rows // SWA_BLOCK
    dup = k_dup.shape[1]
    first_block = META_ROW0 // SWA_BLOCK
    blk = pl.BlockSpec((SWA_BLOCK, dup), lambda n: (n, 0))
    prev = pl.BlockSpec((SWA_BLOCK, dup), lambda n: (jnp.maximum(n - 1, 0), 0))
    meta = pl.BlockSpec((SWA_BLOCK, dup), lambda n: (first_block, 0))
    return pl.pallas_call(
        _swa_kernel,
        grid=(steps,),
        in_specs=[
            pl.BlockSpec(memory_space=pltpu.SMEM),
            pl.BlockSpec((SWA_BLOCK, nq), lambda n: (n, 0)),
            prev, blk, meta, prev, blk, meta,
        ],
        out_specs=pl.BlockSpec((SWA_BLOCK, nq), lambda n: (n, 0)),
        out_shape=jax.ShapeDtypeStruct((rows, nq), _BF16),
        compiler_params=_params("parallel"),
        name="swa",
    )(sinks, q, k_dup, k_dup, k_dup, v_dup, v_dup, v_dup)


def kernel(x, meta_tokens, norm_mix, norm_mlp, w_mlp_up, w_mlp_down, a_w_in, a_w_gate_up, a_b_gate, a_norm_out, a_w_out, kv_norm, w_kv, b_w_q, b_sinks, b_w_out, norm_final):
    batch, seq, d = x.shape
    assert batch == 1 and d == D_MODEL and seq % TM == 0
    assert norm_mix.shape[0] == 2 and a_w_in.shape[0] == 1 and b_w_q.shape[0] == 1

    head = jnp.concatenate([jnp.zeros((META_ROW0, d), x.dtype), meta_tokens.astype(x.dtype)], axis=0)
    x2 = x[0]
    bf = lambda w: w.astype(_BF16)
    n_main = GLA_IN - GLA_RANK
    w_in = bf(a_w_in[0])
    w_r = jnp.pad(w_in[:, n_main:], ((0, 0), (0, RANK_PAD - GLA_RANK)))
    w_gate = jnp.pad(bf(a_w_gate_up[0]), ((0, RANK_PAD - GLA_RANK), (0, 0)))

    proj, r, (w_up, w_down, w_out_a, w_q, w_out_b, w_kv_b) = _in_proj(
        head, x2, norm_mix[0], w_in, n_main, w_r, w_mlp_up, w_mlp_down,
        (a_w_out[0], b_w_q[0], b_w_out[0], w_kv))
    o = _gla(proj, r, w_gate, a_b_gate[0])
    h = _gla_out(o, proj, a_norm_out[0], w_out_a, head, x2)
    h = _mlp(h, norm_mlp[0], w_up, w_down, 0)

    q, k_dup, v_dup = _qkv(h, norm_mix[1], kv_norm, w_q, w_kv_b)
    o = _swa(q, k_dup, v_dup, b_sinks[0])
    h = _matmul_residual(o, w_out_b, h)
    return _mlp(h, norm_mlp[1], w_up, w_down, 1, final_gain=norm_final)[None]
```

```python
import functools
import math

import jax
import jax.numpy as jnp
from jax import lax
from jax.experimental import pallas as pl
from jax.experimental.pallas import tpu as pltpu

D_MODEL = 2048
N_META = 16
D_FF = 4 * D_MODEL
RMS_EPS = 1e-6
NEG_INF = -1e30

GLA_HEADS = 4
GLA_DK = 256
GLA_DV = 512
GLA_QK = GLA_HEADS * GLA_DK
GLA_VD = GLA_HEADS * GLA_DV
GLA_RANK = 16
GLA_GATE_NORM = 16.0
GLA_IN = 2 * GLA_QK + 2 * GLA_VD + GLA_RANK

SWA_HEAD_DIM = 64
SWA_Q_HEADS = 32
SWA_KV_HEADS = 4
SWA_GROUP = SWA_Q_HEADS // SWA_KV_HEADS
SWA_WINDOW = 128
ROPE_THETA = 10000.0

LANES = 128
BF16_ROWS = 16
FRONT = 512
META_ROW0 = FRONT - N_META
TM = 512
IN_PROJ_TN = 1024
MLP_TF = 1024
GLA_CHUNK = 256
GLA_SUB = 64
SWA_BLOCK = 128
SWA_HALF = SWA_BLOCK // 2
SWA_KEYS = 2 * SWA_BLOCK
LOG2E = math.log2(math.e)
RANK_PAD = LANES
MAX_CAST_STEPS = 128
VMEM_LIMIT = 58 * 1024 * 1024

_F32 = jnp.float32
_BF16 = jnp.bfloat16


def _dot(a, b):
    return jnp.dot(a, b, preferred_element_type=_F32)


def _dot_nt(a, b):
    return lax.dot_general(a, b, (((1,), (1,)), ((), ())), preferred_element_type=_F32)


def _dot_tn(a, b):
    return lax.dot_general(a, b, (((0,), (0,)), ((), ())), preferred_element_type=_F32)


def _params(*semantics):
    return pltpu.CompilerParams(dimension_semantics=semantics, vmem_limit_bytes=VMEM_LIMIT)


def _rms_scale(x):
    return lax.rsqrt(jnp.mean(x * x, axis=-1, keepdims=True) + RMS_EPS)


def _normed(x, g_ref):
    return (x * _rms_scale(x) * g_ref[...]).astype(_BF16)


def _norm_next_rows(src_ref, g_ref, xn_ref, slot, j, n_steps):
    slab = -(-TM // (n_steps * BF16_ROWS)) * BF16_ROWS
    start = pl.multiple_of(jnp.minimum(j * slab, TM - slab), BF16_ROWS)
    xn_ref[slot, pl.ds(start, slab), :] = _normed(src_ref[pl.ds(start, slab), :], g_ref)


def _stream_tile(head_ref, x_ref, i):
    return jnp.where(i == 0, head_ref[...], x_ref[...])


def _stream_specs(d, grid_rank):
    if grid_rank == 1:
        return [pl.BlockSpec((TM, d), lambda i: (0, 0)),
                pl.BlockSpec((TM, d), lambda i: (jnp.maximum(i - 1, 0), 0))]
    return [pl.BlockSpec((TM, d), lambda i, j: (0, 0)),
            pl.BlockSpec((TM, d), lambda i, j: (jnp.maximum(i - 1, 0), 0))]


def _cast_slabs(src_refs, dst_refs):
    (wu_src, wd_src, *small_src), (wu, wd, *small) = src_refs, dst_refs
    for t in range(wu.shape[0]):
        wu[t] = wu_src[:, t * MLP_TF:(t + 1) * MLP_TF].astype(_BF16)
    wd[...] = wd_src[...].astype(_BF16)
    for src, dst in zip(small_src, small):
        dst[...] = src[...].astype(_BF16)


def _cast_specs(w_up, w_down, small, n_tiles, cast_steps):
    layers, d, ff = w_up.shape
    per_layer = cast_steps // layers
    ru, rd = d // per_layer, ff // per_layer
    step = lambda i, j: jnp.minimum(i * n_tiles + j, cast_steps - 1)
    layer = lambda i, j: step(i, j) // per_layer
    slab = lambda i, j: step(i, j) % per_layer
    in_specs = [pl.BlockSpec((None, ru, ff), lambda i, j: (layer(i, j), slab(i, j), 0)),
                pl.BlockSpec((None, rd, d), lambda i, j: (layer(i, j), slab(i, j), 0))]
    out_specs = [pl.BlockSpec((None, ff // MLP_TF, ru, MLP_TF), lambda i, j: (layer(i, j), 0, slab(i, j), 0)),
                 pl.BlockSpec((None, rd, d), lambda i, j: (layer(i, j), slab(i, j), 0))]
    out_shapes = [jax.ShapeDtypeStruct((layers, ff // MLP_TF, d, MLP_TF), _BF16),
                  jax.ShapeDtypeStruct((layers, ff, d), _BF16)]
    for w in small:
        r, c = w.shape
        spec = pl.BlockSpec((r // cast_steps, c), lambda i, j: (step(i, j), 0))
        in_specs.append(spec)
        out_specs.append(spec)
        out_shapes.append(jax.ShapeDtypeStruct((r, c), _BF16))
    return in_specs, out_specs, out_shapes


def _in_proj_kernel(*refs, n_tiles, n_cast):
    head_ref, xnext_ref, g_ref, w_ref, wr_ref = refs[:5]
    cast_src = refs[5:5 + n_cast]
    o_ref, r_ref = refs[5 + n_cast:7 + n_cast]
    cast_dst = refs[7 + n_cast:7 + 2 * n_cast]
    xn_ref = refs[-1]
    i, j = pl.program_id(0), pl.program_id(1)
    slot = i % 2

    @pl.when((i == 0) & (j == 0))
    def _():
        xn_ref[0] = _normed(head_ref[...], g_ref)

    def step(first):
        xn = xn_ref[slot]
        cols = pl.ds(pl.multiple_of(j * IN_PROJ_TN, IN_PROJ_TN), IN_PROJ_TN)
        o_ref[...] = _dot(xn, w_ref[:, cols]).astype(o_ref.dtype)
        if first:
            r_ref[...] = _dot(xn, wr_ref[...]).astype(r_ref.dtype)
        _norm_next_rows(xnext_ref, g_ref, xn_ref, 1 - slot, j, n_tiles)
        _cast_slabs(cast_src, cast_dst)

    pl.when(j == 0)(functools.partial(step, True))
    pl.when(j > 0)(functools.partial(step, False))


def _in_proj(head, x, gain, w, n, w_r, w_up, w_down, small):
    d = x.shape[1]
    rows = head.shape[0] + x.shape[0]
    tn = IN_PROJ_TN
    n_tiles = n // tn
    nr = w_r.shape[1]
    last_x = x.shape[0] // TM - 1
    steps = rows // TM * n_tiles
    cast_steps = 1 << (min(steps, MAX_CAST_STEPS).bit_length() - 1)
    cast_in, cast_out, cast_shapes = _cast_specs(w_up, w_down, small, n_tiles, cast_steps)
    proj, r, *casts = pl.pallas_call(
        functools.partial(_in_proj_kernel, n_tiles=n_tiles, n_cast=len(cast_in)),
        grid=(rows // TM, n_tiles),
        in_specs=[
            pl.BlockSpec((TM, d), lambda i, j: (0, 0)),
            pl.BlockSpec((TM, d), lambda i, j: (jnp.minimum(i, last_x), 0)),
            pl.BlockSpec((1, d), lambda i, j: (0, 0)),
            pl.BlockSpec(w.shape, lambda i, j: (0, 0), pipeline_mode=pl.Buffered(1)),
            pl.BlockSpec((d, nr), lambda i, j: (0, 0)),
        ] + cast_in,
        out_specs=[pl.BlockSpec((TM, tn), lambda i, j: (i, j)),
                   pl.BlockSpec((TM, nr), lambda i, j: (i, 0))] + cast_out,
        out_shape=[jax.ShapeDtypeStruct((rows, n), _BF16),
                   jax.ShapeDtypeStruct((rows, nr), _BF16)] + cast_shapes,
        scratch_shapes=[pltpu.VMEM((2, TM, d), _BF16)],
        compiler_params=_params("arbitrary", "arbitrary"),
        name="in_proj",
    )(head, x, gain.reshape(1, d), w, w_r, w_up, w_down, *small)
    return proj, r, casts


def _matmul_residual_kernel(a_ref, w_ref, h_ref, o_ref):
    o_ref[...] = h_ref[...] + _dot(a_ref[...], w_ref[...])


def _matmul_residual(a, w, h):
    rows, k = a.shape
    d = w.shape[1]
    return pl.pallas_call(
        _matmul_residual_kernel,
        grid=(rows // TM,),
        in_specs=[
            pl.BlockSpec((TM, k), lambda i: (i, 0)),
            pl.BlockSpec((k, d), lambda i: (0, 0)),
            pl.BlockSpec((TM, d), lambda i: (i, 0)),
        ],
        out_specs=pl.BlockSpec((TM, d), lambda i: (i, 0)),
        out_shape=jax.ShapeDtypeStruct((rows, d), _F32),
        compiler_params=_params("parallel"),
        name="matmul_residual",
    )(a, w, h)


def _mlp_kernel(slab_ref, g_ref, wu_ref, wd_ref, gf_ref, o_ref, xn_ref, res_ref, *, n_tiles, final):
    t = pl.program_id(0)
    j = t % n_tiles
    slot = (t // n_tiles) % 2
    slab = TM // n_tiles
    rows = pl.ds(pl.multiple_of(j * slab, slab), slab)

    def stage():
        x = slab_ref[...]
        res_ref[rows, :] = x
        xn_ref[slot, rows, :] = _normed(x, g_ref)

    def step(first):
        u = jnp.maximum(_dot(xn_ref[1 - slot], wu_ref[...]), 0.0)
        part = _dot((u * u).astype(_BF16), wd_ref[...])
        if first:
            o_ref[...] = res_ref[...] + part
        else:
            o_ref[...] += part
        stage()

    pl.when(t < n_tiles)(stage)
    pl.when((t >= n_tiles) & (j == 0))(functools.partial(step, True))
    pl.when((t >= n_tiles) & (j > 0))(functools.partial(step, False))

    if final:
        @pl.when((t >= n_tiles) & (j == n_tiles - 1))
        def _():
            y = o_ref[...]
            o_ref[...] = y * _rms_scale(y) * gf_ref[...]


def _mlp(h, gain, w_up, w_down, layer, final_gain=None):
    rows, d = h.shape
    _, n_tiles, _, tf = w_up.shape
    assert TM % (n_tiles * BF16_ROWS) == 0
    final = final_gain is not None
    skip = FRONT // TM if final else 0
    out_rows = rows - skip * TM
    slab = TM // n_tiles
    last_slab = rows // slab - 1
    gf = (final_gain if final else gain).reshape(1, d)
    hidden = lambda t: jnp.where(t < n_tiles, 0, t % n_tiles)
    row_tile = lambda t: jnp.maximum(t // n_tiles - 1, 0)
    return pl.pallas_call(
        functools.partial(_mlp_kernel, n_tiles=n_tiles, final=final),
        grid=(out_rows // TM * n_tiles + n_tiles,),
        in_specs=[
            pl.BlockSpec((slab, d), lambda t: (jnp.minimum(t + skip * n_tiles, last_slab), 0)),
            pl.BlockSpec((1, d), lambda t: (0, 0)),
            pl.BlockSpec((None, None, d, tf), lambda t: (layer, hidden(t), 0, 0)),
            pl.BlockSpec((None, tf, d), lambda t: (layer, hidden(t), 0)),
            pl.BlockSpec((1, d), lambda t: (0, 0)),
        ],
        out_specs=pl.BlockSpec((TM, d), lambda t: (row_tile(t), 0)),
        out_shape=jax.ShapeDtypeStruct((out_rows, d), _F32),
        scratch_shapes=[pltpu.VMEM((2, TM, d), _BF16), pltpu.VMEM((TM, d), _F32)],
        compiler_params=_params("arbitrary"),
        name="mlp_final" if final else "mlp",
    )(h, gain.reshape(1, d), w_up, w_down, gf)


def _gla_kernel(q_ref, k_ref, v_ref, r_ref, wg_ref, bg_ref, o_ref, s_ref, b_ref):
    step = pl.program_id(0)
    c = GLA_CHUNK
    nsub = c // GLA_SUB

    @pl.when(step == 0)
    def _():
        s_ref[...] = jnp.zeros_like(s_ref)
        b_ref[0] = jnp.zeros((c, GLA_QK), _F32)

    b_all = b_ref[step % 2]

    causal = [lax.broadcasted_iota(jnp.int32, (GLA_SUB, c), 1)
              <= lax.broadcasted_iota(jnp.int32, (GLA_SUB, c), 0) + i * GLA_SUB for i in range(nsub)]

    def rows_of(t, i):
        return t[i * GLA_SUB:(i + 1) * GLA_SUB]

    def scaled(t, e):
        return t * jnp.exp2(e).astype(_BF16)

    q_shift = math.log2(GLA_DK ** -0.5)

    def attention_weights(h):
        ks = slice(h * GLA_DK, (h + 1) * GLA_DK)
        b = b_all[:, ks]
        beta = [jnp.zeros((1, GLA_DK), _F32)] + [b[i * GLA_SUB - 1:i * GLA_SUB, :] for i in range(1, nsub + 1)]
        q_exp = jnp.concatenate([rows_of(b, i) - beta[i] for i in range(nsub)], axis=0)
        k_exp = jnp.concatenate([beta[i + 1] - rows_of(b, i) for i in range(nsub)], axis=0)
        qs = q_ref[:, ks] * jnp.exp2(q_exp + q_shift).astype(_BF16)
        kd = k_ref[:, ks] * jnp.exp2(k_exp).astype(_BF16)
        a_rows = []
        for i in range(nsub):
            kt = jnp.concatenate([scaled(rows_of(kd, j), beta[i] - beta[j + 1]) for j in range(i + 1)]
                                 + [rows_of(kd, j) for j in range(i + 1, nsub)], axis=0)
            a_rows.append(jnp.where(causal[i], _dot_nt(rows_of(qs, i), kt), 0.0).astype(_BF16))
        qe = jnp.concatenate([scaled(rows_of(qs, i), beta[i]) for i in range(nsub)], axis=0)
        k_dec = jnp.concatenate([scaled(rows_of(kd, j), beta[nsub] - beta[j + 1]) for j in range(nsub)], axis=0)
        decay = jnp.exp2(jnp.broadcast_to(beta[nsub], (LANES, GLA_DK))).T
        return jnp.concatenate([jnp.concatenate(a_rows, axis=0), qe], axis=1), k_dec, decay

    def outputs(h, lhs, k_dec, decay):
        vs = slice(h * GLA_DV, (h + 1) * GLA_DV)
        vh = v_ref[:, vs]
        state = s_ref[h]
        o_ref[:, vs] = _dot(lhs, jnp.concatenate([vh, state.astype(_BF16)], axis=0)).astype(o_ref.dtype)
        s_ref[h] = state * jnp.concatenate([decay] * (GLA_DV // LANES), axis=1) + _dot_tn(k_dec, vh)

    def gate_logits():
        return _dot(r_ref[...], wg_ref[...]) + bg_ref[...]

    def gate_rows(x, i):
        xs = rows_of(x, i)
        gk = (jnp.minimum(xs, 0.0) - jnp.log1p(jnp.exp(-jnp.abs(xs)))) * (LOG2E / GLA_GATE_NORM)
        row = (step + 1) * c + i * GLA_SUB + lax.broadcasted_iota(jnp.int32, (GLA_SUB, 1), 0)
        gk = jnp.where(row >= META_ROW0, gk, 0.0)
        hi = gk.astype(_BF16)
        return hi, (gk - hi.astype(_F32)).astype(_BF16)

    def cumulative(pairs):
        tri = (lax.broadcasted_iota(jnp.int32, (c, c), 0)
               >= lax.broadcasted_iota(jnp.int32, (c, c), 1)).astype(_BF16)
        hi = jnp.concatenate([p[0] for p in pairs], axis=0)
        lo = jnp.concatenate([p[1] for p in pairs], axis=0)
        return _dot(tri, hi) + _dot(tri, lo)

    assert nsub == GLA_HEADS
    x_gate = gate_logits()
    pairs = []
    pending = None
    for h in range(GLA_HEADS):
        ready = attention_weights(h)
        pairs.append(gate_rows(x_gate, h))
        if pending is not None:
            outputs(h - 1, *pending)
        pending = ready
    b_next = cumulative(pairs)
    outputs(GLA_HEADS - 1, *pending)
    b_ref[(step + 1) % 2] = b_next


def _gla(proj, r, w_gate, b_gate):
    rows = proj.shape[0]
    c = GLA_CHUNK
    steps = rows // c
    last = steps - 1
    return pl.pallas_call(
        _gla_kernel,
        grid=(steps,),
        in_specs=[
            pl.BlockSpec((c, GLA_QK), lambda s: (s, 0)),
            pl.BlockSpec((c, GLA_QK), lambda s: (s, 1)),
            pl.BlockSpec((c, GLA_VD), lambda s: (s, 1)),
            pl.BlockSpec((c, RANK_PAD), lambda s: (jnp.minimum(s + 1, last), 0)),
            pl.BlockSpec((RANK_PAD, GLA_QK), lambda s: (0, 0)),
            pl.BlockSpec((1, GLA_QK), lambda s: (0, 0)),
        ],
        out_specs=pl.BlockSpec((c, GLA_VD), lambda s: (s, 0)),
        out_shape=jax.ShapeDtypeStruct((rows, GLA_VD), _BF16),
        scratch_shapes=[pltpu.VMEM((GLA_HEADS, GLA_DK, GLA_DV), _F32),
                        pltpu.VMEM((2, c, GLA_QK), _F32)],
        compiler_params=_params("arbitrary"),
        name="gla",
    )(proj, proj, proj, r, w_gate, b_gate.reshape(1, GLA_QK))


def _gla_out_kernel(o_ref, g_ref, no_ref, w_ref, head_ref, x_ref, out_ref):
    parts = []
    for h in range(GLA_HEADS):
        vs = slice(h * GLA_DV, (h + 1) * GLA_DV)
        o = o_ref[:, vs].astype(_F32)
        hg = 0.5 * g_ref[:, vs].astype(_F32)
        silu = hg * jnp.tanh(hg) + hg
        parts.append((o * _rms_scale(o) * no_ref[...] * silu).astype(_BF16))
    y = jnp.concatenate(parts, axis=1)
    out_ref[...] = _stream_tile(head_ref, x_ref, pl.program_id(0)) + _dot(y, w_ref[...])


def _gla_out(o, proj, norm_out, w, head, x):
    rows, k = o.shape
    d = w.shape[1]
    return pl.pallas_call(
        _gla_out_kernel,
        grid=(rows // TM,),
        in_specs=[
            pl.BlockSpec((TM, k), lambda i: (i, 0)),
            pl.BlockSpec((TM, GLA_VD), lambda i: (i, 2)),
            pl.BlockSpec((1, GLA_DV), lambda i: (0, 0)),
            pl.BlockSpec((k, d), lambda i: (0, 0)),
        ] + _stream_specs(d, 1),
        out_specs=pl.BlockSpec((TM, d), lambda i: (i, 0)),
        out_shape=jax.ShapeDtypeStruct((rows, d), _F32),
        compiler_params=_params("parallel"),
        name="gla_out",
    )(o, proj, norm_out.reshape(1, GLA_DV), w, head, x)


def _swap_halves(x, width):
    lane = lax.broadcasted_iota(jnp.int32, x.shape, 1)
    first = (lane % (2 * width)) < width
    return jnp.where(first, pltpu.roll(x, LANES - width, 1), pltpu.roll(x, width, 1))


def _qkv_kernel(h_ref, gq_ref, gkv_ref, wq_ref, wkv_ref, q_ref, k_ref, v_ref, tab_ref):
    half = SWA_HEAD_DIM // 2
    i = pl.program_id(0)

    def inv_freq(shape):
        lane = lax.broadcasted_iota(jnp.int32, shape, 1)
        return jnp.exp((lane % half).astype(_F32) * (-math.log(ROPE_THETA) / half))

    @pl.when(i == 0)
    def _():
        off = lax.broadcasted_iota(jnp.int32, (TM, LANES), 0).astype(_F32) * inv_freq((TM, LANES))
        tab_ref[0] = jnp.cos(off)
        tab_ref[1] = jnp.sin(off)

    x = h_ref[...]
    xhat = x * _rms_scale(x)
    q = _dot((xhat * gq_ref[...]).astype(_BF16), wq_ref[...])
    kv = _dot((xhat * gkv_ref[...]).astype(_BF16), wkv_ref[...])

    base = (i * TM - META_ROW0).astype(_F32) * inv_freq((8, LANES))[:1]
    cos_b, sin_b = jnp.cos(base), jnp.sin(base)
    cos = cos_b * tab_ref[0] - sin_b * tab_ref[1]
    sin = sin_b * tab_ref[0] + cos_b * tab_ref[1]
    lane = lax.broadcasted_iota(jnp.int32, (TM, LANES), 1)
    sin = jnp.where((lane % SWA_HEAD_DIM) < half, -sin, sin)

    def rope(t):
        return t * cos + _swap_halves(t, half) * sin

    qscale = SWA_HEAD_DIM ** -0.5 * LOG2E
    for p in range(SWA_Q_HEADS // 2):
        sl = slice(p * LANES, (p + 1) * LANES)
        q_ref[:, sl] = (rope(q[:, sl]) * qscale).astype(q_ref.dtype)

    low = (lane % LANES) < SWA_HEAD_DIM
    kv_width = SWA_KV_HEADS * SWA_HEAD_DIM
    for p in range(SWA_KV_HEADS // 2):
        sl = slice(p * LANES, (p + 1) * LANES)
        for t, o_ref in ((rope(kv[:, sl]), k_ref), (kv[:, kv_width + p * LANES:kv_width + (p + 1) * LANES], v_ref)):
            swapped = pltpu.roll(t, SWA_HEAD_DIM, 1)
            o_ref[:, (2 * p) * LANES:(2 * p + 1) * LANES] = jnp.where(low, t, swapped).astype(o_ref.dtype)
            o_ref[:, (2 * p + 1) * LANES:(2 * p + 2) * LANES] = jnp.where(low, swapped, t).astype(o_ref.dtype)


def _qkv(h, gain_q, gain_kv, w_q, w_kv):
    rows, d = h.shape
    nq = w_q.shape[1]
    nkv = w_kv.shape[1]
    dup = SWA_KV_HEADS * LANES
    return pl.pallas_call(
        _qkv_kernel,
        grid=(rows // TM,),
        in_specs=[
            pl.BlockSpec((TM, d), lambda i: (i, 0)),
            pl.BlockSpec((1, d), lambda i: (0, 0)),
            pl.BlockSpec((1, d), lambda i: (0, 0)),
            pl.BlockSpec((d, nq), lambda i: (0, 0)),
            pl.BlockSpec((d, nkv), lambda i: (0, 0)),
        ],
        out_specs=[
            pl.BlockSpec((TM, nq), lambda i: (i, 0)),
            pl.BlockSpec((TM, dup), lambda i: (i, 0)),
            pl.BlockSpec((TM, dup), lambda i: (i, 0)),
        ],
        out_shape=[
            jax.ShapeDtypeStruct((rows, nq), _BF16),
            jax.ShapeDtypeStruct((rows, dup), _BF16),
            jax.ShapeDtypeStruct((rows, dup), _BF16),
        ],
        scratch_shapes=[pltpu.VMEM((2, TM, LANES), _F32)],
        compiler_params=_params("arbitrary"),
        name="qkv_rope",
    )(h, gain_q.reshape(1, d), gain_kv.reshape(1, d), w_q, w_kv)


def _swa_kernel(sink_ref, q_ref, kp_ref, kc_ref, km_ref, vp_ref, vc_ref, vm_ref, o_ref):
    n = pl.program_id(0)
    first_block = META_ROW0 // SWA_BLOCK
    meta_lo = META_ROW0 % SWA_BLOCK

    @pl.when(n < first_block)
    def _():
        o_ref[...] = jnp.zeros_like(o_ref)

    @pl.when(n >= first_block)
    def _():
        kb = jnp.concatenate([km_ref[meta_lo:, :], kp_ref[...], kc_ref[...], km_ref[meta_lo:, :]], axis=0)
        vb = jnp.concatenate([vm_ref[meta_lo:, :], vp_ref[...], vc_ref[...], vm_ref[meta_lo:, :]], axis=0)
        lane = lax.broadcasted_iota(jnp.int32, (SWA_HALF, LANES), 1)
        low = lane < SWA_HEAD_DIM
        high = lane >= SWA_HEAD_DIM

        qi0 = lax.broadcasted_iota(jnp.int32, (SWA_HALF, SWA_KEYS), 0)
        col = lax.broadcasted_iota(jnp.int32, (SWA_HALF, SWA_KEYS), 1)
        band_w = SWA_KEYS - N_META
        caps = []
        for hq in range(2):
            qi = qi0 + hq * SWA_HALF
            if hq == 0:
                is_meta = col < N_META
                kj = col - N_META
                m = col
                start = 0
            else:
                is_meta = col >= band_w
                kj = col + (2 * SWA_BLOCK - band_w)
                m = col - band_w
                start = 2 * N_META + 2 * SWA_BLOCK - SWA_KEYS
            band_ok = (kj > qi) & (kj <= qi + SWA_WINDOW) & ((n - 1) * SWA_BLOCK + kj >= FRONT)
            meta_ok = (META_ROW0 + m) <= n * SWA_BLOCK + qi
            ok = (is_meta & meta_ok) | (jnp.logical_not(is_meta) & band_ok)
            caps.append((jnp.where(ok, jnp.inf, NEG_INF).astype(_F32), start))

        ones = jnp.ones((SWA_KEYS, LANES), _BF16)

        def sink_tile(g):
            return jnp.concatenate(
                [jnp.full((SWA_HALF, LANES), sink_ref[g * SWA_GROUP + j] * LOG2E, _F32)
                 for j in range(SWA_GROUP)], axis=0)

        def scores(g, hq):
            cap, start = caps[hq]
            rows = slice(hq * SWA_HALF, (hq + 1) * SWA_HALF)
            parts = []
            for j in range(SWA_GROUP):
                pair = (g * SWA_GROUP + j) // 2
                t = q_ref[rows, pair * LANES:(pair + 1) * LANES]
                parts.append(jnp.where(low if j % 2 == 0 else high, t, jnp.zeros_like(t)))
            lhs = jnp.concatenate(parts, axis=0)
            s = _dot_nt(lhs, kb[start:start + SWA_KEYS, g * LANES:(g + 1) * LANES])
            s = jnp.concatenate(
                [jnp.minimum(s[j * SWA_HALF:(j + 1) * SWA_HALF], cap) for j in range(SWA_GROUP)], axis=0)
            row_max = jnp.max(s, axis=-1, keepdims=True)
            m = jnp.maximum(jnp.broadcast_to(row_max, (SWA_GROUP * SWA_HALF, LANES)), sink_tile(g))
            return s, m

        def probs(s, m, g):
            p = jnp.concatenate([jnp.exp2(s[:, c * LANES:(c + 1) * LANES] - m)
                                 for c in range(SWA_KEYS // LANES)], axis=1).astype(_BF16)
            return p, jnp.exp2(sink_tile(g) - m)

        def outputs(p, e, g, hq):
            _, start = caps[hq]
            rows = slice(hq * SWA_HALF, (hq + 1) * SWA_HALF)
            vw = jnp.concatenate([vb[start:start + SWA_KEYS, g * LANES:(g + 1) * LANES], ones], axis=1)
            od = _dot(p, vw)
            o = od[:, :LANES] / (od[:, LANES:] + e)
            for j in range(0, SWA_GROUP, 2):
                pair = (g * SWA_GROUP + j) // 2
                even = o[j * SWA_HALF:(j + 1) * SWA_HALF]
                odd = o[(j + 1) * SWA_HALF:(j + 2) * SWA_HALF]
                o_ref[rows, pair * LANES:(pair + 1) * LANES] = jnp.where(low, even, odd).astype(o_ref.dtype)

        tiles = [(g, hq) for g in range(SWA_KV_HEADS) for hq in range(2)]
        stage_a, stage_b = {}, {}
        for t in range(len(tiles) + 2):
            if t < len(tiles):
                stage_a[t] = scores(*tiles[t])
            if 0 <= t - 1 < len(tiles):
                stage_b[t - 1] = probs(*stage_a.pop(t - 1), tiles[t - 1][0])
            if 0 <= t - 2 < len(tiles):
                outputs(*stage_b.pop(t - 2), *tiles[t - 2])


def _swa(q, k_dup, v_dup, sinks):
    rows, nq = q.shape
    steps = rows // SWA_BLOCK
    dup = k_dup.shape[1]
    first_block = META_ROW0 // SWA_BLOCK
    blk = pl.BlockSpec((SWA_BLOCK, dup), lambda n: (n, 0))
    prev = pl.BlockSpec((SWA_BLOCK, dup), lambda n: (jnp.maximum(n - 1, 0), 0))
    meta = pl.BlockSpec((SWA_BLOCK, dup), lambda n: (first_block, 0))
    return pl.pallas_call(
        _swa_kernel,
        grid=(steps,),
        in_specs=[
            pl.BlockSpec(memory_space=pltpu.SMEM),
            pl.BlockSpec((SWA_BLOCK, nq), lambda n: (n, 0)),
            prev, blk, meta, prev, blk, meta,
        ],
        out_specs=pl.BlockSpec((SWA_BLOCK, nq), lambda n: (n, 0)),
        out_shape=jax.ShapeDtypeStruct((rows, nq), _BF16),
        compiler_params=_params("parallel"),
        name="swa",
    )(sinks, q, k_dup, k_dup, k_dup, v_dup, v_dup, v_dup)


def kernel(x, meta_tokens, norm_mix, norm_mlp, w_mlp_up, w_mlp_down, a_w_in, a_w_gate_up, a_b_gate, a_norm_out, a_w_out, kv_norm, w_kv, b_w_q, b_sinks, b_w_out, norm_final):
    batch, seq, d = x.shape
    assert batch == 1 and d == D_MODEL and seq % TM == 0
    assert norm_mix.shape[0] == 2 and a_w_in.shape[0] == 1 and b_w_q.shape[0] == 1

    head = jnp.concatenate([jnp.zeros((META_ROW0, d), x.dtype), meta_tokens.astype(x.dtype)], axis=0)
    x2 = x[0]
    bf = lambda w: w.astype(_BF16)
    n_main = GLA_IN - GLA_RANK
    w_in = bf(a_w_in[0])
    w_r = jnp.pad(w_in[:, n_main:], ((0, 0), (0, RANK_PAD - GLA_RANK)))
    w_gate = jnp.pad(bf(a_w_gate_up[0]), ((0, RANK_PAD - GLA_RANK), (0, 0)))

    proj, r, (w_up, w_down, w_out_a, w_q, w_out_b, w_kv_b) = _in_proj(
        head, x2, norm_mix[0], w_in, n_main, w_r, w_mlp_up, w_mlp_down,
        (a_w_out[0], b_w_q[0], b_w_out[0], w_kv))
    o = _gla(proj, r, w_gate, a_b_gate[0])
    h = _gla_out(o, proj, a_norm_out[0], w_out_a, head, x2)
    h = _mlp(h, norm_mlp[0], w_up, w_down, 0)

    q, k_dup, v_dup = _qkv(h, norm_mix[1], kv_norm, w_q, w_kv_b)
    o = _swa(q, k_dup, v_dup, b_sinks[0])
    h = _matmul_residual(o, w_out_b, h)
    return _mlp(h, norm_mlp[1], w_up, w_down, 1, final_gain=norm_final)[None]
```

```python
import functools
import math

import jax
import jax.numpy as jnp
from jax import lax
from jax.experimental import pallas as pl
from jax.experimental.pallas import tpu as pltpu

D_MODEL = 2048
N_META = 16
D_FF = 4 * D_MODEL
RMS_EPS = 1e-6
NEG_INF = -1e30

GLA_HEADS = 4
GLA_DK = 256
GLA_DV = 512
GLA_QK = GLA_HEADS * GLA_DK
GLA_VD = GLA_HEADS * GLA_DV
GLA_RANK = 16
GLA_GATE_NORM = 16.0
GLA_IN = 2 * GLA_QK + 2 * GLA_VD + GLA_RANK

SWA_HEAD_DIM = 64
SWA_Q_HEADS = 32
SWA_KV_HEADS = 4
SWA_GROUP = SWA_Q_HEADS // SWA_KV_HEADS
SWA_WINDOW = 128
ROPE_THETA = 10000.0

LANES = 128
BF16_ROWS = 16
FRONT = 512
META_ROW0 = FRONT - N_META
TM = 512
IN_PROJ_TN = 1024
MLP_TF = 1024
GLA_CHUNK = 256
GLA_SUB = 64
SWA_BLOCK = 128
SWA_HALF = SWA_BLOCK // 2
SWA_KEYS = 2 * SWA_BLOCK
LOG2E = math.log2(math.e)
RANK_PAD = LANES
MAX_CAST_STEPS = 128
VMEM_LIMIT = 58 * 1024 * 1024

_F32 = jnp.float32
_BF16 = jnp.bfloat16


def _dot(a, b):
    return jnp.dot(a, b, preferred_element_type=_F32)


def _dot_nt(a, b):
    return lax.dot_general(a, b, (((1,), (1,)), ((), ())), preferred_element_type=_F32)


def _dot_tn(a, b):
    return lax.dot_general(a, b, (((0,), (0,)), ((), ())), preferred_element_type=_F32)


def _params(*semantics):
    return pltpu.CompilerParams(dimension_semantics=semantics, vmem_limit_bytes=VMEM_LIMIT)


def _rms_scale(x):
    return lax.rsqrt(jnp.mean(x * x, axis=-1, keepdims=True) + RMS_EPS)


def _normed(x, g_ref):
    return (x * _rms_scale(x) * g_ref[...]).astype(_BF16)


def _norm_next_rows(src_ref, g_ref, xn_ref, slot, j, n_steps):
    slab = -(-TM // (n_steps * BF16_ROWS)) * BF16_ROWS
    start = pl.multiple_of(jnp.minimum(j * slab, TM - slab), BF16_ROWS)
    xn_ref[slot, pl.ds(start, slab), :] = _normed(src_ref[pl.ds(start, slab), :], g_ref)


def _stream_tile(head_ref, x_ref, i):
    return jnp.where(i == 0, head_ref[...], x_ref[...])


def _stream_specs(d, grid_rank):
    if grid_rank == 1:
        return [pl.BlockSpec((TM, d), lambda i: (0, 0)),
                pl.BlockSpec((TM, d), lambda i: (jnp.maximum(i - 1, 0), 0))]
    return [pl.BlockSpec((TM, d), lambda i, j: (0, 0)),
            pl.BlockSpec((TM, d), lambda i, j: (jnp.maximum(i - 1, 0), 0))]


def _cast_slabs(src_refs, dst_refs):
    (wu_src, wd_src, *small_src), (wu, wd, *small) = src_refs, dst_refs
    for t in range(wu.shape[0]):
        wu[t] = wu_src[:, t * MLP_TF:(t + 1) * MLP_TF].astype(_BF16)
    wd[...] = wd_src[...].astype(_BF16)
    for src, dst in zip(small_src, small):
        dst[...] = src[...].astype(_BF16)


def _cast_specs(w_up, w_down, small, n_tiles, cast_steps):
    layers, d, ff = w_up.shape
    per_layer = cast_steps // layers
    ru, rd = d // per_layer, ff // per_layer
    step = lambda i, j: jnp.minimum(i * n_tiles + j, cast_steps - 1)
    layer = lambda i, j: step(i, j) // per_layer
    slab = lambda i, j: step(i, j) % per_layer
    in_specs = [pl.BlockSpec((None, ru, ff), lambda i, j: (layer(i, j), slab(i, j), 0)),
                pl.BlockSpec((None, rd, d), lambda i, j: (layer(i, j), slab(i, j), 0))]
    out_specs = [pl.BlockSpec((None, ff // MLP_TF, ru, MLP_TF), lambda i, j: (layer(i, j), 0, slab(i, j), 0)),
                 pl.BlockSpec((None, rd, d), lambda i, j: (layer(i, j), slab(i, j), 0))]
    out_shapes = [jax.ShapeDtypeStruct((layers, ff // MLP_TF, d, MLP_TF), _BF16),
                  jax.ShapeDtypeStruct((layers, ff, d), _BF16)]
    for w in small:
        r, c = w.shape
        spec = pl.BlockSpec((r // cast_steps, c), lambda i, j: (step(i, j), 0))
        in_specs.append(spec)
        out_specs.append(spec)
        out_shapes.append(jax.ShapeDtypeStruct((r, c), _BF16))
    return in_specs, out_specs, out_shapes


def _in_proj_kernel(*refs, n_tiles, n_cast):
    head_ref, xnext_ref, g_ref, w_ref, wr_ref = refs[:5]
    cast_src = refs[5:5 + n_cast]
    o_ref, r_ref = refs[5 + n_cast:7 + n_cast]
    cast_dst = refs[7 + n_cast:7 + 2 * n_cast]
    xn_ref = refs[-1]
    i, j = pl.program_id(0), pl.program_id(1)
    slot = i % 2

    @pl.when((i == 0) & (j == 0))
    def _():
        xn_ref[0] = _normed(head_ref[...], g_ref)

    def step(first):
        xn = xn_ref[slot]
        cols = pl.ds(pl.multiple_of(j * IN_PROJ_TN, IN_PROJ_TN), IN_PROJ_TN)
        o_ref[...] = _dot(xn, w_ref[:, cols]).astype(o_ref.dtype)
        if first:
            r_ref[...] = _dot(xn, wr_ref[...]).astype(r_ref.dtype)
        _norm_next_rows(xnext_ref, g_ref, xn_ref, 1 - slot, j, n_tiles)
        _cast_slabs(cast_src, cast_dst)

    pl.when(j == 0)(functools.partial(step, True))
    pl.when(j > 0)(functools.partial(step, False))


def _in_proj(head, x, gain, w, n, w_r, w_up, w_down, small):
    d = x.shape[1]
    rows = head.shape[0] + x.shape[0]
    tn = IN_PROJ_TN
    n_tiles = n // tn
    nr = w_r.shape[1]
    last_x = x.shape[0] // TM - 1
    steps = rows // TM * n_tiles
    cast_steps = 1 << (min(steps, MAX_CAST_STEPS).bit_length() - 1)
    cast_in, cast_out, cast_shapes = _cast_specs(w_up, w_down, small, n_tiles, cast_steps)
    proj, r, *casts = pl.pallas_call(
        functools.partial(_in_proj_kernel, n_tiles=n_tiles, n_cast=len(cast_in)),
        grid=(rows // TM, n_tiles),
        in_specs=[
            pl.BlockSpec((TM, d), lambda i, j: (0, 0)),
            pl.BlockSpec((TM, d), lambda i, j: (jnp.minimum(i, last_x), 0)),
            pl.BlockSpec((1, d), lambda i, j: (0, 0)),
            pl.BlockSpec(w.shape, lambda i, j: (0, 0), pipeline_mode=pl.Buffered(1)),
            pl.BlockSpec((d, nr), lambda i, j: (0, 0)),
        ] + cast_in,
        out_specs=[pl.BlockSpec((TM, tn), lambda i, j: (i, j)),
                   pl.BlockSpec((TM, nr), lambda i, j: (i, 0))] + cast_out,
        out_shape=[jax.ShapeDtypeStruct((rows, n), _BF16),
                   jax.ShapeDtypeStruct((rows, nr), _BF16)] + cast_shapes,
        scratch_shapes=[pltpu.VMEM((2, TM, d), _BF16)],
        compiler_params=_params("arbitrary", "arbitrary"),
        name="in_proj",
    )(head, x, gain.reshape(1, d), w, w_r, w_up, w_down, *small)
    return proj, r, casts


def _matmul_residual_kernel(a_ref, w_ref, h_ref, o_ref):
    o_ref[...] = h_ref[...] + _dot(a_ref[...], w_ref[...])


def _matmul_residual(a, w, h):
    rows, k = a.shape
    d = w.shape[1]
    return pl.pallas_call(
        _matmul_residual_kernel,
        grid=(rows // TM,),
        in_specs=[
            pl.BlockSpec((TM, k), lambda i: (i, 0)),
            pl.BlockSpec((k, d), lambda i: (0, 0)),
            pl.BlockSpec((TM, d), lambda i: (i, 0)),
        ],
        out_specs=pl.BlockSpec((TM, d), lambda i: (i, 0)),
        out_shape=jax.ShapeDtypeStruct((rows, d), _F32),
        compiler_params=_params("parallel"),
        name="matmul_residual",
    )(a, w, h)


def _mlp_kernel(hnext_ref, g_ref, wu_ref, wd_ref, gf_ref, o_ref, xn_ref, res_ref, *, n_tiles, final):
    t = pl.program_id(0)
    s = jnp.maximum(t - 1, 0)
    j = s % n_tiles
    slot = (s // n_tiles) % 2

    @pl.when(t == 0)
    def _():
        x = hnext_ref[...]
        xn_ref[0] = _normed(x, g_ref)
        res_ref[...] = x

    def step(first):
        u = jnp.maximum(_dot(xn_ref[slot], wu_ref[...]), 0.0)
        part = _dot((u * u).astype(_BF16), wd_ref[...])
        if first:
            o_ref[...] = res_ref[...] + part
        else:
            o_ref[...] += part
        slab = TM // n_tiles
        rows = pl.ds(pl.multiple_of(j * slab, slab), slab)
        x = hnext_ref[rows, :]
        res_ref[rows, :] = x
        xn_ref[1 - slot, rows, :] = _normed(x, g_ref)

    pl.when((t > 0) & (j == 0))(functools.partial(step, True))
    pl.when((t > 0) & (j > 0))(functools.partial(step, False))

    if final:
        @pl.when((t > 0) & (j == n_tiles - 1))
        def _():
            y = o_ref[...]
            o_ref[...] = y * _rms_scale(y) * gf_ref[...]


def _mlp(h, gain, w_up, w_down, layer, final_gain=None):
    rows, d = h.shape
    _, n_tiles, _, tf = w_up.shape
    assert TM % (n_tiles * BF16_ROWS) == 0
    final = final_gain is not None
    skip = FRONT // TM if final else 0
    out_rows = rows - skip * TM
    last = rows // TM - 1
    gf = (final_gain if final else gain).reshape(1, d)
    hidden = lambda t: jnp.maximum(t - 1, 0) % n_tiles
    row_tile = lambda t: jnp.maximum(t - 1, 0) // n_tiles
    staged = lambda t: jnp.minimum((t + n_tiles - 1) // n_tiles + skip, last)
    return pl.pallas_call(
        functools.partial(_mlp_kernel, n_tiles=n_tiles, final=final),
        grid=(out_rows // TM * n_tiles + 1,),
        in_specs=[
            pl.BlockSpec((TM, d), lambda t: (staged(t), 0)),
            pl.BlockSpec((1, d), lambda t: (0, 0)),
            pl.BlockSpec((None, None, d, tf), lambda t: (layer, hidden(t), 0, 0)),
            pl.BlockSpec((None, tf, d), lambda t: (layer, hidden(t), 0)),
            pl.BlockSpec((1, d), lambda t: (0, 0)),
        ],
        out_specs=pl.BlockSpec((TM, d), lambda t: (row_tile(t), 0)),
        out_shape=jax.ShapeDtypeStruct((out_rows, d), _F32),
        scratch_shapes=[pltpu.VMEM((2, TM, d), _BF16), pltpu.VMEM((TM, d), _F32)],
        compiler_params=_params("arbitrary"),
        name="mlp_final" if final else "mlp",
    )(h, gain.reshape(1, d), w_up, w_down, gf)


def _gla_kernel(q_ref, k_ref, v_ref, r_ref, wg_ref, bg_ref, o_ref, s_ref, b_ref):
    step = pl.program_id(0)
    c = GLA_CHUNK
    nsub = c // GLA_SUB

    @pl.when(step == 0)
    def _():
        s_ref[...] = jnp.zeros_like(s_ref)
        b_ref[0] = jnp.zeros((c, GLA_QK), _F32)

    b_all = b_ref[step % 2]

    causal = [lax.broadcasted_iota(jnp.int32, (GLA_SUB, c), 1)
              <= lax.broadcasted_iota(jnp.int32, (GLA_SUB, c), 0) + i * GLA_SUB for i in range(nsub)]

    def rows_of(t, i):
        return t[i * GLA_SUB:(i + 1) * GLA_SUB]

    def scaled(t, e):
        return t * jnp.exp2(e).astype(_BF16)

    q_shift = math.log2(GLA_DK ** -0.5)

    def attention_weights(h):
        ks = slice(h * GLA_DK, (h + 1) * GLA_DK)
        b = b_all[:, ks]
        beta = [jnp.zeros((1, GLA_DK), _F32)] + [b[i * GLA_SUB - 1:i * GLA_SUB, :] for i in range(1, nsub + 1)]
        q_exp = jnp.concatenate([rows_of(b, i) - beta[i] for i in range(nsub)], axis=0)
        k_exp = jnp.concatenate([beta[i + 1] - rows_of(b, i) for i in range(nsub)], axis=0)
        qs = q_ref[:, ks] * jnp.exp2(q_exp + q_shift).astype(_BF16)
        kd = k_ref[:, ks] * jnp.exp2(k_exp).astype(_BF16)
        a_rows = []
        for i in range(nsub):
            kt = jnp.concatenate([scaled(rows_of(kd, j), beta[i] - beta[j + 1]) for j in range(i + 1)]
                                 + [rows_of(kd, j) for j in range(i + 1, nsub)], axis=0)
            a_rows.append(jnp.where(causal[i], _dot_nt(rows_of(qs, i), kt), 0.0).astype(_BF16))
        qe = jnp.concatenate([scaled(rows_of(qs, i), beta[i]) for i in range(nsub)], axis=0)
        k_dec = jnp.concatenate([scaled(rows_of(kd, j), beta[nsub] - beta[j + 1]) for j in range(nsub)], axis=0)
        decay = jnp.exp2(jnp.broadcast_to(beta[nsub], (LANES, GLA_DK))).T
        return jnp.concatenate([jnp.concatenate(a_rows, axis=0), qe], axis=1), k_dec, decay

    def outputs(h, lhs, k_dec, decay):
        vs = slice(h * GLA_DV, (h + 1) * GLA_DV)
        vh = v_ref[:, vs]
        state = s_ref[h]
        o_ref[:, vs] = _dot(lhs, jnp.concatenate([vh, state.astype(_BF16)], axis=0)).astype(o_ref.dtype)
        s_ref[h] = state * jnp.concatenate([decay] * (GLA_DV // LANES), axis=1) + _dot_tn(k_dec, vh)

    def gate_logits():
        return _dot(r_ref[...], wg_ref[...]) + bg_ref[...]

    def gate_rows(x, i):
        xs = rows_of(x, i)
        gk = (jnp.minimum(xs, 0.0) - jnp.log1p(jnp.exp(-jnp.abs(xs)))) * (LOG2E / GLA_GATE_NORM)
        row = (step + 1) * c + i * GLA_SUB + lax.broadcasted_iota(jnp.int32, (GLA_SUB, 1), 0)
        gk = jnp.where(row >= META_ROW0, gk, 0.0)
        hi = gk.astype(_BF16)
        return hi, (gk - hi.astype(_F32)).astype(_BF16)

    def cumulative(pairs):
        tri = (lax.broadcasted_iota(jnp.int32, (c, c), 0)
               >= lax.broadcasted_iota(jnp.int32, (c, c), 1)).astype(_BF16)
        hi = jnp.concatenate([p[0] for p in pairs], axis=0)
        lo = jnp.concatenate([p[1] for p in pairs], axis=0)
        return _dot(tri, hi) + _dot(tri, lo)

    assert nsub == GLA_HEADS
    x_gate = gate_logits()
    pairs = []
    pending = None
    for h in range(GLA_HEADS):
        ready = attention_weights(h)
        pairs.append(gate_rows(x_gate, h))
        if pending is not None:
            outputs(h - 1, *pending)
        pending = ready
    b_next = cumulative(pairs)
    outputs(GLA_HEADS - 1, *pending)
    b_ref[(step + 1) % 2] = b_next


def _gla(proj, r, w_gate, b_gate):
    rows = proj.shape[0]
    c = GLA_CHUNK
    steps = rows // c
    last = steps - 1
    return pl.pallas_call(
        _gla_kernel,
        grid=(steps,),
        in_specs=[
            pl.BlockSpec((c, GLA_QK), lambda s: (s, 0)),
            pl.BlockSpec((c, GLA_QK), lambda s: (s, 1)),
            pl.BlockSpec((c, GLA_VD), lambda s: (s, 1)),
            pl.BlockSpec((c, RANK_PAD), lambda s: (jnp.minimum(s + 1, last), 0)),
            pl.BlockSpec((RANK_PAD, GLA_QK), lambda s: (0, 0)),
            pl.BlockSpec((1, GLA_QK), lambda s: (0, 0)),
        ],
        out_specs=pl.BlockSpec((c, GLA_VD), lambda s: (s, 0)),
        out_shape=jax.ShapeDtypeStruct((rows, GLA_VD), _BF16),
        scratch_shapes=[pltpu.VMEM((GLA_HEADS, GLA_DK, GLA_DV), _F32),
                        pltpu.VMEM((2, c, GLA_QK), _F32)],
        compiler_params=_params("arbitrary"),
        name="gla",
    )(proj, proj, proj, r, w_gate, b_gate.reshape(1, GLA_QK))


def _gla_out_kernel(o_ref, g_ref, no_ref, w_ref, head_ref, x_ref, out_ref):
    parts = []
    for h in range(GLA_HEADS):
        vs = slice(h * GLA_DV, (h + 1) * GLA_DV)
        o = o_ref[:, vs].astype(_F32)
        hg = 0.5 * g_ref[:, vs].astype(_F32)
        silu = hg * jnp.tanh(hg) + hg
        parts.append((o * _rms_scale(o) * no_ref[...] * silu).astype(_BF16))
    y = jnp.concatenate(parts, axis=1)
    out_ref[...] = _stream_tile(head_ref, x_ref, pl.program_id(0)) + _dot(y, w_ref[...])


def _gla_out(o, proj, norm_out, w, head, x):
    rows, k = o.shape
    d = w.shape[1]
    return pl.pallas_call(
        _gla_out_kernel,
        grid=(rows // TM,),
        in_specs=[
            pl.BlockSpec((TM, k), lambda i: (i, 0)),
            pl.BlockSpec((TM, GLA_VD), lambda i: (i, 2)),
            pl.BlockSpec((1, GLA_DV), lambda i: (0, 0)),
            pl.BlockSpec((k, d), lambda i: (0, 0)),
        ] + _stream_specs(d, 1),
        out_specs=pl.BlockSpec((TM, d), lambda i: (i, 0)),
        out_shape=jax.ShapeDtypeStruct((rows, d), _F32),
        compiler_params=_params("parallel"),
        name="gla_out",
    )(o, proj, norm_out.reshape(1, GLA_DV), w, head, x)


def _swap_halves(x, width):
    lane = lax.broadcasted_iota(jnp.int32, x.shape, 1)
    first = (lane % (2 * width)) < width
    return jnp.where(first, pltpu.roll(x, LANES - width, 1), pltpu.roll(x, width, 1))


def _qkv_kernel(h_ref, gq_ref, gkv_ref, wq_ref, wkv_ref, q_ref, k_ref, v_ref, tab_ref):
    half = SWA_HEAD_DIM // 2
    i = pl.program_id(0)

    def inv_freq(shape):
        lane = lax.broadcasted_iota(jnp.int32, shape, 1)
        return jnp.exp((lane % half).astype(_F32) * (-math.log(ROPE_THETA) / half))

    @pl.when(i == 0)
    def _():
        off = lax.broadcasted_iota(jnp.int32, (TM, LANES), 0).astype(_F32) * inv_freq((TM, LANES))
        tab_ref[0] = jnp.cos(off)
        tab_ref[1] = jnp.sin(off)

    x = h_ref[...]
    xhat = x * _rms_scale(x)
    q = _dot((xhat * gq_ref[...]).astype(_BF16), wq_ref[...])
    kv = _dot((xhat * gkv_ref[...]).astype(_BF16), wkv_ref[...])

    base = (i * TM - META_ROW0).astype(_F32) * inv_freq((8, LANES))[:1]
    cos_b, sin_b = jnp.cos(base), jnp.sin(base)
    cos = cos_b * tab_ref[0] - sin_b * tab_ref[1]
    sin = sin_b * tab_ref[0] + cos_b * tab_ref[1]
    lane = lax.broadcasted_iota(jnp.int32, (TM, LANES), 1)
    sin = jnp.where((lane % SWA_HEAD_DIM) < half, -sin, sin)

    def rope(t):
        return t * cos + _swap_halves(t, half) * sin

    qscale = SWA_HEAD_DIM ** -0.5 * LOG2E
    for p in range(SWA_Q_HEADS // 2):
        sl = slice(p * LANES, (p + 1) * LANES)
        q_ref[:, sl] = (rope(q[:, sl]) * qscale).astype(q_ref.dtype)

    low = (lane % LANES) < SWA_HEAD_DIM
    kv_width = SWA_KV_HEADS * SWA_HEAD_DIM
    for p in range(SWA_KV_HEADS // 2):
        sl = slice(p * LANES, (p + 1) * LANES)
        for t, o_ref in ((rope(kv[:, sl]), k_ref), (kv[:, kv_width + p * LANES:kv_width + (p + 1) * LANES], v_ref)):
            swapped = pltpu.roll(t, SWA_HEAD_DIM, 1)
            o_ref[:, (2 * p) * LANES:(2 * p + 1) * LANES] = jnp.where(low, t, swapped).astype(o_ref.dtype)
            o_ref[:, (2 * p + 1) * LANES:(2 * p + 2) * LANES] = jnp.where(low, swapped, t).astype(o_ref.dtype)


def _qkv(h, gain_q, gain_kv, w_q, w_kv):
    rows, d = h.shape
    nq = w_q.shape[1]
    nkv = w_kv.shape[1]
    dup = SWA_KV_HEADS * LANES
    return pl.pallas_call(
        _qkv_kernel,
        grid=(rows // TM,),
        in_specs=[
            pl.BlockSpec((TM, d), lambda i: (i, 0)),
            pl.BlockSpec((1, d), lambda i: (0, 0)),
            pl.BlockSpec((1, d), lambda i: (0, 0)),
            pl.BlockSpec((d, nq), lambda i: (0, 0)),
            pl.BlockSpec((d, nkv), lambda i: (0, 0)),
        ],
        out_specs=[
            pl.BlockSpec((TM, nq), lambda i: (i, 0)),
            pl.BlockSpec((TM, dup), lambda i: (i, 0)),
            pl.BlockSpec((TM, dup), lambda i: (i, 0)),
        ],
        out_shape=[
            jax.ShapeDtypeStruct((rows, nq), _BF16),
            jax.ShapeDtypeStruct((rows, dup), _BF16),
            jax.ShapeDtypeStruct((rows, dup), _BF16),
        ],
        scratch_shapes=[pltpu.VMEM((2, TM, LANES), _F32)],
        compiler_params=_params("arbitrary"),
        name="qkv_rope",
    )(h, gain_q.reshape(1, d), gain_kv.reshape(1, d), w_q, w_kv)


def _swa_kernel(sink_ref, q_ref, kp_ref, kc_ref, km_ref, vp_ref, vc_ref, vm_ref, o_ref, bias_ref):
    n = pl.program_id(0)
    first_block = META_ROW0 // SWA_BLOCK
    meta_lo = META_ROW0 % SWA_BLOCK
    band_w = SWA_KEYS - N_META
    win_start = (0, 2 * N_META + 2 * SWA_BLOCK - SWA_KEYS)
    spare_col = (SWA_KEYS - 1, 0)

    @pl.when((n >= first_block) & (n <= first_block + 2))
    def _():
        qi0 = lax.broadcasted_iota(jnp.int32, (SWA_HALF, SWA_KEYS), 0)
        col = lax.broadcasted_iota(jnp.int32, (SWA_HALF, SWA_KEYS), 1)
        for hq in range(2):
            qi = qi0 + hq * SWA_HALF
            if hq == 0:
                is_meta = col < N_META
                kj = col - N_META
                m = col
            else:
                is_meta = col >= band_w
                kj = col + (2 * SWA_BLOCK - band_w)
                m = col - band_w
            band_ok = (kj > qi) & (kj <= qi + SWA_WINDOW) & ((n - 1) * SWA_BLOCK + kj >= FRONT)
            meta_ok = (META_ROW0 + m) <= n * SWA_BLOCK + qi
            ok = (is_meta & meta_ok) | (jnp.logical_not(is_meta) & band_ok)
            mask_bias = jnp.where(ok, 0.0, NEG_INF).astype(_F32)
            for g in range(SWA_KV_HEADS):
                bias_ref[2 * g + hq] = jnp.concatenate(
                    [jnp.where(col == spare_col[hq], sink_ref[g * SWA_GROUP + j] * LOG2E, mask_bias)
                     for j in range(SWA_GROUP)], axis=0)

    @pl.when(n < first_block)
    def _():
        o_ref[...] = jnp.zeros_like(o_ref)

    @pl.when(n >= first_block)
    def _():
        kb = jnp.concatenate([km_ref[meta_lo:, :], kp_ref[...], kc_ref[...], km_ref[meta_lo:, :]], axis=0)
        vb = jnp.concatenate([vm_ref[meta_lo:, :], vp_ref[...], vc_ref[...], vm_ref[meta_lo:, :]], axis=0)
        key_row = lax.broadcasted_iota(jnp.int32, kb.shape, 0)
        spare_row = [key_row == win_start[hq] + spare_col[hq] for hq in range(2)]
        kbs = [jnp.where(spare_row[hq], jnp.zeros_like(kb), kb) for hq in range(2)]
        vbs = [jnp.where(spare_row[hq], jnp.zeros_like(vb), vb) for hq in range(2)]
        lane = lax.broadcasted_iota(jnp.int32, (SWA_HALF, LANES), 1)
        low = lane < SWA_HEAD_DIM
        high = lane >= SWA_HEAD_DIM
        ones = jnp.ones((SWA_KEYS, LANES), _BF16)

        def scores(g, hq):
            rows = slice(hq * SWA_HALF, (hq + 1) * SWA_HALF)
            parts = []
            for j in range(SWA_GROUP):
                pair = (g * SWA_GROUP + j) // 2
                t = q_ref[rows, pair * LANES:(pair + 1) * LANES]
                parts.append(jnp.where(low if j % 2 == 0 else high, t, jnp.zeros_like(t)))
            lhs = jnp.concatenate(parts, axis=0)
            start = win_start[hq]
            s = _dot_nt(lhs, kbs[hq][start:start + SWA_KEYS, g * LANES:(g + 1) * LANES]) + bias_ref[2 * g + hq]
            row_max = jnp.max(s, axis=-1, keepdims=True)
            return s, jnp.broadcast_to(row_max, (SWA_GROUP * SWA_HALF, LANES))

        def probs(s, m):
            return jnp.concatenate([jnp.exp2(s[:, c * LANES:(c + 1) * LANES] - m)
                                    for c in range(SWA_KEYS // LANES)], axis=1).astype(_BF16)

        def outputs(p, g, hq):
            start = win_start[hq]
            rows = slice(hq * SWA_HALF, (hq + 1) * SWA_HALF)
            vw = jnp.concatenate([vbs[hq][start:start + SWA_KEYS, g * LANES:(g + 1) * LANES], ones], axis=1)
            od = _dot(p, vw)
            o = od[:, :LANES] / od[:, LANES:]
            for j in range(0, SWA_GROUP, 2):
                pair = (g * SWA_GROUP + j) // 2
                even = o[j * SWA_HALF:(j + 1) * SWA_HALF]
                odd = o[(j + 1) * SWA_HALF:(j + 2) * SWA_HALF]
                o_ref[rows, pair * LANES:(pair + 1) * LANES] = jnp.where(low, even, odd).astype(o_ref.dtype)

        tiles = [(g, hq) for g in range(SWA_KV_HEADS) for hq in range(2)]
        stage_a, stage_b = {}, {}
        for t in range(len(tiles) + 2):
            if t < len(tiles):
                stage_a[t] = scores(*tiles[t])
            if 0 <= t - 1 < len(tiles):
                stage_b[t - 1] = probs(*stage_a.pop(t - 1))
            if 0 <= t - 2 < len(tiles):
                outputs(stage_b.pop(t - 2), *tiles[t - 2])


def _swa(q, k_dup, v_dup, sinks):
    rows, nq = q.shape
    steps = rows // SWA_BLOCK
    dup = k_dup.shape[1]
    first_block = META_ROW0 // SWA_BLOCK
    blk = pl.BlockSpec((SWA_BLOCK, dup), lambda n: (n, 0))
    prev = pl.BlockSpec((SWA_BLOCK, dup), lambda n: (jnp.maximum(n - 1, 0), 0))
    meta = pl.BlockSpec((SWA_BLOCK, dup), lambda n: (first_block, 0))
    return pl.pallas_call(
        _swa_kernel,
        grid=(steps,),
        in_specs=[
            pl.BlockSpec(memory_space=pltpu.SMEM),
            pl.BlockSpec((SWA_BLOCK, nq), lambda n: (n, 0)),
            prev, blk, meta, prev, blk, meta,
        ],
        out_specs=pl.BlockSpec((SWA_BLOCK, nq), lambda n: (n, 0)),
        out_shape=jax.ShapeDtypeStruct((rows, nq), _BF16),
        scratch_shapes=[pltpu.VMEM((2 * SWA_KV_HEADS, SWA_GROUP * SWA_HALF, SWA_KEYS), _F32)],
        compiler_params=_params("arbitrary"),
        name="swa",
    )(sinks, q, k_dup, k_dup, k_dup, v_dup, v_dup, v_dup)


def kernel(x, meta_tokens, norm_mix, norm_mlp, w_mlp_up, w_mlp_down, a_w_in, a_w_gate_up, a_b_gate, a_norm_out, a_w_out, kv_norm, w_kv, b_w_q, b_sinks, b_w_out, norm_final):
    batch, seq, d = x.shape
    assert batch == 1 and d == D_MODEL and seq % TM == 0
    assert norm_mix.shape[0] == 2 and a_w_in.shape[0] == 1 and b_w_q.shape[0] == 1

    head = jnp.concatenate([jnp.zeros((META_ROW0, d), x.dtype), meta_tokens.astype(x.dtype)], axis=0)
    x2 = x[0]
    bf = lambda w: w.astype(_BF16)
    n_main = GLA_IN - GLA_RANK
    w_in = bf(a_w_in[0])
    w_r = jnp.pad(w_in[:, n_main:], ((0, 0), (0, RANK_PAD - GLA_RANK)))
    w_gate = jnp.pad(bf(a_w_gate_up[0]), ((0, RANK_PAD - GLA_RANK), (0, 0)))

    proj, r, (w_up, w_down, w_out_a, w_q, w_out_b, w_kv_b) = _in_proj(
        head, x2, norm_mix[0], w_in, n_main, w_r, w_mlp_up, w_mlp_down,
        (a_w_out[0], b_w_q[0], b_w_out[0], w_kv))
    o = _gla(proj, r, w_gate, a_b_gate[0])
    h = _gla_out(o, proj, a_norm_out[0], w_out_a, head, x2)
    h = _mlp(h, norm_mlp[0], w_up, w_down, 0)

    q, k_dup, v_dup = _qkv(h, norm_mix[1], kv_norm, w_q, w_kv_b)
    o = _swa(q, k_dup, v_dup, b_sinks[0])
    h = _matmul_residual(o, w_out_b, h)
    return _mlp(h, norm_mlp[1], w_up, w_down, 1, final_gain=norm_final)[None]
```

```python
import functools
import math

import jax
import jax.numpy as jnp
from jax import lax
from jax.experimental import pallas as pl
from jax.experimental.pallas import tpu as pltpu

D_MODEL = 2048
N_META = 16
D_FF = 4 * D_MODEL
RMS_EPS = 1e-6
NEG_INF = -1e30

GLA_HEADS = 4
GLA_DK = 256
GLA_DV = 512
GLA_QK = GLA_HEADS * GLA_DK
GLA_VD = GLA_HEADS * GLA_DV
GLA_RANK = 16
GLA_GATE_NORM = 16.0
GLA_IN = 2 * GLA_QK + 2 * GLA_VD + GLA_RANK

SWA_HEAD_DIM = 64
SWA_Q_HEADS = 32
SWA_KV_HEADS = 4
SWA_GROUP = SWA_Q_HEADS // SWA_KV_HEADS
SWA_WINDOW = 128
ROPE_THETA = 10000.0

LANES = 128
BF16_ROWS = 16
FRONT = 512
META_ROW0 = FRONT - N_META
TM = 512
IN_PROJ_TN = 1024
MLP_TF = 1024
GLA_CHUNK = 256
GLA_SUB = 64
SWA_BLOCK = 128
SWA_HALF = SWA_BLOCK // 2
SWA_KEYS = 2 * SWA_BLOCK
LOG2E = math.log2(math.e)
RANK_PAD = LANES
MAX_CAST_STEPS = 128
VMEM_LIMIT = 58 * 1024 * 1024

_F32 = jnp.float32
_BF16 = jnp.bfloat16


def _dot(a, b):
    return jnp.dot(a, b, preferred_element_type=_F32)


def _dot_nt(a, b):
    return lax.dot_general(a, b, (((1,), (1,)), ((), ())), preferred_element_type=_F32)


def _dot_tn(a, b):
    return lax.dot_general(a, b, (((0,), (0,)), ((), ())), preferred_element_type=_F32)


def _params(*semantics):
    return pltpu.CompilerParams(dimension_semantics=semantics, vmem_limit_bytes=VMEM_LIMIT)


def _rms_scale(x):
    return lax.rsqrt(jnp.mean(x * x, axis=-1, keepdims=True) + RMS_EPS)


def _normed(x, g_ref):
    return (x * _rms_scale(x) * g_ref[...]).astype(_BF16)


def _norm_next_rows(src_ref, g_ref, xn_ref, slot, j, n_steps):
    slab = -(-TM // (n_steps * BF16_ROWS)) * BF16_ROWS
    start = pl.multiple_of(jnp.minimum(j * slab, TM - slab), BF16_ROWS)
    xn_ref[slot, pl.ds(start, slab), :] = _normed(src_ref[pl.ds(start, slab), :], g_ref)


def _stream_tile(head_ref, x_ref, i):
    return jnp.where(i == 0, head_ref[...], x_ref[...])


def _stream_specs(d, grid_rank):
    if grid_rank == 1:
        return [pl.BlockSpec((TM, d), lambda i: (0, 0)),
                pl.BlockSpec((TM, d), lambda i: (jnp.maximum(i - 1, 0), 0))]
    return [pl.BlockSpec((TM, d), lambda i, j: (0, 0)),
            pl.BlockSpec((TM, d), lambda i, j: (jnp.maximum(i - 1, 0), 0))]


def _cast_slabs(src_refs, dst_refs):
    (wu_src, wd_src, *small_src), (wu, wd, *small) = src_refs, dst_refs
    for t in range(wu.shape[0]):
        wu[t] = wu_src[:, t * MLP_TF:(t + 1) * MLP_TF].astype(_BF16)
    wd[...] = wd_src[...].astype(_BF16)
    for src, dst in zip(small_src, small):
        dst[...] = src[...].astype(_BF16)


def _cast_specs(w_up, w_down, small, n_tiles, cast_steps):
    layers, d, ff = w_up.shape
    per_layer = cast_steps // layers
    ru, rd = d // per_layer, ff // per_layer
    step = lambda i, j: jnp.minimum(i * n_tiles + j, cast_steps - 1)
    layer = lambda i, j: step(i, j) // per_layer
    slab = lambda i, j: step(i, j) % per_layer
    in_specs = [pl.BlockSpec((None, ru, ff), lambda i, j: (layer(i, j), slab(i, j), 0)),
                pl.BlockSpec((None, rd, d), lambda i, j: (layer(i, j), slab(i, j), 0))]
    out_specs = [pl.BlockSpec((None, ff // MLP_TF, ru, MLP_TF), lambda i, j: (layer(i, j), 0, slab(i, j), 0)),
                 pl.BlockSpec((None, rd, d), lambda i, j: (layer(i, j), slab(i, j), 0))]
    out_shapes = [jax.ShapeDtypeStruct((layers, ff // MLP_TF, d, MLP_TF), _BF16),
                  jax.ShapeDtypeStruct((layers, ff, d), _BF16)]
    for w in small:
        r, c = w.shape
        spec = pl.BlockSpec((r // cast_steps, c), lambda i, j: (step(i, j), 0))
        in_specs.append(spec)
        out_specs.append(spec)
        out_shapes.append(jax.ShapeDtypeStruct((r, c), _BF16))
    return in_specs, out_specs, out_shapes


def _in_proj_kernel(*refs, n_tiles, n_cast):
    head_ref, xnext_ref, g_ref, w_ref, wr_ref = refs[:5]
    cast_src = refs[5:5 + n_cast]
    o_ref, r_ref = refs[5 + n_cast:7 + n_cast]
    cast_dst = refs[7 + n_cast:7 + 2 * n_cast]
    xn_ref = refs[-1]
    i, j = pl.program_id(0), pl.program_id(1)
    slot = i % 2

    @pl.when((i == 0) & (j == 0))
    def _():
        xn_ref[0] = _normed(head_ref[...], g_ref)

    def step(first):
        xn = xn_ref[slot]
        cols = pl.ds(pl.multiple_of(j * IN_PROJ_TN, IN_PROJ_TN), IN_PROJ_TN)
        o_ref[...] = _dot(xn, w_ref[:, cols]).astype(o_ref.dtype)
        if first:
            r_ref[...] = _dot(xn, wr_ref[...]).astype(r_ref.dtype)
        _norm_next_rows(xnext_ref, g_ref, xn_ref, 1 - slot, j, n_tiles)
        _cast_slabs(cast_src, cast_dst)

    pl.when(j == 0)(functools.partial(step, True))
    pl.when(j > 0)(functools.partial(step, False))


def _in_proj(head, x, gain, w, n, w_r, w_up, w_down, small):
    d = x.shape[1]
    rows = head.shape[0] + x.shape[0]
    tn = IN_PROJ_TN
    n_tiles = n // tn
    nr = w_r.shape[1]
    last_x = x.shape[0] // TM - 1
    steps = rows // TM * n_tiles
    cast_steps = 1 << (min(steps, MAX_CAST_STEPS).bit_length() - 1)
    cast_in, cast_out, cast_shapes = _cast_specs(w_up, w_down, small, n_tiles, cast_steps)
    proj, r, *casts = pl.pallas_call(
        functools.partial(_in_proj_kernel, n_tiles=n_tiles, n_cast=len(cast_in)),
        grid=(rows // TM, n_tiles),
        in_specs=[
            pl.BlockSpec((TM, d), lambda i, j: (0, 0)),
            pl.BlockSpec((TM, d), lambda i, j: (jnp.minimum(i, last_x), 0)),
            pl.BlockSpec((1, d), lambda i, j: (0, 0)),
            pl.BlockSpec(w.shape, lambda i, j: (0, 0), pipeline_mode=pl.Buffered(1)),
            pl.BlockSpec((d, nr), lambda i, j: (0, 0)),
        ] + cast_in,
        out_specs=[pl.BlockSpec((TM, tn), lambda i, j: (i, j)),
                   pl.BlockSpec((TM, nr), lambda i, j: (i, 0))] + cast_out,
        out_shape=[jax.ShapeDtypeStruct((rows, n), _BF16),
                   jax.ShapeDtypeStruct((rows, nr), _BF16)] + cast_shapes,
        scratch_shapes=[pltpu.VMEM((2, TM, d), _BF16)],
        compiler_params=_params("arbitrary", "arbitrary"),
        name="in_proj",
    )(head, x, gain.reshape(1, d), w, w_r, w_up, w_down, *small)
    return proj, r, casts


def _matmul_residual_kernel(a_ref, w_ref, h_ref, o_ref):
    o_ref[...] = h_ref[...] + _dot(a_ref[...], w_ref[...])


def _matmul_residual(a, w, h):
    rows, k = a.shape
    d = w.shape[1]
    return pl.pallas_call(
        _matmul_residual_kernel,
        grid=(rows // TM,),
        in_specs=[
            pl.BlockSpec((TM, k), lambda i: (i, 0)),
            pl.BlockSpec((k, d), lambda i: (0, 0)),
            pl.BlockSpec((TM, d), lambda i: (i, 0)),
        ],
        out_specs=pl.BlockSpec((TM, d), lambda i: (i, 0)),
        out_shape=jax.ShapeDtypeStruct((rows, d), _F32),
        compiler_params=_params("parallel"),
        name="matmul_residual",
    )(a, w, h)


def _mlp_kernel(hnext_ref, g_ref, wu_ref, wd_ref, gf_ref, o_ref, xn_ref, res_ref, *, n_tiles, final):
    t = pl.program_id(0)
    s = jnp.maximum(t - 1, 0)
    j = s % n_tiles
    slot = (s // n_tiles) % 2

    @pl.when(t == 0)
    def _():
        x = hnext_ref[...]
        xn_ref[0] = _normed(x, g_ref)
        res_ref[...] = x

    def step(first):
        u = jnp.maximum(_dot(xn_ref[slot], wu_ref[...]), 0.0)
        part = _dot((u * u).astype(_BF16), wd_ref[...])
        if first:
            o_ref[...] = res_ref[...] + part
        else:
            o_ref[...] += part
        slab = TM // n_tiles
        rows = pl.ds(pl.multiple_of(j * slab, slab), slab)
        x = hnext_ref[rows, :]
        res_ref[rows, :] = x
        xn_ref[1 - slot, rows, :] = _normed(x, g_ref)

    pl.when((t > 0) & (j == 0))(functools.partial(step, True))
    pl.when((t > 0) & (j > 0))(functools.partial(step, False))

    if final:
        @pl.when((t > 0) & (j == n_tiles - 1))
        def _():
            y = o_ref[...]
            o_ref[...] = y * _rms_scale(y) * gf_ref[...]


def _mlp(h, gain, w_up, w_down, layer, final_gain=None):
    rows, d = h.shape
    _, n_tiles, _, tf = w_up.shape
    assert TM % (n_tiles * BF16_ROWS) == 0
    final = final_gain is not None
    skip = FRONT // TM if final else 0
    out_rows = rows - skip * TM
    last = rows // TM - 1
    gf = (final_gain if final else gain).reshape(1, d)
    hidden = lambda t: jnp.maximum(t - 1, 0) % n_tiles
    row_tile = lambda t: jnp.maximum(t - 1, 0) // n_tiles
    staged = lambda t: jnp.minimum((t + n_tiles - 1) // n_tiles + skip, last)
    return pl.pallas_call(
        functools.partial(_mlp_kernel, n_tiles=n_tiles, final=final),
        grid=(out_rows // TM * n_tiles + 1,),
        in_specs=[
            pl.BlockSpec((TM, d), lambda t: (staged(t), 0)),
            pl.BlockSpec((1, d), lambda t: (0, 0)),
            pl.BlockSpec((None, None, d, tf), lambda t: (layer, hidden(t), 0, 0)),
            pl.BlockSpec((None, tf, d), lambda t: (layer, hidden(t), 0)),
            pl.BlockSpec((1, d), lambda t: (0, 0)),
        ],
        out_specs=pl.BlockSpec((TM, d), lambda t: (row_tile(t), 0)),
        out_shape=jax.ShapeDtypeStruct((out_rows, d), _F32),
        scratch_shapes=[pltpu.VMEM((2, TM, d), _BF16), pltpu.VMEM((TM, d), _F32)],
        compiler_params=_params("arbitrary"),
        name="mlp_final" if final else "mlp",
    )(h, gain.reshape(1, d), w_up, w_down, gf)


def _gla_kernel(q_ref, k_ref, v_ref, r_ref, wg_ref, bg_ref, o_ref, s_ref, b_ref):
    step = pl.program_id(0)
    c = GLA_CHUNK
    nsub = c // GLA_SUB

    @pl.when(step == 0)
    def _():
        s_ref[...] = jnp.zeros_like(s_ref)
        b_ref[0] = jnp.zeros((c, GLA_QK), _F32)

    b_all = b_ref[step % 2]

    causal = [lax.broadcasted_iota(jnp.int32, (GLA_SUB, c), 1)
              <= lax.broadcasted_iota(jnp.int32, (GLA_SUB, c), 0) + i * GLA_SUB for i in range(nsub)]

    def rows_of(t, i):
        return t[i * GLA_SUB:(i + 1) * GLA_SUB]

    def scaled(t, e):
        return t * jnp.exp2(e).astype(_BF16)

    q_shift = math.log2(GLA_DK ** -0.5)

    def attention_weights(h):
        ks = slice(h * GLA_DK, (h + 1) * GLA_DK)
        b = b_all[:, ks]
        beta = [jnp.zeros((1, GLA_DK), _F32)] + [b[i * GLA_SUB - 1:i * GLA_SUB, :] for i in range(1, nsub + 1)]
        q_exp = jnp.concatenate([rows_of(b, i) - beta[i] for i in range(nsub)], axis=0)
        k_exp = jnp.concatenate([beta[i + 1] - rows_of(b, i) for i in range(nsub)], axis=0)
        qs = q_ref[:, ks] * jnp.exp2(q_exp + q_shift).astype(_BF16)
        kd = k_ref[:, ks] * jnp.exp2(k_exp).astype(_BF16)
        a_rows = []
        for i in range(nsub):
            kt = jnp.concatenate([scaled(rows_of(kd, j), beta[i] - beta[j + 1]) for j in range(i + 1)]
                                 + [rows_of(kd, j) for j in range(i + 1, nsub)], axis=0)
            a_rows.append(jnp.where(causal[i], _dot_nt(rows_of(qs, i), kt), 0.0).astype(_BF16))
        qe = jnp.concatenate([scaled(rows_of(qs, i), beta[i]) for i in range(nsub)], axis=0)
        k_dec = jnp.concatenate([scaled(rows_of(kd, j), beta[nsub] - beta[j + 1]) for j in range(nsub)], axis=0)
        decay = jnp.exp2(jnp.broadcast_to(beta[nsub], (LANES, GLA_DK))).T
        return jnp.concatenate([jnp.concatenate(a_rows, axis=0), qe], axis=1), k_dec, decay

    def outputs(h, lhs, k_dec, decay):
        vs = slice(h * GLA_DV, (h + 1) * GLA_DV)
        vh = v_ref[:, vs]
        state = s_ref[h]
        o_ref[:, vs] = _dot(lhs, jnp.concatenate([vh, state.astype(_BF16)], axis=0)).astype(o_ref.dtype)
        s_ref[h] = state * jnp.concatenate([decay] * (GLA_DV // LANES), axis=1) + _dot_tn(k_dec, vh)

    def gate_logits():
        return _dot(r_ref[...], wg_ref[...]) + bg_ref[...]

    def gate_rows(x, i):
        xs = rows_of(x, i)
        gk = (jnp.minimum(xs, 0.0) - jnp.log1p(jnp.exp(-jnp.abs(xs)))) * (LOG2E / GLA_GATE_NORM)
        row = (step + 1) * c + i * GLA_SUB + lax.broadcasted_iota(jnp.int32, (GLA_SUB, 1), 0)
        gk = jnp.where(row >= META_ROW0, gk, 0.0)
        hi = gk.astype(_BF16)
        return hi, (gk - hi.astype(_F32)).astype(_BF16)

    def cumulative(pairs):
        tri = (lax.broadcasted_iota(jnp.int32, (c, c), 0)
               >= lax.broadcasted_iota(jnp.int32, (c, c), 1)).astype(_BF16)
        hi = jnp.concatenate([p[0] for p in pairs], axis=0)
        lo = jnp.concatenate([p[1] for p in pairs], axis=0)
        return _dot(tri, hi) + _dot(tri, lo)

    assert nsub == GLA_HEADS
    x_gate = gate_logits()
    pairs = []
    pending = None
    for h in range(GLA_HEADS):
        ready = attention_weights(h)
        pairs.append(gate_rows(x_gate, h))
        if pending is not None:
            outputs(h - 1, *pending)
        pending = ready
    b_next = cumulative(pairs)
    outputs(GLA_HEADS - 1, *pending)
    b_ref[(step + 1) % 2] = b_next


def _gla(proj, r, w_gate, b_gate):
    rows = proj.shape[0]
    c = GLA_CHUNK
    steps = rows // c
    last = steps - 1
    return pl.pallas_call(
        _gla_kernel,
        grid=(steps,),
        in_specs=[
            pl.BlockSpec((c, GLA_QK), lambda s: (s, 0)),
            pl.BlockSpec((c, GLA_QK), lambda s: (s, 1)),
            pl.BlockSpec((c, GLA_VD), lambda s: (s, 1)),
            pl.BlockSpec((c, RANK_PAD), lambda s: (jnp.minimum(s + 1, last), 0)),
            pl.BlockSpec((RANK_PAD, GLA_QK), lambda s: (0, 0)),
            pl.BlockSpec((1, GLA_QK), lambda s: (0, 0)),
        ],
        out_specs=pl.BlockSpec((c, GLA_VD), lambda s: (s, 0)),
        out_shape=jax.ShapeDtypeStruct((rows, GLA_VD), _BF16),
        scratch_shapes=[pltpu.VMEM((GLA_HEADS, GLA_DK, GLA_DV), _F32),
                        pltpu.VMEM((2, c, GLA_QK), _F32)],
        compiler_params=_params("arbitrary"),
        name="gla",
    )(proj, proj, proj, r, w_gate, b_gate.reshape(1, GLA_QK))


def _gla_out_kernel(o_ref, g_ref, no_ref, w_ref, head_ref, x_ref, out_ref):
    parts = []
    for h in range(GLA_HEADS):
        vs = slice(h * GLA_DV, (h + 1) * GLA_DV)
        o = o_ref[:, vs].astype(_F32)
        hg = 0.5 * g_ref[:, vs].astype(_F32)
        silu = hg * jnp.tanh(hg) + hg
        parts.append((o * _rms_scale(o) * no_ref[...] * silu).astype(_BF16))
    y = jnp.concatenate(parts, axis=1)
    out_ref[...] = _stream_tile(head_ref, x_ref, pl.program_id(0)) + _dot(y, w_ref[...])


def _gla_out(o, proj, norm_out, w, head, x):
    rows, k = o.shape
    d = w.shape[1]
    return pl.pallas_call(
        _gla_out_kernel,
        grid=(rows // TM,),
        in_specs=[
            pl.BlockSpec((TM, k), lambda i: (i, 0)),
            pl.BlockSpec((TM, GLA_VD), lambda i: (i, 2)),
            pl.BlockSpec((1, GLA_DV), lambda i: (0, 0)),
            pl.BlockSpec((k, d), lambda i: (0, 0)),
        ] + _stream_specs(d, 1),
        out_specs=pl.BlockSpec((TM, d), lambda i: (i, 0)),
        out_shape=jax.ShapeDtypeStruct((rows, d), _F32),
        compiler_params=_params("parallel"),
        name="gla_out",
    )(o, proj, norm_out.reshape(1, GLA_DV), w, head, x)


def _swap_halves(x, width):
    lane = lax.broadcasted_iota(jnp.int32, x.shape, 1)
    first = (lane % (2 * width)) < width
    return jnp.where(first, pltpu.roll(x, LANES - width, 1), pltpu.roll(x, width, 1))


def _qkv_kernel(h_ref, gq_ref, gkv_ref, wq_ref, wkv_ref, q_ref, k_ref, v_ref, tab_ref):
    half = SWA_HEAD_DIM // 2
    i = pl.program_id(0)

    def inv_freq(shape):
        lane = lax.broadcasted_iota(jnp.int32, shape, 1)
        return jnp.exp((lane % half).astype(_F32) * (-math.log(ROPE_THETA) / half))

    @pl.when(i == 0)
    def _():
        off = lax.broadcasted_iota(jnp.int32, (TM, LANES), 0).astype(_F32) * inv_freq((TM, LANES))
        tab_ref[0] = jnp.cos(off)
        tab_ref[1] = jnp.sin(off)

    x = h_ref[...]
    xhat = x * _rms_scale(x)
    q = _dot((xhat * gq_ref[...]).astype(_BF16), wq_ref[...])
    kv = _dot((xhat * gkv_ref[...]).astype(_BF16), wkv_ref[...])

    base = (i * TM - META_ROW0).astype(_F32) * inv_freq((8, LANES))[:1]
    cos_b, sin_b = jnp.cos(base), jnp.sin(base)
    cos = cos_b * tab_ref[0] - sin_b * tab_ref[1]
    sin = sin_b * tab_ref[0] + cos_b * tab_ref[1]
    lane = lax.broadcasted_iota(jnp.int32, (TM, LANES), 1)
    sin = jnp.where((lane % SWA_HEAD_DIM) < half, -sin, sin)

    def rope(t):
        return t * cos + _swap_halves(t, half) * sin

    qscale = SWA_HEAD_DIM ** -0.5 * LOG2E
    for p in range(SWA_Q_HEADS // 2):
        sl = slice(p * LANES, (p + 1) * LANES)
        q_ref[:, sl] = (rope(q[:, sl]) * qscale).astype(q_ref.dtype)

    low = (lane % LANES) < SWA_HEAD_DIM
    kv_width = SWA_KV_HEADS * SWA_HEAD_DIM
    for p in range(SWA_KV_HEADS // 2):
        sl = slice(p * LANES, (p + 1) * LANES)
        for t, o_ref in ((rope(kv[:, sl]), k_ref), (kv[:, kv_width + p * LANES:kv_width + (p + 1) * LANES], v_ref)):
            swapped = pltpu.roll(t, SWA_HEAD_DIM, 1)
            o_ref[:, (2 * p) * LANES:(2 * p + 1) * LANES] = jnp.where(low, t, swapped).astype(o_ref.dtype)
            o_ref[:, (2 * p + 1) * LANES:(2 * p + 2) * LANES] = jnp.where(low, swapped, t).astype(o_ref.dtype)


def _qkv(h, gain_q, gain_kv, w_q, w_kv):
    rows, d = h.shape
    nq = w_q.shape[1]
    nkv = w_kv.shape[1]
    dup = SWA_KV_HEADS * LANES
    return pl.pallas_call(
        _qkv_kernel,
        grid=(rows // TM,),
        in_specs=[
            pl.BlockSpec((TM, d), lambda i: (i, 0)),
            pl.BlockSpec((1, d), lambda i: (0, 0)),
            pl.BlockSpec((1, d), lambda i: (0, 0)),
            pl.BlockSpec((d, nq), lambda i: (0, 0)),
            pl.BlockSpec((d, nkv), lambda i: (0, 0)),
        ],
        out_specs=[
            pl.BlockSpec((TM, nq), lambda i: (i, 0)),
            pl.BlockSpec((TM, dup), lambda i: (i, 0)),
            pl.BlockSpec((TM, dup), lambda i: (i, 0)),
        ],
        out_shape=[
            jax.ShapeDtypeStruct((rows, nq), _BF16),
            jax.ShapeDtypeStruct((rows, dup), _BF16),
            jax.ShapeDtypeStruct((rows, dup), _BF16),
        ],
        scratch_shapes=[pltpu.VMEM((2, TM, LANES), _F32)],
        compiler_params=_params("arbitrary"),
        name="qkv_rope",
    )(h, gain_q.reshape(1, d), gain_kv.reshape(1, d), w_q, w_kv)


def _swa_kernel(sink_ref, q_ref, kp_ref, kc_ref, km_ref, vp_ref, vc_ref, vm_ref, o_ref, bias_ref):
    n = pl.program_id(0)
    first_block = META_ROW0 // SWA_BLOCK
    meta_lo = META_ROW0 % SWA_BLOCK
    band_w = SWA_KEYS - N_META
    win_start = (0, 2 * N_META + 2 * SWA_BLOCK - SWA_KEYS)
    spare_col = (SWA_KEYS - 1, 0)

    @pl.when((n >= first_block) & (n <= first_block + 2))
    def _():
        qi0 = lax.broadcasted_iota(jnp.int32, (SWA_HALF, SWA_KEYS), 0)
        col = lax.broadcasted_iota(jnp.int32, (SWA_HALF, SWA_KEYS), 1)
        for hq in range(2):
            qi = qi0 + hq * SWA_HALF
            if hq == 0:
                is_meta = col < N_META
                kj = col - N_META
                m = col
            else:
                is_meta = col >= band_w
                kj = col + (2 * SWA_BLOCK - band_w)
                m = col - band_w
            band_ok = (kj > qi) & (kj <= qi + SWA_WINDOW) & ((n - 1) * SWA_BLOCK + kj >= FRONT)
            meta_ok = (META_ROW0 + m) <= n * SWA_BLOCK + qi
            ok = (is_meta & meta_ok) | (jnp.logical_not(is_meta) & band_ok)
            mask_bias = jnp.where(ok, 0.0, NEG_INF).astype(_F32)
            for g in range(SWA_KV_HEADS):
                bias_ref[2 * g + hq] = jnp.concatenate(
                    [jnp.where(col == spare_col[hq], sink_ref[g * SWA_GROUP + j] * LOG2E, mask_bias)
                     for j in range(SWA_GROUP)], axis=0)

    @pl.when(n < first_block)
    def _():
        o_ref[...] = jnp.zeros_like(o_ref)

    @pl.when(n >= first_block)
    def _():
        kb = jnp.concatenate([km_ref[meta_lo:, :], kp_ref[...], kc_ref[...], km_ref[meta_lo:, :]], axis=0)
        vb = jnp.concatenate([vm_ref[meta_lo:, :], vp_ref[...], vc_ref[...], vm_ref[meta_lo:, :]], axis=0)
        key_row = lax.broadcasted_iota(jnp.int32, kb.shape, 0)
        spare_row = [key_row == win_start[hq] + spare_col[hq] for hq in range(2)]
        kbs = [jnp.where(spare_row[hq], jnp.zeros_like(kb), kb) for hq in range(2)]
        vbs = [jnp.where(spare_row[hq], jnp.zeros_like(vb), vb) for hq in range(2)]
        lane = lax.broadcasted_iota(jnp.int32, (SWA_HALF, LANES), 1)
        low = lane < SWA_HEAD_DIM
        high = lane >= SWA_HEAD_DIM
        ones = jnp.ones((SWA_KEYS, LANES), _BF16)

        def scores(g, hq):
            rows = slice(hq * SWA_HALF, (hq + 1) * SWA_HALF)
            parts = []
            for j in range(SWA_GROUP):
                pair = (g * SWA_GROUP + j) // 2
                t = q_ref[rows, pair * LANES:(pair + 1) * LANES]
                parts.append(jnp.where(low if j % 2 == 0 else high, t, jnp.zeros_like(t)))
            lhs = jnp.concatenate(parts, axis=0)
            start = win_start[hq]
            s = _dot_nt(lhs, kbs[hq][start:start + SWA_KEYS, g * LANES:(g + 1) * LANES]) + bias_ref[2 * g + hq]
            row_max = jnp.max(s, axis=-1, keepdims=True)
            return s, jnp.broadcast_to(row_max, (SWA_GROUP * SWA_HALF, LANES))

        def probs(s, m):
            return jnp.concatenate([jnp.exp2(s[:, c * LANES:(c + 1) * LANES] - m)
                                    for c in range(SWA_KEYS // LANES)], axis=1).astype(_BF16)

        def outputs(p, g, hq):
            start = win_start[hq]
            rows = slice(hq * SWA_HALF, (hq + 1) * SWA_HALF)
            vw = jnp.concatenate([vbs[hq][start:start + SWA_KEYS, g * LANES:(g + 1) * LANES], ones], axis=1)
            od = _dot(p, vw)
            o = od[:, :LANES] / od[:, LANES:]
            for j in range(0, SWA_GROUP, 2):
                pair = (g * SWA_GROUP + j) // 2
                even = o[j * SWA_HALF:(j + 1) * SWA_HALF]
                odd = o[(j + 1) * SWA_HALF:(j + 2) * SWA_HALF]
                o_ref[rows, pair * LANES:(pair + 1) * LANES] = jnp.where(low, even, odd).astype(o_ref.dtype)

        tiles = [(g, hq) for g in range(SWA_KV_HEADS) for hq in range(2)]
        stage_a, stage_b = {}, {}
        for t in range(len(tiles) + 2):
            if t < len(tiles):
                stage_a[t] = scores(*tiles[t])
            if 0 <= t - 1 < len(tiles):
                stage_b[t - 1] = probs(*stage_a.pop(t - 1))
            if 0 <= t - 2 < len(tiles):
                outputs(stage_b.pop(t - 2), *tiles[t - 2])


def _swa(q, k_dup, v_dup, sinks):
    rows, nq = q.shape
    steps = rows // SWA_BLOCK
    dup = k_dup.shape[1]
    first_block = META_ROW0 // SWA_BLOCK
    blk = pl.BlockSpec((SWA_BLOCK, dup), lambda n: (n, 0))
    prev = pl.BlockSpec((SWA_BLOCK, dup), lambda n: (jnp.maximum(n - 1, 0), 0))
    meta = pl.BlockSpec((SWA_BLOCK, dup), lambda n: (first_block, 0))
    return pl.pallas_call(
        _swa_kernel,
        grid=(steps,),
        in_specs=[
            pl.BlockSpec(memory_space=pltpu.SMEM),
            pl.BlockSpec((SWA_BLOCK, nq), lambda n: (n, 0)),
            prev, blk, meta, prev, blk, meta,
        ],
        out_specs=pl.BlockSpec((SWA_BLOCK, nq), lambda n: (n, 0)),
        out_shape=jax.ShapeDtypeStruct((rows, nq), _BF16),
        scratch_shapes=[pltpu.VMEM((2 * SWA_KV_HEADS, SWA_GROUP * SWA_HALF, SWA_KEYS), _F32)],
        compiler_params=_params("arbitrary"),
        name="swa",
    )(sinks, q, k_dup, k_dup, k_dup, v_dup, v_dup, v_dup)


def kernel(x, meta_tokens, norm_mix, norm_mlp, w_mlp_up, w_mlp_down, a_w_in, a_w_gate_up, a_b_gate, a_norm_out, a_w_out, kv_norm, w_kv, b_w_q, b_sinks, b_w_out, norm_final):
    batch, seq, d = x.shape
    assert batch == 1 and d == D_MODEL and seq % TM == 0
    assert norm_mix.shape[0] == 2 and a_w_in.shape[0] == 1 and b_w_q.shape[0] == 1

    head = jnp.concatenate([jnp.zeros((META_ROW0, d), x.dtype), meta_tokens.astype(x.dtype)], axis=0)
    x2 = x[0]
    bf = lambda w: w.astype(_BF16)
    n_main = GLA_IN - GLA_RANK
    w_in = bf(a_w_in[0])
    w_r = jnp.pad(w_in[:, n_main:], ((0, 0), (0, RANK_PAD - GLA_RANK)))
    w_gate = jnp.pad(bf(a_w_gate_up[0]), ((0, RANK_PAD - GLA_RANK), (0, 0)))

    proj, r, (w_up, w_down) = _in_proj(head, x2, norm_mix[0], w_in, n_main, w_r, w_mlp_up, w_mlp_down, ())
    w_out_a, w_q, w_out_b, w_kv_b = bf(a_w_out[0]), bf(b_w_q[0]), bf(b_w_out[0]), bf(w_kv)
    o = _gla(proj, r, w_gate, a_b_gate[0])
    h = _gla_out(o, proj, a_norm_out[0], w_out_a, head, x2)
    h = _mlp(h, norm_mlp[0], w_up, w_down, 0)

    q, k_dup, v_dup = _qkv(h, norm_mix[1], kv_norm, w_q, w_kv_b)
    o = _swa(q, k_dup, v_dup, b_sinks[0])
    h = _matmul_residual(o, w_out_b, h)
    return _mlp(h, norm_mlp[1], w_up, w_down, 1, final_gain=norm_final)[None]
```

```python
import functools
import math

import jax
import jax.numpy as jnp
from jax import lax
from jax.experimental import pallas as pl
from jax.experimental.pallas import tpu as pltpu

D_MODEL = 2048
N_META = 16
D_FF = 4 * D_MODEL
RMS_EPS = 1e-6
NEG_INF = -1e30

GLA_HEADS = 4
GLA_DK = 256
GLA_DV = 512
GLA_QK = GLA_HEADS * GLA_DK
GLA_VD = GLA_HEADS * GLA_DV
GLA_RANK = 16
GLA_GATE_NORM = 16.0
GLA_IN = 2 * GLA_QK + 2 * GLA_VD + GLA_RANK

SWA_HEAD_DIM = 64
SWA_Q_HEADS = 32
SWA_KV_HEADS = 4
SWA_GROUP = SWA_Q_HEADS // SWA_KV_HEADS
SWA_WINDOW = 128
ROPE_THETA = 10000.0

LANES = 128
BF16_ROWS = 16
FRONT = 512
META_ROW0 = FRONT - N_META
TM = 512
IN_PROJ_TN = 1024
MLP_TF = 2048
GLA_CHUNK = 256
GLA_SUB = 64
SWA_BLOCK = 128
SWA_HALF = SWA_BLOCK // 2
SWA_KEYS = 2 * SWA_BLOCK
LOG2E = math.log2(math.e)
RANK_PAD = LANES
MAX_CAST_STEPS = 128
VMEM_LIMIT = 58 * 1024 * 1024

_F32 = jnp.float32
_BF16 = jnp.bfloat16


def _dot(a, b):
    return jnp.dot(a, b, preferred_element_type=_F32)


def _dot_nt(a, b):
    return lax.dot_general(a, b, (((1,), (1,)), ((), ())), preferred_element_type=_F32)


def _dot_tn(a, b):
    return lax.dot_general(a, b, (((0,), (0,)), ((), ())), preferred_element_type=_F32)


def _params(*semantics):
    return pltpu.CompilerParams(dimension_semantics=semantics, vmem_limit_bytes=VMEM_LIMIT)


def _rms_scale(x):
    return lax.rsqrt(jnp.mean(x * x, axis=-1, keepdims=True) + RMS_EPS)


def _normed(x, g_ref):
    return (x * _rms_scale(x) * g_ref[...]).astype(_BF16)


def _norm_next_rows(src_ref, g_ref, xn_ref, slot, j, n_steps):
    slab = -(-TM // (n_steps * BF16_ROWS)) * BF16_ROWS
    start = pl.multiple_of(jnp.minimum(j * slab, TM - slab), BF16_ROWS)
    xn_ref[slot, pl.ds(start, slab), :] = _normed(src_ref[pl.ds(start, slab), :], g_ref)


def _stream_tile(head_ref, x_ref, i):
    return jnp.where(i == 0, head_ref[...], x_ref[...])


def _stream_specs(d, grid_rank):
    if grid_rank == 1:
        return [pl.BlockSpec((TM, d), lambda i: (0, 0)),
                pl.BlockSpec((TM, d), lambda i: (jnp.maximum(i - 1, 0), 0))]
    return [pl.BlockSpec((TM, d), lambda i, j: (0, 0)),
            pl.BlockSpec((TM, d), lambda i, j: (jnp.maximum(i - 1, 0), 0))]


def _cast_slabs(src_refs, dst_refs):
    (wu_src, wd_src, *small_src), (wu, wd, *small) = src_refs, dst_refs
    for t in range(wu.shape[0]):
        wu[t] = wu_src[:, t * MLP_TF:(t + 1) * MLP_TF].astype(_BF16)
    wd[...] = wd_src[...].astype(_BF16)
    for src, dst in zip(small_src, small):
        dst[...] = src[...].astype(_BF16)


def _cast_specs(w_up, w_down, small, n_tiles, cast_steps):
    layers, d, ff = w_up.shape
    per_layer = cast_steps // layers
    ru, rd = d // per_layer, ff // per_layer
    step = lambda i, j: jnp.minimum(i * n_tiles + j, cast_steps - 1)
    layer = lambda i, j: step(i, j) // per_layer
    slab = lambda i, j: step(i, j) % per_layer
    in_specs = [pl.BlockSpec((None, ru, ff), lambda i, j: (layer(i, j), slab(i, j), 0)),
                pl.BlockSpec((None, rd, d), lambda i, j: (layer(i, j), slab(i, j), 0))]
    out_specs = [pl.BlockSpec((None, ff // MLP_TF, ru, MLP_TF), lambda i, j: (layer(i, j), 0, slab(i, j), 0)),
                 pl.BlockSpec((None, rd, d), lambda i, j: (layer(i, j), slab(i, j), 0))]
    out_shapes = [jax.ShapeDtypeStruct((layers, ff // MLP_TF, d, MLP_TF), _BF16),
                  jax.ShapeDtypeStruct((layers, ff, d), _BF16)]
    for w in small:
        r, c = w.shape
        spec = pl.BlockSpec((r // cast_steps, c), lambda i, j: (step(i, j), 0))
        in_specs.append(spec)
        out_specs.append(spec)
        out_shapes.append(jax.ShapeDtypeStruct((r, c), _BF16))
    return in_specs, out_specs, out_shapes


def _in_proj_kernel(*refs, n_tiles, n_cast):
    head_ref, xnext_ref, g_ref, w_ref, wr_ref = refs[:5]
    cast_src = refs[5:5 + n_cast]
    o_ref, r_ref = refs[5 + n_cast:7 + n_cast]
    cast_dst = refs[7 + n_cast:7 + 2 * n_cast]
    xn_ref = refs[-1]
    i, j = pl.program_id(0), pl.program_id(1)
    slot = i % 2

    @pl.when((i == 0) & (j == 0))
    def _():
        xn_ref[0] = _normed(head_ref[...], g_ref)

    def step(first):
        xn = xn_ref[slot]
        cols = pl.ds(pl.multiple_of(j * IN_PROJ_TN, IN_PROJ_TN), IN_PROJ_TN)
        o_ref[...] = _dot(xn, w_ref[:, cols]).astype(o_ref.dtype)
        if first:
            r_ref[...] = _dot(xn, wr_ref[...]).astype(r_ref.dtype)
        _norm_next_rows(xnext_ref, g_ref, xn_ref, 1 - slot, j, n_tiles)
        _cast_slabs(cast_src, cast_dst)

    pl.when(j == 0)(functools.partial(step, True))
    pl.when(j > 0)(functools.partial(step, False))


def _in_proj(head, x, gain, w, n, w_r, w_up, w_down, small):
    d = x.shape[1]
    rows = head.shape[0] + x.shape[0]
    tn = IN_PROJ_TN
    n_tiles = n // tn
    nr = w_r.shape[1]
    last_x = x.shape[0] // TM - 1
    steps = rows // TM * n_tiles
    cast_steps = 1 << (min(steps, MAX_CAST_STEPS).bit_length() - 1)
    cast_in, cast_out, cast_shapes = _cast_specs(w_up, w_down, small, n_tiles, cast_steps)
    proj, r, *casts = pl.pallas_call(
        functools.partial(_in_proj_kernel, n_tiles=n_tiles, n_cast=len(cast_in)),
        grid=(rows // TM, n_tiles),
        in_specs=[
            pl.BlockSpec((TM, d), lambda i, j: (0, 0)),
            pl.BlockSpec((TM, d), lambda i, j: (jnp.minimum(i, last_x), 0)),
            pl.BlockSpec((1, d), lambda i, j: (0, 0)),
            pl.BlockSpec(w.shape, lambda i, j: (0, 0), pipeline_mode=pl.Buffered(1)),
            pl.BlockSpec((d, nr), lambda i, j: (0, 0)),
        ] + cast_in,
        out_specs=[pl.BlockSpec((TM, tn), lambda i, j: (i, j)),
                   pl.BlockSpec((TM, nr), lambda i, j: (i, 0))] + cast_out,
        out_shape=[jax.ShapeDtypeStruct((rows, n), _BF16),
                   jax.ShapeDtypeStruct((rows, nr), _BF16)] + cast_shapes,
        scratch_shapes=[pltpu.VMEM((2, TM, d), _BF16)],
        compiler_params=_params("arbitrary", "arbitrary"),
        name="in_proj",
    )(head, x, gain.reshape(1, d), w, w_r, w_up, w_down, *small)
    return proj, r, casts


def _matmul_residual_kernel(a_ref, w_ref, h_ref, o_ref):
    o_ref[...] = h_ref[...] + _dot(a_ref[...], w_ref[...])


def _matmul_residual(a, w, h):
    rows, k = a.shape
    d = w.shape[1]
    return pl.pallas_call(
        _matmul_residual_kernel,
        grid=(rows // TM,),
        in_specs=[
            pl.BlockSpec((TM, k), lambda i: (i, 0)),
            pl.BlockSpec((k, d), lambda i: (0, 0)),
            pl.BlockSpec((TM, d), lambda i: (i, 0)),
        ],
        out_specs=pl.BlockSpec((TM, d), lambda i: (i, 0)),
        out_shape=jax.ShapeDtypeStruct((rows, d), _F32),
        compiler_params=_params("parallel"),
        name="matmul_residual",
    )(a, w, h)


def _mlp_kernel(slab_ref, g_ref, wu_ref, wd_ref, gf_ref, o_ref, xn_ref, res_ref, *, n_tiles, final):
    t = pl.program_id(0)
    j = t % n_tiles
    slot = (t // n_tiles) % 2
    slab = TM // n_tiles
    rows = pl.ds(pl.multiple_of(j * slab, slab), slab)

    def stage():
        x = slab_ref[...]
        res_ref[rows, :] = x
        xn_ref[slot, rows, :] = _normed(x, g_ref)

    def step(first):
        u = jnp.maximum(_dot(xn_ref[1 - slot], wu_ref[...]), 0.0)
        part = _dot((u * u).astype(_BF16), wd_ref[...])
        if first:
            o_ref[...] = res_ref[...] + part
        else:
            o_ref[...] += part
        stage()

    pl.when(t < n_tiles)(stage)
    pl.when((t >= n_tiles) & (j == 0))(functools.partial(step, True))
    pl.when((t >= n_tiles) & (j > 0))(functools.partial(step, False))

    if final:
        @pl.when((t >= n_tiles) & (j == n_tiles - 1))
        def _():
            y = o_ref[...]
            o_ref[...] = y * _rms_scale(y) * gf_ref[...]


def _mlp(h, gain, w_up, w_down, layer, final_gain=None):
    rows, d = h.shape
    _, n_tiles, _, tf = w_up.shape
    assert TM % (n_tiles * BF16_ROWS) == 0
    final = final_gain is not None
    skip = FRONT // TM if final else 0
    out_rows = rows - skip * TM
    slab = TM // n_tiles
    last_slab = rows // slab - 1
    gf = (final_gain if final else gain).reshape(1, d)
    hidden = lambda t: jnp.where(t < n_tiles, 0, t % n_tiles)
    row_tile = lambda t: jnp.maximum(t // n_tiles - 1, 0)
    return pl.pallas_call(
        functools.partial(_mlp_kernel, n_tiles=n_tiles, final=final),
        grid=(out_rows // TM * n_tiles + n_tiles,),
        in_specs=[
            pl.BlockSpec((slab, d), lambda t: (jnp.minimum(t + skip * n_tiles, last_slab), 0)),
            pl.BlockSpec((1, d), lambda t: (0, 0)),
            pl.BlockSpec((None, None, d, tf), lambda t: (layer, hidden(t), 0, 0)),
            pl.BlockSpec((None, tf, d), lambda t: (layer, hidden(t), 0)),
            pl.BlockSpec((1, d), lambda t: (0, 0)),
        ],
        out_specs=pl.BlockSpec((TM, d), lambda t: (row_tile(t), 0)),
        out_shape=jax.ShapeDtypeStruct((out_rows, d), _F32),
        scratch_shapes=[pltpu.VMEM((2, TM, d), _BF16), pltpu.VMEM((TM, d), _F32)],
        compiler_params=_params("arbitrary"),
        name="mlp_final" if final else "mlp",
    )(h, gain.reshape(1, d), w_up, w_down, gf)


def _gla_kernel(q_ref, k_ref, v_ref, r_ref, wg_ref, bg_ref, o_ref, s_ref, b_ref):
    step = pl.program_id(0)
    c = GLA_CHUNK
    nsub = c // GLA_SUB

    @pl.when(step == 0)
    def _():
        s_ref[...] = jnp.zeros_like(s_ref)
        b_ref[0] = jnp.zeros((c, GLA_QK), _F32)

    b_all = b_ref[step % 2]

    causal = [lax.broadcasted_iota(jnp.int32, (GLA_SUB, c), 1)
              <= lax.broadcasted_iota(jnp.int32, (GLA_SUB, c), 0) + i * GLA_SUB for i in range(nsub)]

    def rows_of(t, i):
        return t[i * GLA_SUB:(i + 1) * GLA_SUB]

    def scaled(t, e):
        return t * jnp.exp2(e).astype(_BF16)

    q_shift = math.log2(GLA_DK ** -0.5)

    def attention_weights(h):
        ks = slice(h * GLA_DK, (h + 1) * GLA_DK)
        b = b_all[:, ks]
        beta = [jnp.zeros((1, GLA_DK), _F32)] + [b[i * GLA_SUB - 1:i * GLA_SUB, :] for i in range(1, nsub + 1)]
        q_exp = jnp.concatenate([rows_of(b, i) - beta[i] for i in range(nsub)], axis=0)
        k_exp = jnp.concatenate([beta[i + 1] - rows_of(b, i) for i in range(nsub)], axis=0)
        qs = q_ref[:, ks] * jnp.exp2(q_exp + q_shift).astype(_BF16)
        kd = k_ref[:, ks] * jnp.exp2(k_exp).astype(_BF16)
        a_rows = []
        for i in range(nsub):
            kt = jnp.concatenate([scaled(rows_of(kd, j), beta[i] - beta[j + 1]) for j in range(i + 1)]
                                 + [rows_of(kd, j) for j in range(i + 1, nsub)], axis=0)
            a_rows.append(jnp.where(causal[i], _dot_nt(rows_of(qs, i), kt), 0.0).astype(_BF16))
        qe = jnp.concatenate([scaled(rows_of(qs, i), beta[i]) for i in range(nsub)], axis=0)
        k_dec = jnp.concatenate([scaled(rows_of(kd, j), beta[nsub] - beta[j + 1]) for j in range(nsub)], axis=0)
        decay = jnp.exp2(jnp.broadcast_to(beta[nsub], (LANES, GLA_DK))).T
        return jnp.concatenate([jnp.concatenate(a_rows, axis=0), qe], axis=1), k_dec, decay

    def outputs(h, lhs, k_dec, decay):
        vs = slice(h * GLA_DV, (h + 1) * GLA_DV)
        vh = v_ref[:, vs]
        state = s_ref[h]
        o_ref[:, vs] = _dot(lhs, jnp.concatenate([vh, state.astype(_BF16)], axis=0)).astype(o_ref.dtype)
        s_ref[h] = state * jnp.concatenate([decay] * (GLA_DV // LANES), axis=1) + _dot_tn(k_dec, vh)

    def gate_logits():
        return _dot(r_ref[...], wg_ref[...]) + bg_ref[...]

    def gate_rows(x, i):
        xs = rows_of(x, i)
        gk = (jnp.minimum(xs, 0.0) - jnp.log1p(jnp.exp(-jnp.abs(xs)))) * (LOG2E / GLA_GATE_NORM)
        row = (step + 1) * c + i * GLA_SUB + lax.broadcasted_iota(jnp.int32, (GLA_SUB, 1), 0)
        gk = jnp.where(row >= META_ROW0, gk, 0.0)
        hi = gk.astype(_BF16)
        return hi, (gk - hi.astype(_F32)).astype(_BF16)

    def cumulative(pairs):
        tri = (lax.broadcasted_iota(jnp.int32, (c, c), 0)
               >= lax.broadcasted_iota(jnp.int32, (c, c), 1)).astype(_BF16)
        hi = jnp.concatenate([p[0] for p in pairs], axis=0)
        lo = jnp.concatenate([p[1] for p in pairs], axis=0)
        return _dot(tri, hi) + _dot(tri, lo)

    assert nsub == GLA_HEADS
    x_gate = gate_logits()
    pairs = []
    pending = None
    for h in range(GLA_HEADS):
        ready = attention_weights(h)
        pairs.append(gate_rows(x_gate, h))
        if pending is not None:
            outputs(h - 1, *pending)
        pending = ready
    b_next = cumulative(pairs)
    outputs(GLA_HEADS - 1, *pending)
    b_ref[(step + 1) % 2] = b_next


def _gla(proj, r, w_gate, b_gate):
    rows = proj.shape[0]
    c = GLA_CHUNK
    steps = rows // c
    last = steps - 1
    return pl.pallas_call(
        _gla_kernel,
        grid=(steps,),
        in_specs=[
            pl.BlockSpec((c, GLA_QK), lambda s: (s, 0)),
            pl.BlockSpec((c, GLA_QK), lambda s: (s, 1)),
            pl.BlockSpec((c, GLA_VD), lambda s: (s, 1)),
            pl.BlockSpec((c, RANK_PAD), lambda s: (jnp.minimum(s + 1, last), 0)),
            pl.BlockSpec((RANK_PAD, GLA_QK), lambda s: (0, 0)),
            pl.BlockSpec((1, GLA_QK), lambda s: (0, 0)),
        ],
        out_specs=pl.BlockSpec((c, GLA_VD), lambda s: (s, 0)),
        out_shape=jax.ShapeDtypeStruct((rows, GLA_VD), _BF16),
        scratch_shapes=[pltpu.VMEM((GLA_HEADS, GLA_DK, GLA_DV), _F32),
                        pltpu.VMEM((2, c, GLA_QK), _F32)],
        compiler_params=_params("arbitrary"),
        name="gla",
    )(proj, proj, proj, r, w_gate, b_gate.reshape(1, GLA_QK))


def _gla_out_kernel(o_ref, g_ref, no_ref, w_ref, head_ref, x_ref, out_ref):
    parts = []
    for h in range(GLA_HEADS):
        vs = slice(h * GLA_DV, (h + 1) * GLA_DV)
        o = o_ref[:, vs].astype(_F32)
        hg = 0.5 * g_ref[:, vs].astype(_F32)
        silu = hg * jnp.tanh(hg) + hg
        parts.append((o * _rms_scale(o) * no_ref[...] * silu).astype(_BF16))
    y = jnp.concatenate(parts, axis=1)
    out_ref[...] = _stream_tile(head_ref, x_ref, pl.program_id(0)) + _dot(y, w_ref[...])


def _gla_out(o, proj, norm_out, w, head, x):
    rows, k = o.shape
    d = w.shape[1]
    return pl.pallas_call(
        _gla_out_kernel,
        grid=(rows // TM,),
        in_specs=[
            pl.BlockSpec((TM, k), lambda i: (i, 0)),
            pl.BlockSpec((TM, GLA_VD), lambda i: (i, 2)),
            pl.BlockSpec((1, GLA_DV), lambda i: (0, 0)),
            pl.BlockSpec((k, d), lambda i: (0, 0)),
        ] + _stream_specs(d, 1),
        out_specs=pl.BlockSpec((TM, d), lambda i: (i, 0)),
        out_shape=jax.ShapeDtypeStruct((rows, d), _F32),
        compiler_params=_params("parallel"),
        name="gla_out",
    )(o, proj, norm_out.reshape(1, GLA_DV), w, head, x)


def _swap_halves(x, width):
    lane = lax.broadcasted_iota(jnp.int32, x.shape, 1)
    first = (lane % (2 * width)) < width
    return jnp.where(first, pltpu.roll(x, LANES - width, 1), pltpu.roll(x, width, 1))


def _qkv_kernel(h_ref, gq_ref, gkv_ref, wq_ref, wkv_ref, q_ref, k_ref, v_ref, tab_ref):
    half = SWA_HEAD_DIM // 2
    i = pl.program_id(0)

    def inv_freq(shape):
        lane = lax.broadcasted_iota(jnp.int32, shape, 1)
        return jnp.exp((lane % half).astype(_F32) * (-math.log(ROPE_THETA) / half))

    @pl.when(i == 0)
    def _():
        off = lax.broadcasted_iota(jnp.int32, (TM, LANES), 0).astype(_F32) * inv_freq((TM, LANES))
        tab_ref[0] = jnp.cos(off)
        tab_ref[1] = jnp.sin(off)

    x = h_ref[...]
    xhat = x * _rms_scale(x)
    q = _dot((xhat * gq_ref[...]).astype(_BF16), wq_ref[...])
    kv = _dot((xhat * gkv_ref[...]).astype(_BF16), wkv_ref[...])

    base = (i * TM - META_ROW0).astype(_F32) * inv_freq((8, LANES))[:1]
    cos_b, sin_b = jnp.cos(base), jnp.sin(base)
    cos = cos_b * tab_ref[0] - sin_b * tab_ref[1]
    sin = sin_b * tab_ref[0] + cos_b * tab_ref[1]
    lane = lax.broadcasted_iota(jnp.int32, (TM, LANES), 1)
    sin = jnp.where((lane % SWA_HEAD_DIM) < half, -sin, sin)

    def rope(t):
        return t * cos + _swap_halves(t, half) * sin

    qscale = SWA_HEAD_DIM ** -0.5 * LOG2E
    for p in range(SWA_Q_HEADS // 2):
        sl = slice(p * LANES, (p + 1) * LANES)
        q_ref[:, sl] = (rope(q[:, sl]) * qscale).astype(q_ref.dtype)

    low = (lane % LANES) < SWA_HEAD_DIM
    kv_width = SWA_KV_HEADS * SWA_HEAD_DIM
    for p in range(SWA_KV_HEADS // 2):
        sl = slice(p * LANES, (p + 1) * LANES)
        for t, o_ref in ((rope(kv[:, sl]), k_ref), (kv[:, kv_width + p * LANES:kv_width + (p + 1) * LANES], v_ref)):
            swapped = pltpu.roll(t, SWA_HEAD_DIM, 1)
            o_ref[:, (2 * p) * LANES:(2 * p + 1) * LANES] = jnp.where(low, t, swapped).astype(o_ref.dtype)
            o_ref[:, (2 * p + 1) * LANES:(2 * p + 2) * LANES] = jnp.where(low, swapped, t).astype(o_ref.dtype)


def _qkv(h, gain_q, gain_kv, w_q, w_kv):
    rows, d = h.shape
    nq = w_q.shape[1]
    nkv = w_kv.shape[1]
    dup = SWA_KV_HEADS * LANES
    return pl.pallas_call(
        _qkv_kernel,
        grid=(rows // TM,),
        in_specs=[
            pl.BlockSpec((TM, d), lambda i: (i, 0)),
            pl.BlockSpec((1, d), lambda i: (0, 0)),
            pl.BlockSpec((1, d), lambda i: (0, 0)),
            pl.BlockSpec((d, nq), lambda i: (0, 0)),
            pl.BlockSpec((d, nkv), lambda i: (0, 0)),
        ],
        out_specs=[
            pl.BlockSpec((TM, nq), lambda i: (i, 0)),
            pl.BlockSpec((TM, dup), lambda i: (i, 0)),
            pl.BlockSpec((TM, dup), lambda i: (i, 0)),
        ],
        out_shape=[
            jax.ShapeDtypeStruct((rows, nq), _BF16),
            jax.ShapeDtypeStruct((rows, dup), _BF16),
            jax.ShapeDtypeStruct((rows, dup), _BF16),
        ],
        scratch_shapes=[pltpu.VMEM((2, TM, LANES), _F32)],
        compiler_params=_params("arbitrary"),
        name="qkv_rope",
    )(h, gain_q.reshape(1, d), gain_kv.reshape(1, d), w_q, w_kv)


def _swa_kernel(sink_ref, q_ref, kp_ref, kc_ref, km_ref, vp_ref, vc_ref, vm_ref, o_ref, bias_ref):
    n = pl.program_id(0)
    first_block = META_ROW0 // SWA_BLOCK
    meta_lo = META_ROW0 % SWA_BLOCK
    band_w = SWA_KEYS - N_META
    win_start = (0, 2 * N_META + 2 * SWA_BLOCK - SWA_KEYS)
    spare_col = (SWA_KEYS - 1, 0)

    @pl.when((n >= first_block) & (n <= first_block + 2))
    def _():
        qi0 = lax.broadcasted_iota(jnp.int32, (SWA_HALF, SWA_KEYS), 0)
        col = lax.broadcasted_iota(jnp.int32, (SWA_HALF, SWA_KEYS), 1)
        for hq in range(2):
            qi = qi0 + hq * SWA_HALF
            if hq == 0:
                is_meta = col < N_META
                kj = col - N_META
                m = col
            else:
                is_meta = col >= band_w
                kj = col + (2 * SWA_BLOCK - band_w)
                m = col - band_w
            band_ok = (kj > qi) & (kj <= qi + SWA_WINDOW) & ((n - 1) * SWA_BLOCK + kj >= FRONT)
            meta_ok = (META_ROW0 + m) <= n * SWA_BLOCK + qi
            ok = (is_meta & meta_ok) | (jnp.logical_not(is_meta) & band_ok)
            mask_bias = jnp.where(ok, 0.0, NEG_INF).astype(_F32)
            for g in range(SWA_KV_HEADS):
                bias_ref[2 * g + hq] = jnp.concatenate(
                    [jnp.where(col == spare_col[hq], sink_ref[g * SWA_GROUP + j] * LOG2E, mask_bias)
                     for j in range(SWA_GROUP)], axis=0)

    @pl.when(n < first_block)
    def _():
        o_ref[...] = jnp.zeros_like(o_ref)

    @pl.when(n >= first_block)
    def _():
        kb = jnp.concatenate([km_ref[meta_lo:, :], kp_ref[...], kc_ref[...], km_ref[meta_lo:, :]], axis=0)
        vb = jnp.concatenate([vm_ref[meta_lo:, :], vp_ref[...], vc_ref[...], vm_ref[meta_lo:, :]], axis=0)
        key_row = lax.broadcasted_iota(jnp.int32, kb.shape, 0)
        spare_row = [key_row == win_start[hq] + spare_col[hq] for hq in range(2)]
        kbs = [jnp.where(spare_row[hq], jnp.zeros_like(kb), kb) for hq in range(2)]
        vbs = [jnp.where(spare_row[hq], jnp.zeros_like(vb), vb) for hq in range(2)]
        lane = lax.broadcasted_iota(jnp.int32, (SWA_HALF, LANES), 1)
        low = lane < SWA_HEAD_DIM
        high = lane >= SWA_HEAD_DIM
        ones = jnp.ones((SWA_KEYS, LANES), _BF16)

        def scores(g, hq):
            rows = slice(hq * SWA_HALF, (hq + 1) * SWA_HALF)
            parts = []
            for j in range(SWA_GROUP):
                pair = (g * SWA_GROUP + j) // 2
                t = q_ref[rows, pair * LANES:(pair + 1) * LANES]
                parts.append(jnp.where(low if j % 2 == 0 else high, t, jnp.zeros_like(t)))
            lhs = jnp.concatenate(parts, axis=0)
            start = win_start[hq]
            s = _dot_nt(lhs, kbs[hq][start:start + SWA_KEYS, g * LANES:(g + 1) * LANES]) + bias_ref[2 * g + hq]
            row_max = jnp.max(s, axis=-1, keepdims=True)
            return s, jnp.broadcast_to(row_max, (SWA_GROUP * SWA_HALF, LANES))

        def probs(s, m):
            return jnp.concatenate([jnp.exp2(s[:, c * LANES:(c + 1) * LANES] - m)
                                    for c in range(SWA_KEYS // LANES)], axis=1).astype(_BF16)

        def outputs(p, g, hq):
            start = win_start[hq]
            rows = slice(hq * SWA_HALF, (hq + 1) * SWA_HALF)
            vw = jnp.concatenate([vbs[hq][start:start + SWA_KEYS, g * LANES:(g + 1) * LANES], ones], axis=1)
            od = _dot(p, vw)
            o = od[:, :LANES] / od[:, LANES:]
            for j in range(0, SWA_GROUP, 2):
                pair = (g * SWA_GROUP + j) // 2
                even = o[j * SWA_HALF:(j + 1) * SWA_HALF]
                odd = o[(j + 1) * SWA_HALF:(j + 2) * SWA_HALF]
                o_ref[rows, pair * LANES:(pair + 1) * LANES] = jnp.where(low, even, odd).astype(o_ref.dtype)

        tiles = [(g, hq) for g in range(SWA_KV_HEADS) for hq in range(2)]
        stage_a, stage_b = {}, {}
        for t in range(len(tiles) + 2):
            if t < len(tiles):
                stage_a[t] = scores(*tiles[t])
            if 0 <= t - 1 < len(tiles):
                stage_b[t - 1] = probs(*stage_a.pop(t - 1))
            if 0 <= t - 2 < len(tiles):
                outputs(stage_b.pop(t - 2), *tiles[t - 2])


def _swa(q, k_dup, v_dup, sinks):
    rows, nq = q.shape
    steps = rows // SWA_BLOCK
    dup = k_dup.shape[1]
    first_block = META_ROW0 // SWA_BLOCK
    blk = pl.BlockSpec((SWA_BLOCK, dup), lambda n: (n, 0))
    prev = pl.BlockSpec((SWA_BLOCK, dup), lambda n: (jnp.maximum(n - 1, 0), 0))
    meta = pl.BlockSpec((SWA_BLOCK, dup), lambda n: (first_block, 0))
    return pl.pallas_call(
        _swa_kernel,
        grid=(steps,),
        in_specs=[
            pl.BlockSpec(memory_space=pltpu.SMEM),
            pl.BlockSpec((SWA_BLOCK, nq), lambda n: (n, 0)),
            prev, blk, meta, prev, blk, meta,
        ],
        out_specs=pl.BlockSpec((SWA_BLOCK, nq), lambda n: (n, 0)),
        out_shape=jax.ShapeDtypeStruct((rows, nq), _BF16),
        scratch_shapes=[pltpu.VMEM((2 * SWA_KV_HEADS, SWA_GROUP * SWA_HALF, SWA_KEYS), _F32)],
        compiler_params=_params("arbitrary"),
        name="swa",
    )(sinks, q, k_dup, k_dup, k_dup, v_dup, v_dup, v_dup)


def kernel(x, meta_tokens, norm_mix, norm_mlp, w_mlp_up, w_mlp_down, a_w_in, a_w_gate_up, a_b_gate, a_norm_out, a_w_out, kv_norm, w_kv, b_w_q, b_sinks, b_w_out, norm_final):
    batch, seq, d = x.shape
    assert batch == 1 and d == D_MODEL and seq % TM == 0
    assert norm_mix.shape[0] == 2 and a_w_in.shape[0] == 1 and b_w_q.shape[0] == 1

    head = jnp.concatenate([jnp.zeros((META_ROW0, d), x.dtype), meta_tokens.astype(x.dtype)], axis=0)
    x2 = x[0]
    bf = lambda w: w.astype(_BF16)
    n_main = GLA_IN - GLA_RANK
    w_in = bf(a_w_in[0])
    w_r = jnp.pad(w_in[:, n_main:], ((0, 0), (0, RANK_PAD - GLA_RANK)))
    w_gate = jnp.pad(bf(a_w_gate_up[0]), ((0, RANK_PAD - GLA_RANK), (0, 0)))

    proj, r, (w_up, w_down, w_out_a, w_q, w_out_b, w_kv_b) = _in_proj(
        head, x2, norm_mix[0], w_in, n_main, w_r, w_mlp_up, w_mlp_down,
        (a_w_out[0], b_w_q[0], b_w_out[0], w_kv))
    o = _gla(proj, r, w_gate, a_b_gate[0])
    h = _gla_out(o, proj, a_norm_out[0], w_out_a, head, x2)
    h = _mlp(h, norm_mlp[0], w_up, w_down, 0)

    q, k_dup, v_dup = _qkv(h, norm_mix[1], kv_norm, w_q, w_kv_b)
    o = _swa(q, k_dup, v_dup, b_sinks[0])
    h = _matmul_residual(o, w_out_b, h)
    return _mlp(h, norm_mlp[1], w_up, w_down, 1, final_gain=norm_final)[None]
```

```python
import functools
import math

import jax
import jax.numpy as jnp
from jax import lax
from jax.experimental import pallas as pl
from jax.experimental.pallas import tpu as pltpu

D_MODEL = 2048
N_META = 16
D_FF = 4 * D_MODEL
RMS_EPS = 1e-6
NEG_INF = -1e30

GLA_HEADS = 4
GLA_DK = 256
GLA_DV = 512
GLA_QK = GLA_HEADS * GLA_DK
GLA_VD = GLA_HEADS * GLA_DV
GLA_RANK = 16
GLA_GATE_NORM = 16.0
GLA_IN = 2 * GLA_QK + 2 * GLA_VD + GLA_RANK

SWA_HEAD_DIM = 64
SWA_Q_HEADS = 32
SWA_KV_HEADS = 4
SWA_GROUP = SWA_Q_HEADS // SWA_KV_HEADS
SWA_WINDOW = 128
ROPE_THETA = 10000.0

LANES = 128
BF16_ROWS = 16
FRONT = 512
META_ROW0 = FRONT - N_META
TM = 512
IN_PROJ_TN = 1536
MLP_TF = 2048
GLA_CHUNK = 256
GLA_SUB = 64
SWA_BLOCK = 128
SWA_STEP_BLOCKS = 2
SWA_HALF = SWA_BLOCK // 2
SWA_KEYS = 2 * SWA_BLOCK
LOG2E = math.log2(math.e)
RANK_PAD = LANES
MAX_CAST_STEPS = 128
VMEM_LIMIT = 58 * 1024 * 1024

_F32 = jnp.float32
_BF16 = jnp.bfloat16


def _dot(a, b):
    return jnp.dot(a, b, preferred_element_type=_F32)


def _dot_nt(a, b):
    return lax.dot_general(a, b, (((1,), (1,)), ((), ())), preferred_element_type=_F32)


def _dot_tn(a, b):
    return lax.dot_general(a, b, (((0,), (0,)), ((), ())), preferred_element_type=_F32)


def _params(*semantics):
    return pltpu.CompilerParams(dimension_semantics=semantics, vmem_limit_bytes=VMEM_LIMIT)


def _rms_scale(x):
    return lax.rsqrt(jnp.mean(x * x, axis=-1, keepdims=True) + RMS_EPS)


def _normed(x, g_ref):
    return (x * _rms_scale(x) * g_ref[...]).astype(_BF16)


def _norm_next_rows(src_ref, g_ref, xn_ref, slot, j, n_steps):
    slab = -(-TM // (n_steps * BF16_ROWS)) * BF16_ROWS
    start = pl.multiple_of(jnp.minimum(j * slab, TM - slab), BF16_ROWS)
    xn_ref[slot, pl.ds(start, slab), :] = _normed(src_ref[pl.ds(start, slab), :], g_ref)


def _stream_tile(head_ref, x_ref, i):
    return jnp.where(i == 0, head_ref[...], x_ref[...])


def _stream_specs(d, grid_rank):
    if grid_rank == 1:
        return [pl.BlockSpec((TM, d), lambda i: (0, 0)),
                pl.BlockSpec((TM, d), lambda i: (jnp.maximum(i - 1, 0), 0))]
    return [pl.BlockSpec((TM, d), lambda i, j: (0, 0)),
            pl.BlockSpec((TM, d), lambda i, j: (jnp.maximum(i - 1, 0), 0))]


def _cast_slabs(src_refs, dst_refs):
    (wu_src, wd_src, *small_src), (wu, wd, *small) = src_refs, dst_refs
    for t in range(wu.shape[0]):
        wu[t] = wu_src[:, t * MLP_TF:(t + 1) * MLP_TF].astype(_BF16)
    wd[...] = wd_src[...].astype(_BF16)
    for src, dst in zip(small_src, small):
        dst[...] = src[...].astype(_BF16)


def _cast_specs(w_up, w_down, small, n_tiles, cast_steps):
    layers, d, ff = w_up.shape
    per_layer = cast_steps // layers
    ru, rd = d // per_layer, ff // per_layer
    step = lambda i, j: jnp.minimum(i * n_tiles + j, cast_steps - 1)
    layer = lambda i, j: step(i, j) // per_layer
    slab = lambda i, j: step(i, j) % per_layer
    in_specs = [pl.BlockSpec((None, ru, ff), lambda i, j: (layer(i, j), slab(i, j), 0)),
                pl.BlockSpec((None, rd, d), lambda i, j: (layer(i, j), slab(i, j), 0))]
    out_specs = [pl.BlockSpec((None, ff // MLP_TF, ru, MLP_TF), lambda i, j: (layer(i, j), 0, slab(i, j), 0)),
                 pl.BlockSpec((None, rd, d), lambda i, j: (layer(i, j), slab(i, j), 0))]
    out_shapes = [jax.ShapeDtypeStruct((layers, ff // MLP_TF, d, MLP_TF), _BF16),
                  jax.ShapeDtypeStruct((layers, ff, d), _BF16)]
    for w in small:
        r, c = w.shape
        spec = pl.BlockSpec((r // cast_steps, c), lambda i, j: (step(i, j), 0))
        in_specs.append(spec)
        out_specs.append(spec)
        out_shapes.append(jax.ShapeDtypeStruct((r, c), _BF16))
    return in_specs, out_specs, out_shapes


def _in_proj_kernel(*refs, n_tiles, n_cast):
    head_ref, xnext_ref, g_ref, w_ref, wr_ref = refs[:5]
    cast_src = refs[5:5 + n_cast]
    o_ref, r_ref = refs[5 + n_cast:7 + n_cast]
    cast_dst = refs[7 + n_cast:7 + 2 * n_cast]
    xn_ref = refs[-1]
    i, j = pl.program_id(0), pl.program_id(1)
    slot = i % 2

    @pl.when((i == 0) & (j == 0))
    def _():
        xn_ref[0] = _normed(head_ref[...], g_ref)

    def step(first):
        xn = xn_ref[slot]
        cols = pl.ds(pl.multiple_of(j * IN_PROJ_TN, IN_PROJ_TN), IN_PROJ_TN)
        o_ref[...] = _dot(xn, w_ref[:, cols]).astype(o_ref.dtype)
        if first:
            r_ref[...] = _dot(xn, wr_ref[...]).astype(r_ref.dtype)
        _norm_next_rows(xnext_ref, g_ref, xn_ref, 1 - slot, j, n_tiles)
        _cast_slabs(cast_src, cast_dst)

    pl.when(j == 0)(functools.partial(step, True))
    pl.when(j > 0)(functools.partial(step, False))


def _in_proj(head, x, gain, w, n, w_r, w_up, w_down, small):
    d = x.shape[1]
    rows = head.shape[0] + x.shape[0]
    tn = IN_PROJ_TN
    n_tiles = n // tn
    nr = w_r.shape[1]
    last_x = x.shape[0] // TM - 1
    steps = rows // TM * n_tiles
    cast_steps = 1 << (min(steps, MAX_CAST_STEPS).bit_length() - 1)
    cast_in, cast_out, cast_shapes = _cast_specs(w_up, w_down, small, n_tiles, cast_steps)
    proj, r, *casts = pl.pallas_call(
        functools.partial(_in_proj_kernel, n_tiles=n_tiles, n_cast=len(cast_in)),
        grid=(rows // TM, n_tiles),
        in_specs=[
            pl.BlockSpec((TM, d), lambda i, j: (0, 0)),
            pl.BlockSpec((TM, d), lambda i, j: (jnp.minimum(i, last_x), 0)),
            pl.BlockSpec((1, d), lambda i, j: (0, 0)),
            pl.BlockSpec(w.shape, lambda i, j: (0, 0), pipeline_mode=pl.Buffered(1)),
            pl.BlockSpec((d, nr), lambda i, j: (0, 0)),
        ] + cast_in,
        out_specs=[pl.BlockSpec((TM, tn), lambda i, j: (i, j)),
                   pl.BlockSpec((TM, nr), lambda i, j: (i, 0))] + cast_out,
        out_shape=[jax.ShapeDtypeStruct((rows, n), _BF16),
                   jax.ShapeDtypeStruct((rows, nr), _BF16)] + cast_shapes,
        scratch_shapes=[pltpu.VMEM((2, TM, d), _BF16)],
        compiler_params=_params("arbitrary", "arbitrary"),
        name="in_proj",
    )(head, x, gain.reshape(1, d), w, w_r, w_up, w_down, *small)
    return proj, r, casts


def _matmul_residual_kernel(a_ref, w_ref, h_ref, o_ref):
    o_ref[...] = h_ref[...] + _dot(a_ref[...], w_ref[...])


def _matmul_residual(a, w, h):
    rows, k = a.shape
    d = w.shape[1]
    return pl.pallas_call(
        _matmul_residual_kernel,
        grid=(rows // TM,),
        in_specs=[
            pl.BlockSpec((TM, k), lambda i: (i, 0)),
            pl.BlockSpec((k, d), lambda i: (0, 0)),
            pl.BlockSpec((TM, d), lambda i: (i, 0)),
        ],
        out_specs=pl.BlockSpec((TM, d), lambda i: (i, 0)),
        out_shape=jax.ShapeDtypeStruct((rows, d), _F32),
        compiler_params=_params("parallel"),
        name="matmul_residual",
    )(a, w, h)


def _mlp_kernel(slab_ref, g_ref, wu_ref, wd_ref, gf_ref, o_ref, xn_ref, res_ref, *, n_tiles, final):
    t = pl.program_id(0)
    j = t % n_tiles
    slot = (t // n_tiles) % 2
    slab = TM // n_tiles
    rows = pl.ds(pl.multiple_of(j * slab, slab), slab)

    def stage():
        x = slab_ref[...]
        res_ref[rows, :] = x
        xn_ref[slot, rows, :] = _normed(x, g_ref)

    def step(first):
        u = jnp.maximum(_dot(xn_ref[1 - slot], wu_ref[...]), 0.0)
        part = _dot((u * u).astype(_BF16), wd_ref[...])
        if first:
            o_ref[...] = res_ref[...] + part
        else:
            o_ref[...] += part
        stage()

    pl.when(t < n_tiles)(stage)
    pl.when((t >= n_tiles) & (j == 0))(functools.partial(step, True))
    pl.when((t >= n_tiles) & (j > 0))(functools.partial(step, False))

    if final:
        @pl.when((t >= n_tiles) & (j == n_tiles - 1))
        def _():
            y = o_ref[...]
            o_ref[...] = y * _rms_scale(y) * gf_ref[...]


def _mlp(h, gain, w_up, w_down, layer, final_gain=None):
    rows, d = h.shape
    _, n_tiles, _, tf = w_up.shape
    assert TM % (n_tiles * BF16_ROWS) == 0
    final = final_gain is not None
    skip = FRONT // TM if final else 0
    out_rows = rows - skip * TM
    slab = TM // n_tiles
    last_slab = rows // slab - 1
    gf = (final_gain if final else gain).reshape(1, d)
    hidden = lambda t: jnp.where(t < n_tiles, 0, t % n_tiles)
    row_tile = lambda t: jnp.maximum(t // n_tiles - 1, 0)
    return pl.pallas_call(
        functools.partial(_mlp_kernel, n_tiles=n_tiles, final=final),
        grid=(out_rows // TM * n_tiles + n_tiles,),
        in_specs=[
            pl.BlockSpec((slab, d), lambda t: (jnp.minimum(t + skip * n_tiles, last_slab), 0)),
            pl.BlockSpec((1, d), lambda t: (0, 0)),
            pl.BlockSpec((None, None, d, tf), lambda t: (layer, hidden(t), 0, 0)),
            pl.BlockSpec((None, tf, d), lambda t: (layer, hidden(t), 0)),
            pl.BlockSpec((1, d), lambda t: (0, 0)),
        ],
        out_specs=pl.BlockSpec((TM, d), lambda t: (row_tile(t), 0)),
        out_shape=jax.ShapeDtypeStruct((out_rows, d), _F32),
        scratch_shapes=[pltpu.VMEM((2, TM, d), _BF16), pltpu.VMEM((TM, d), _F32)],
        compiler_params=_params("arbitrary"),
        name="mlp_final" if final else "mlp",
    )(h, gain.reshape(1, d), w_up, w_down, gf)


def _gla_kernel(q_ref, k_ref, v_ref, r_ref, wg_ref, bg_ref, o_ref, s_ref, b_ref):
    step = pl.program_id(0)
    c = GLA_CHUNK
    nsub = c // GLA_SUB

    @pl.when(step == 0)
    def _():
        s_ref[...] = jnp.zeros_like(s_ref)
        b_ref[0] = jnp.zeros((c, GLA_QK), _F32)

    b_all = b_ref[step % 2]

    causal = [lax.broadcasted_iota(jnp.int32, (GLA_SUB, c), 1)
              <= lax.broadcasted_iota(jnp.int32, (GLA_SUB, c), 0) + i * GLA_SUB for i in range(nsub)]

    def rows_of(t, i):
        return t[i * GLA_SUB:(i + 1) * GLA_SUB]

    def scaled(t, e):
        return t * jnp.exp2(e).astype(_BF16)

    q_shift = math.log2(GLA_DK ** -0.5)

    def attention_weights(h):
        ks = slice(h * GLA_DK, (h + 1) * GLA_DK)
        b = b_all[:, ks]
        beta = [jnp.zeros((1, GLA_DK), _F32)] + [b[i * GLA_SUB - 1:i * GLA_SUB, :] for i in range(1, nsub + 1)]
        q_exp = jnp.concatenate([rows_of(b, i) - beta[i] for i in range(nsub)], axis=0)
        k_exp = jnp.concatenate([beta[i + 1] - rows_of(b, i) for i in range(nsub)], axis=0)
        qs = q_ref[:, ks] * jnp.exp2(q_exp + q_shift).astype(_BF16)
        kd = k_ref[:, ks] * jnp.exp2(k_exp).astype(_BF16)
        a_rows = []
        for i in range(nsub):
            kt = jnp.concatenate([scaled(rows_of(kd, j), beta[i] - beta[j + 1]) for j in range(i + 1)]
                                 + [rows_of(kd, j) for j in range(i + 1, nsub)], axis=0)
            a_rows.append(jnp.where(causal[i], _dot_nt(rows_of(qs, i), kt), 0.0).astype(_BF16))
        qe = jnp.concatenate([scaled(rows_of(qs, i), beta[i]) for i in range(nsub)], axis=0)
        k_dec = jnp.concatenate([scaled(rows_of(kd, j), beta[nsub] - beta[j + 1]) for j in range(nsub)], axis=0)
        decay = jnp.exp2(jnp.broadcast_to(beta[nsub], (LANES, GLA_DK))).T
        return jnp.concatenate([jnp.concatenate(a_rows, axis=0), qe], axis=1), k_dec, decay

    def outputs(h, lhs, k_dec, decay):
        vs = slice(h * GLA_DV, (h + 1) * GLA_DV)
        vh = v_ref[:, vs]
        state = s_ref[h]
        o_ref[:, vs] = _dot(lhs, jnp.concatenate([vh, state.astype(_BF16)], axis=0)).astype(o_ref.dtype)
        s_ref[h] = state * jnp.concatenate([decay] * (GLA_DV // LANES), axis=1) + _dot_tn(k_dec, vh)

    def gate_logits():
        return _dot(r_ref[...], wg_ref[...]) + bg_ref[...]

    def gate_rows(x, i):
        xs = rows_of(x, i)
        gk = (jnp.minimum(xs, 0.0) - jnp.log1p(jnp.exp(-jnp.abs(xs)))) * (LOG2E / GLA_GATE_NORM)
        row = (step + 1) * c + i * GLA_SUB + lax.broadcasted_iota(jnp.int32, (GLA_SUB, 1), 0)
        gk = jnp.where(row >= META_ROW0, gk, 0.0)
        hi = gk.astype(_BF16)
        return hi, (gk - hi.astype(_F32)).astype(_BF16)

    def cumulative(pairs):
        tri = (lax.broadcasted_iota(jnp.int32, (c, c), 0)
               >= lax.broadcasted_iota(jnp.int32, (c, c), 1)).astype(_BF16)
        hi = jnp.concatenate([p[0] for p in pairs], axis=0)
        lo = jnp.concatenate([p[1] for p in pairs], axis=0)
        return _dot(tri, hi) + _dot(tri, lo)

    assert nsub == GLA_HEADS
    x_gate = gate_logits()
    pairs = []
    pending = None
    for h in range(GLA_HEADS):
        ready = attention_weights(h)
        pairs.append(gate_rows(x_gate, h))
        if pending is not None:
            outputs(h - 1, *pending)
        pending = ready
    b_next = cumulative(pairs)
    outputs(GLA_HEADS - 1, *pending)
    b_ref[(step + 1) % 2] = b_next


def _gla(proj, r, w_gate, b_gate):
    rows = proj.shape[0]
    c = GLA_CHUNK
    steps = rows // c
    last = steps - 1
    return pl.pallas_call(
        _gla_kernel,
        grid=(steps,),
        in_specs=[
            pl.BlockSpec((c, GLA_QK), lambda s: (s, 0)),
            pl.BlockSpec((c, GLA_QK), lambda s: (s, 1)),
            pl.BlockSpec((c, GLA_VD), lambda s: (s, 1)),
            pl.BlockSpec((c, RANK_PAD), lambda s: (jnp.minimum(s + 1, last), 0)),
            pl.BlockSpec((RANK_PAD, GLA_QK), lambda s: (0, 0)),
            pl.BlockSpec((1, GLA_QK), lambda s: (0, 0)),
        ],
        out_specs=pl.BlockSpec((c, GLA_VD), lambda s: (s, 0)),
        out_shape=jax.ShapeDtypeStruct((rows, GLA_VD), _BF16),
        scratch_shapes=[pltpu.VMEM((GLA_HEADS, GLA_DK, GLA_DV), _F32),
                        pltpu.VMEM((2, c, GLA_QK), _F32)],
        compiler_params=_params("arbitrary"),
        name="gla",
    )(proj, proj, proj, r, w_gate, b_gate.reshape(1, GLA_QK))


def _gla_out_kernel(o_ref, g_ref, no_ref, w_ref, head_ref, x_ref, out_ref):
    parts = []
    for h in range(GLA_HEADS):
        vs = slice(h * GLA_DV, (h + 1) * GLA_DV)
        o = o_ref[:, vs].astype(_F32)
        hg = 0.5 * g_ref[:, vs].astype(_F32)
        silu = hg * jnp.tanh(hg) + hg
        parts.append((o * _rms_scale(o) * no_ref[...] * silu).astype(_BF16))
    y = jnp.concatenate(parts, axis=1)
    out_ref[...] = _stream_tile(head_ref, x_ref, pl.program_id(0)) + _dot(y, w_ref[...])


def _gla_out(o, proj, norm_out, w, head, x):
    rows, k = o.shape
    d = w.shape[1]
    return pl.pallas_call(
        _gla_out_kernel,
        grid=(rows // TM,),
        in_specs=[
            pl.BlockSpec((TM, k), lambda i: (i, 0)),
            pl.BlockSpec((TM, GLA_VD), lambda i: (i, 2)),
            pl.BlockSpec((1, GLA_DV), lambda i: (0, 0)),
            pl.BlockSpec((k, d), lambda i: (0, 0)),
        ] + _stream_specs(d, 1),
        out_specs=pl.BlockSpec((TM, d), lambda i: (i, 0)),
        out_shape=jax.ShapeDtypeStruct((rows, d), _F32),
        compiler_params=_params("parallel"),
        name="gla_out",
    )(o, proj, norm_out.reshape(1, GLA_DV), w, head, x)


def _swap_halves(x, width):
    lane = lax.broadcasted_iota(jnp.int32, x.shape, 1)
    first = (lane % (2 * width)) < width
    return jnp.where(first, pltpu.roll(x, LANES - width, 1), pltpu.roll(x, width, 1))


def _qkv_kernel(h_ref, gq_ref, gkv_ref, wq_ref, wkv_ref, q_ref, k_ref, v_ref, tab_ref):
    half = SWA_HEAD_DIM // 2
    i = pl.program_id(0)

    def inv_freq(shape):
        lane = lax.broadcasted_iota(jnp.int32, shape, 1)
        return jnp.exp((lane % half).astype(_F32) * (-math.log(ROPE_THETA) / half))

    @pl.when(i == 0)
    def _():
        off = lax.broadcasted_iota(jnp.int32, (TM, LANES), 0).astype(_F32) * inv_freq((TM, LANES))
        tab_ref[0] = jnp.cos(off)
        tab_ref[1] = jnp.sin(off)

    x = h_ref[...]
    xhat = x * _rms_scale(x)
    q = _dot((xhat * gq_ref[...]).astype(_BF16), wq_ref[...])
    kv = _dot((xhat * gkv_ref[...]).astype(_BF16), wkv_ref[...])

    base = (i * TM - META_ROW0).astype(_F32) * inv_freq((8, LANES))[:1]
    cos_b, sin_b = jnp.cos(base), jnp.sin(base)
    cos = cos_b * tab_ref[0] - sin_b * tab_ref[1]
    sin = sin_b * tab_ref[0] + cos_b * tab_ref[1]
    lane = lax.broadcasted_iota(jnp.int32, (TM, LANES), 1)
    sin = jnp.where((lane % SWA_HEAD_DIM) < half, -sin, sin)

    def rope(t):
        return t * cos + _swap_halves(t, half) * sin

    qscale = SWA_HEAD_DIM ** -0.5 * LOG2E
    for p in range(SWA_Q_HEADS // 2):
        sl = slice(p * LANES, (p + 1) * LANES)
        q_ref[:, sl] = (rope(q[:, sl]) * qscale).astype(q_ref.dtype)

    low = (lane % LANES) < SWA_HEAD_DIM
    kv_width = SWA_KV_HEADS * SWA_HEAD_DIM
    for p in range(SWA_KV_HEADS // 2):
        sl = slice(p * LANES, (p + 1) * LANES)
        for t, o_ref in ((rope(kv[:, sl]), k_ref), (kv[:, kv_width + p * LANES:kv_width + (p + 1) * LANES], v_ref)):
            swapped = pltpu.roll(t, SWA_HEAD_DIM, 1)
            o_ref[:, (2 * p) * LANES:(2 * p + 1) * LANES] = jnp.where(low, t, swapped).astype(o_ref.dtype)
            o_ref[:, (2 * p + 1) * LANES:(2 * p + 2) * LANES] = jnp.where(low, swapped, t).astype(o_ref.dtype)


def _qkv(h, gain_q, gain_kv, w_q, w_kv):
    rows, d = h.shape
    nq = w_q.shape[1]
    nkv = w_kv.shape[1]
    dup = SWA_KV_HEADS * LANES
    return pl.pallas_call(
        _qkv_kernel,
        grid=(rows // TM,),
        in_specs=[
            pl.BlockSpec((TM, d), lambda i: (i, 0)),
            pl.BlockSpec((1, d), lambda i: (0, 0)),
            pl.BlockSpec((1, d), lambda i: (0, 0)),
            pl.BlockSpec((d, nq), lambda i: (0, 0)),
            pl.BlockSpec((d, nkv), lambda i: (0, 0)),
        ],
        out_specs=[
            pl.BlockSpec((TM, nq), lambda i: (i, 0)),
            pl.BlockSpec((TM, dup), lambda i: (i, 0)),
            pl.BlockSpec((TM, dup), lambda i: (i, 0)),
        ],
        out_shape=[
            jax.ShapeDtypeStruct((rows, nq), _BF16),
            jax.ShapeDtypeStruct((rows, dup), _BF16),
            jax.ShapeDtypeStruct((rows, dup), _BF16),
        ],
        scratch_shapes=[pltpu.VMEM((2, TM, LANES), _F32)],
        compiler_params=_params("arbitrary"),
        name="qkv_rope",
    )(h, gain_q.reshape(1, d), gain_kv.reshape(1, d), w_q, w_kv)


def _swa_kernel(sink_ref, q_ref, kp_ref, kc_ref, km_ref, vp_ref, vc_ref, vm_ref, o_ref, bias_ref):
    for blk in range(SWA_STEP_BLOCKS):
        _swa_block(pl.program_id(0) * SWA_STEP_BLOCKS + blk, blk, sink_ref, q_ref, kp_ref, kc_ref, km_ref,
                   vp_ref, vc_ref, vm_ref, o_ref, bias_ref)


def _swa_block(n, blk, sink_ref, q_ref, kp_ref, kc_ref, km_ref, vp_ref, vc_ref, vm_ref, o_ref, bias_ref):
    row0 = blk * SWA_BLOCK
    first_block = META_ROW0 // SWA_BLOCK
    meta_lo = META_ROW0 % SWA_BLOCK
    band_w = SWA_KEYS - N_META
    win_start = (0, 2 * N_META + 2 * SWA_BLOCK - SWA_KEYS)
    spare_col = (SWA_KEYS - 1, 0)

    @pl.when((n >= first_block) & (n <= first_block + 2))
    def _():
        qi0 = lax.broadcasted_iota(jnp.int32, (SWA_HALF, SWA_KEYS), 0)
        col = lax.broadcasted_iota(jnp.int32, (SWA_HALF, SWA_KEYS), 1)
        for hq in range(2):
            qi = qi0 + hq * SWA_HALF
            if hq == 0:
                is_meta = col < N_META
                kj = col - N_META
                m = col
            else:
                is_meta = col >= band_w
                kj = col + (2 * SWA_BLOCK - band_w)
                m = col - band_w
            band_ok = (kj > qi) & (kj <= qi + SWA_WINDOW) & ((n - 1) * SWA_BLOCK + kj >= FRONT)
            meta_ok = (META_ROW0 + m) <= n * SWA_BLOCK + qi
            ok = (is_meta & meta_ok) | (jnp.logical_not(is_meta) & band_ok)
            mask_bias = jnp.where(ok, 0.0, NEG_INF).astype(_F32)
            for g in range(SWA_KV_HEADS):
                bias_ref[2 * g + hq] = jnp.concatenate(
                    [jnp.where(col == spare_col[hq], sink_ref[g * SWA_GROUP + j] * LOG2E, mask_bias)
                     for j in range(SWA_GROUP)], axis=0)

    @pl.when(n < first_block)
    def _():
        o_ref[row0:row0 + SWA_BLOCK, :] = jnp.zeros((SWA_BLOCK, o_ref.shape[1]), o_ref.dtype)

    @pl.when(n >= first_block)
    def _():
        cur = slice(row0, row0 + SWA_BLOCK)
        k_prev = kp_ref[...] if blk == 0 else kc_ref[row0 - SWA_BLOCK:row0, :]
        v_prev = vp_ref[...] if blk == 0 else vc_ref[row0 - SWA_BLOCK:row0, :]
        kb = jnp.concatenate([km_ref[meta_lo:, :], k_prev, kc_ref[cur, :], km_ref[meta_lo:, :]], axis=0)
        vb = jnp.concatenate([vm_ref[meta_lo:, :], v_prev, vc_ref[cur, :], vm_ref[meta_lo:, :]], axis=0)
        key_row = lax.broadcasted_iota(jnp.int32, kb.shape, 0)
        spare_row = [key_row == win_start[hq] + spare_col[hq] for hq in range(2)]
        kbs = [jnp.where(spare_row[hq], jnp.zeros_like(kb), kb) for hq in range(2)]
        vbs = [jnp.where(spare_row[hq], jnp.zeros_like(vb), vb) for hq in range(2)]
        lane = lax.broadcasted_iota(jnp.int32, (SWA_HALF, LANES), 1)
        low = lane < SWA_HEAD_DIM
        high = lane >= SWA_HEAD_DIM
        ones = jnp.ones((SWA_KEYS, LANES), _BF16)

        def scores(g, hq):
            rows = slice(row0 + hq * SWA_HALF, row0 + (hq + 1) * SWA_HALF)
            parts = []
            for j in range(SWA_GROUP):
                pair = (g * SWA_GROUP + j) // 2
                t = q_ref[rows, pair * LANES:(pair + 1) * LANES]
                parts.append(jnp.where(low if j % 2 == 0 else high, t, jnp.zeros_like(t)))
            lhs = jnp.concatenate(parts, axis=0)
            start = win_start[hq]
            s = _dot_nt(lhs, kbs[hq][start:start + SWA_KEYS, g * LANES:(g + 1) * LANES]) + bias_ref[2 * g + hq]
            row_max = jnp.max(s, axis=-1, keepdims=True)
            return s, jnp.broadcast_to(row_max, (SWA_GROUP * SWA_HALF, LANES))

        def probs(s, m):
            return jnp.concatenate([jnp.exp2(s[:, c * LANES:(c + 1) * LANES] - m)
                                    for c in range(SWA_KEYS // LANES)], axis=1).astype(_BF16)

        def outputs(p, g, hq):
            start = win_start[hq]
            rows = slice(row0 + hq * SWA_HALF, row0 + (hq + 1) * SWA_HALF)
            vw = jnp.concatenate([vbs[hq][start:start + SWA_KEYS, g * LANES:(g + 1) * LANES], ones], axis=1)
            od = _dot(p, vw)
            o = od[:, :LANES] / od[:, LANES:]
            for j in range(0, SWA_GROUP, 2):
                pair = (g * SWA_GROUP + j) // 2
                even = o[j * SWA_HALF:(j + 1) * SWA_HALF]
                odd = o[(j + 1) * SWA_HALF:(j + 2) * SWA_HALF]
                o_ref[rows, pair * LANES:(pair + 1) * LANES] = jnp.where(low, even, odd).astype(o_ref.dtype)

        tiles = [(g, hq) for g in range(SWA_KV_HEADS) for hq in range(2)]
        stage_a, stage_b = {}, {}
        for t in range(len(tiles) + 2):
            if t < len(tiles):
                stage_a[t] = scores(*tiles[t])
            if 0 <= t - 1 < len(tiles):
                stage_b[t - 1] = probs(*stage_a.pop(t - 1))
            if 0 <= t - 2 < len(tiles):
                outputs(stage_b.pop(t - 2), *tiles[t - 2])


def _swa(q, k_dup, v_dup, sinks):
    rows, nq = q.shape
    nb = SWA_STEP_BLOCKS
    steps = rows // (nb * SWA_BLOCK)
    dup = k_dup.shape[1]
    first_block = META_ROW0 // SWA_BLOCK
    blk = pl.BlockSpec((nb * SWA_BLOCK, dup), lambda n: (n, 0))
    prev = pl.BlockSpec((SWA_BLOCK, dup), lambda n: (jnp.maximum(nb * n - 1, 0), 0))
    meta = pl.BlockSpec((SWA_BLOCK, dup), lambda n: (first_block, 0))
    return pl.pallas_call(
        _swa_kernel,
        grid=(steps,),
        in_specs=[
            pl.BlockSpec(memory_space=pltpu.SMEM),
            pl.BlockSpec((nb * SWA_BLOCK, nq), lambda n: (n, 0)),
            prev, blk, meta, prev, blk, meta,
        ],
        out_specs=pl.BlockSpec((nb * SWA_BLOCK, nq), lambda n: (n, 0)),
        out_shape=jax.ShapeDtypeStruct((rows, nq), _BF16),
        scratch_shapes=[pltpu.VMEM((2 * SWA_KV_HEADS, SWA_GROUP * SWA_HALF, SWA_KEYS), _F32)],
        compiler_params=_params("arbitrary"),
        name="swa",
    )(sinks, q, k_dup, k_dup, k_dup, v_dup, v_dup, v_dup)


def kernel(x, meta_tokens, norm_mix, norm_mlp, w_mlp_up, w_mlp_down, a_w_in, a_w_gate_up, a_b_gate, a_norm_out, a_w_out, kv_norm, w_kv, b_w_q, b_sinks, b_w_out, norm_final):
    batch, seq, d = x.shape
    assert batch == 1 and d == D_MODEL and seq % TM == 0
    assert norm_mix.shape[0] == 2 and a_w_in.shape[0] == 1 and b_w_q.shape[0] == 1

    head = jnp.concatenate([jnp.zeros((META_ROW0, d), x.dtype), meta_tokens.astype(x.dtype)], axis=0)
    x2 = x[0]
    bf = lambda w: w.astype(_BF16)
    n_main = GLA_IN - GLA_RANK
    w_in = bf(a_w_in[0])
    w_r = jnp.pad(w_in[:, n_main:], ((0, 0), (0, RANK_PAD - GLA_RANK)))
    w_gate = jnp.pad(bf(a_w_gate_up[0]), ((0, RANK_PAD - GLA_RANK), (0, 0)))

    proj, r, (w_up, w_down, w_out_a, w_q, w_out_b, w_kv_b) = _in_proj(
        head, x2, norm_mix[0], w_in, n_main, w_r, w_mlp_up, w_mlp_down,
        (a_w_out[0], b_w_q[0], b_w_out[0], w_kv))
    o = _gla(proj, r, w_gate, a_b_gate[0])
    h = _gla_out(o, proj, a_norm_out[0], w_out_a, head, x2)
    h = _mlp(h, norm_mlp[0], w_up, w_down, 0)

    q, k_dup, v_dup = _qkv(h, norm_mix[1], kv_norm, w_q, w_kv_b)
    o = _swa(q, k_dup, v_dup, b_sinks[0])
    h = _matmul_residual(o, w_out_b, h)
    return _mlp(h, norm_mlp[1], w_up, w_down, 1, final_gain=norm_final)[None]
```

```python
import functools
import math

import jax
import jax.numpy as jnp
from jax import lax
from jax.experimental import pallas as pl
from jax.experimental.pallas import tpu as pltpu

D_MODEL = 2048
N_META = 16
D_FF = 4 * D_MODEL
RMS_EPS = 1e-6
NEG_INF = -1e30

GLA_HEADS = 4
GLA_DK = 256
GLA_DV = 512
GLA_QK = GLA_HEADS * GLA_DK
GLA_VD = GLA_HEADS * GLA_DV
GLA_RANK = 16
GLA_GATE_NORM = 16.0
GLA_IN = 2 * GLA_QK + 2 * GLA_VD + GLA_RANK

SWA_HEAD_DIM = 64
SWA_Q_HEADS = 32
SWA_KV_HEADS = 4
SWA_GROUP = SWA_Q_HEADS // SWA_KV_HEADS
SWA_WINDOW = 128
ROPE_THETA = 10000.0

LANES = 128
BF16_ROWS = 16
FRONT = 512
META_ROW0 = FRONT - N_META
TM = 512
IN_PROJ_TN = 1536
MLP_TF = 2048
GLA_CHUNK = 256
GLA_STEP_CHUNKS = 2
GLA_SUB = 64
SWA_BLOCK = 128
SWA_STEP_BLOCKS = 2
SWA_HALF = SWA_BLOCK // 2
SWA_KEYS = 2 * SWA_BLOCK
LOG2E = math.log2(math.e)
RANK_PAD = LANES
MAX_CAST_STEPS = 128
VMEM_LIMIT = 58 * 1024 * 1024

_F32 = jnp.float32
_BF16 = jnp.bfloat16


def _dot(a, b):
    return jnp.dot(a, b, preferred_element_type=_F32)


def _dot_nt(a, b):
    return lax.dot_general(a, b, (((1,), (1,)), ((), ())), preferred_element_type=_F32)


def _dot_tn(a, b):
    return lax.dot_general(a, b, (((0,), (0,)), ((), ())), preferred_element_type=_F32)


def _params(*semantics):
    return pltpu.CompilerParams(dimension_semantics=semantics, vmem_limit_bytes=VMEM_LIMIT)


def _rms_scale(x):
    return lax.rsqrt(jnp.mean(x * x, axis=-1, keepdims=True) + RMS_EPS)


def _normed(x, g_ref):
    return (x * _rms_scale(x) * g_ref[...]).astype(_BF16)


def _norm_next_rows(src_ref, g_ref, xn_ref, slot, j, n_steps):
    slab = -(-TM // (n_steps * BF16_ROWS)) * BF16_ROWS
    start = pl.multiple_of(jnp.minimum(j * slab, TM - slab), BF16_ROWS)
    xn_ref[slot, pl.ds(start, slab), :] = _normed(src_ref[pl.ds(start, slab), :], g_ref)


def _stream_tile(head_ref, x_ref, i):
    return jnp.where(i == 0, head_ref[...], x_ref[...])


def _stream_specs(d, grid_rank):
    if grid_rank == 1:
        return [pl.BlockSpec((TM, d), lambda i: (0, 0)),
                pl.BlockSpec((TM, d), lambda i: (jnp.maximum(i - 1, 0), 0))]
    return [pl.BlockSpec((TM, d), lambda i, j: (0, 0)),
            pl.BlockSpec((TM, d), lambda i, j: (jnp.maximum(i - 1, 0), 0))]


def _cast_slabs(src_refs, dst_refs):
    (wu_src, wd_src, *small_src), (wu, wd, *small) = src_refs, dst_refs
    for t in range(wu.shape[0]):
        wu[t] = wu_src[:, t * MLP_TF:(t + 1) * MLP_TF].astype(_BF16)
    wd[...] = wd_src[...].astype(_BF16)
    for src, dst in zip(small_src, small):
        dst[...] = src[...].astype(_BF16)


def _cast_specs(w_up, w_down, small, n_tiles, cast_steps):
    layers, d, ff = w_up.shape
    per_layer = cast_steps // layers
    ru, rd = d // per_layer, ff // per_layer
    step = lambda i, j: jnp.minimum(i * n_tiles + j, cast_steps - 1)
    layer = lambda i, j: step(i, j) // per_layer
    slab = lambda i, j: step(i, j) % per_layer
    in_specs = [pl.BlockSpec((None, ru, ff), lambda i, j: (layer(i, j), slab(i, j), 0)),
                pl.BlockSpec((None, rd, d), lambda i, j: (layer(i, j), slab(i, j), 0))]
    out_specs = [pl.BlockSpec((None, ff // MLP_TF, ru, MLP_TF), lambda i, j: (layer(i, j), 0, slab(i, j), 0)),
                 pl.BlockSpec((None, rd, d), lambda i, j: (layer(i, j), slab(i, j), 0))]
    out_shapes = [jax.ShapeDtypeStruct((layers, ff // MLP_TF, d, MLP_TF), _BF16),
                  jax.ShapeDtypeStruct((layers, ff, d), _BF16)]
    for w in small:
        r, c = w.shape
        spec = pl.BlockSpec((r // cast_steps, c), lambda i, j: (step(i, j), 0))
        in_specs.append(spec)
        out_specs.append(spec)
        out_shapes.append(jax.ShapeDtypeStruct((r, c), _BF16))
    return in_specs, out_specs, out_shapes


def _in_proj_kernel(*refs, n_tiles, n_cast):
    head_ref, xnext_ref, g_ref, w_ref, wr_ref = refs[:5]
    cast_src = refs[5:5 + n_cast]
    o_ref, r_ref = refs[5 + n_cast:7 + n_cast]
    cast_dst = refs[7 + n_cast:7 + 2 * n_cast]
    xn_ref = refs[-1]
    i, j = pl.program_id(0), pl.program_id(1)
    slot = i % 2

    @pl.when((i == 0) & (j == 0))
    def _():
        xn_ref[0] = _normed(head_ref[...], g_ref)

    def step(first):
        xn = xn_ref[slot]
        cols = pl.ds(pl.multiple_of(j * IN_PROJ_TN, IN_PROJ_TN), IN_PROJ_TN)
        o_ref[...] = _dot(xn, w_ref[:, cols]).astype(o_ref.dtype)
        if first:
            r_ref[...] = _dot(xn, wr_ref[...]).astype(r_ref.dtype)
        _norm_next_rows(xnext_ref, g_ref, xn_ref, 1 - slot, j, n_tiles)
        _cast_slabs(cast_src, cast_dst)

    pl.when(j == 0)(functools.partial(step, True))
    pl.when(j > 0)(functools.partial(step, False))


def _in_proj(head, x, gain, w, n, w_r, w_up, w_down, small):
    d = x.shape[1]
    rows = head.shape[0] + x.shape[0]
    tn = IN_PROJ_TN
    n_tiles = n // tn
    nr = w_r.shape[1]
    last_x = x.shape[0] // TM - 1
    steps = rows // TM * n_tiles
    cast_steps = 1 << (min(steps, MAX_CAST_STEPS).bit_length() - 1)
    cast_in, cast_out, cast_shapes = _cast_specs(w_up, w_down, small, n_tiles, cast_steps)
    proj, r, *casts = pl.pallas_call(
        functools.partial(_in_proj_kernel, n_tiles=n_tiles, n_cast=len(cast_in)),
        grid=(rows // TM, n_tiles),
        in_specs=[
            pl.BlockSpec((TM, d), lambda i, j: (0, 0)),
            pl.BlockSpec((TM, d), lambda i, j: (jnp.minimum(i, last_x), 0)),
            pl.BlockSpec((1, d), lambda i, j: (0, 0)),
            pl.BlockSpec(w.shape, lambda i, j: (0, 0), pipeline_mode=pl.Buffered(1)),
            pl.BlockSpec((d, nr), lambda i, j: (0, 0)),
        ] + cast_in,
        out_specs=[pl.BlockSpec((TM, tn), lambda i, j: (i, j)),
                   pl.BlockSpec((TM, nr), lambda i, j: (i, 0))] + cast_out,
        out_shape=[jax.ShapeDtypeStruct((rows, n), _BF16),
                   jax.ShapeDtypeStruct((rows, nr), _BF16)] + cast_shapes,
        scratch_shapes=[pltpu.VMEM((2, TM, d), _BF16)],
        compiler_params=_params("arbitrary", "arbitrary"),
        name="in_proj",
    )(head, x, gain.reshape(1, d), w, w_r, w_up, w_down, *small)
    return proj, r, casts


def _matmul_residual_kernel(a_ref, w_ref, h_ref, o_ref):
    o_ref[...] = h_ref[...] + _dot(a_ref[...], w_ref[...])


def _matmul_residual(a, w, h):
    rows, k = a.shape
    d = w.shape[1]
    return pl.pallas_call(
        _matmul_residual_kernel,
        grid=(rows // TM,),
        in_specs=[
            pl.BlockSpec((TM, k), lambda i: (i, 0)),
            pl.BlockSpec((k, d), lambda i: (0, 0)),
            pl.BlockSpec((TM, d), lambda i: (i, 0)),
        ],
        out_specs=pl.BlockSpec((TM, d), lambda i: (i, 0)),
        out_shape=jax.ShapeDtypeStruct((rows, d), _F32),
        compiler_params=_params("parallel"),
        name="matmul_residual",
    )(a, w, h)


def _mlp_kernel(slab_ref, g_ref, wu_ref, wd_ref, gf_ref, o_ref, xn_ref, res_ref, *, n_tiles, final):
    t = pl.program_id(0)
    j = t % n_tiles
    slot = (t // n_tiles) % 2
    slab = TM // n_tiles
    rows = pl.ds(pl.multiple_of(j * slab, slab), slab)

    def stage():
        x = slab_ref[...]
        res_ref[rows, :] = x
        xn_ref[slot, rows, :] = _normed(x, g_ref)

    def step(first):
        u = jnp.maximum(_dot(xn_ref[1 - slot], wu_ref[...]), 0.0)
        part = _dot((u * u).astype(_BF16), wd_ref[...])
        if first:
            o_ref[...] = res_ref[...] + part
        else:
            o_ref[...] += part
        stage()

    pl.when(t < n_tiles)(stage)
    pl.when((t >= n_tiles) & (j == 0))(functools.partial(step, True))
    pl.when((t >= n_tiles) & (j > 0))(functools.partial(step, False))

    if final:
        @pl.when((t >= n_tiles) & (j == n_tiles - 1))
        def _():
            y = o_ref[...]
            o_ref[...] = y * _rms_scale(y) * gf_ref[...]


def _mlp(h, gain, w_up, w_down, layer, final_gain=None):
    rows, d = h.shape
    _, n_tiles, _, tf = w_up.shape
    assert TM % (n_tiles * BF16_ROWS) == 0
    final = final_gain is not None
    skip = FRONT // TM if final else 0
    out_rows = rows - skip * TM
    slab = TM // n_tiles
    last_slab = rows // slab - 1
    gf = (final_gain if final else gain).reshape(1, d)
    hidden = lambda t: jnp.where(t < n_tiles, 0, t % n_tiles)
    row_tile = lambda t: jnp.maximum(t // n_tiles - 1, 0)
    return pl.pallas_call(
        functools.partial(_mlp_kernel, n_tiles=n_tiles, final=final),
        grid=(out_rows // TM * n_tiles + n_tiles,),
        in_specs=[
            pl.BlockSpec((slab, d), lambda t: (jnp.minimum(t + skip * n_tiles, last_slab), 0)),
            pl.BlockSpec((1, d), lambda t: (0, 0)),
            pl.BlockSpec((None, None, d, tf), lambda t: (layer, hidden(t), 0, 0)),
            pl.BlockSpec((None, tf, d), lambda t: (layer, hidden(t), 0)),
            pl.BlockSpec((1, d), lambda t: (0, 0)),
        ],
        out_specs=pl.BlockSpec((TM, d), lambda t: (row_tile(t), 0)),
        out_shape=jax.ShapeDtypeStruct((out_rows, d), _F32),
        scratch_shapes=[pltpu.VMEM((2, TM, d), _BF16), pltpu.VMEM((TM, d), _F32)],
        compiler_params=_params("arbitrary"),
        name="mlp_final" if final else "mlp",
    )(h, gain.reshape(1, d), w_up, w_down, gf)


def _gla_kernel(q_ref, k_ref, v_ref, r_ref, rnext_ref, wg_ref, bg_ref, o_ref, s_ref, b_ref):
    @pl.when(pl.program_id(0) == 0)
    def _():
        s_ref[...] = jnp.zeros_like(s_ref)
        b_ref[0] = jnp.zeros((GLA_CHUNK, GLA_QK), _F32)

    for ck in range(GLA_STEP_CHUNKS):
        rows = slice(ck * GLA_CHUNK, (ck + 1) * GLA_CHUNK)
        r_after = rnext_ref[...] if ck == GLA_STEP_CHUNKS - 1 else r_ref[(ck + 1) * GLA_CHUNK:(ck + 2) * GLA_CHUNK, :]
        _gla_chunk(pl.program_id(0) * GLA_STEP_CHUNKS + ck, ck % 2, rows, r_after,
                   q_ref, k_ref, v_ref, wg_ref, bg_ref, o_ref, s_ref, b_ref)


def _gla_chunk(step, slot, rows, r_after, q_ref, k_ref, v_ref, wg_ref, bg_ref, o_ref, s_ref, b_ref):
    c = GLA_CHUNK
    nsub = c // GLA_SUB

    b_all = b_ref[slot]

    causal = [lax.broadcasted_iota(jnp.int32, (GLA_SUB, c), 1)
              <= lax.broadcasted_iota(jnp.int32, (GLA_SUB, c), 0) + i * GLA_SUB for i in range(nsub)]

    def rows_of(t, i):
        return t[i * GLA_SUB:(i + 1) * GLA_SUB]

    def scaled(t, e):
        return t * jnp.exp2(e).astype(_BF16)

    q_shift = math.log2(GLA_DK ** -0.5)

    def attention_weights(h):
        ks = slice(h * GLA_DK, (h + 1) * GLA_DK)
        b = b_all[:, ks]
        beta = [jnp.zeros((1, GLA_DK), _F32)] + [b[i * GLA_SUB - 1:i * GLA_SUB, :] for i in range(1, nsub + 1)]
        q_exp = jnp.concatenate([rows_of(b, i) - beta[i] for i in range(nsub)], axis=0)
        k_exp = jnp.concatenate([beta[i + 1] - rows_of(b, i) for i in range(nsub)], axis=0)
        qs = q_ref[rows, ks] * jnp.exp2(q_exp + q_shift).astype(_BF16)
        kd = k_ref[rows, ks] * jnp.exp2(k_exp).astype(_BF16)
        a_rows = []
        for i in range(nsub):
            kt = jnp.concatenate([scaled(rows_of(kd, j), beta[i] - beta[j + 1]) for j in range(i + 1)]
                                 + [rows_of(kd, j) for j in range(i + 1, nsub)], axis=0)
            a_rows.append(jnp.where(causal[i], _dot_nt(rows_of(qs, i), kt), 0.0).astype(_BF16))
        qe = jnp.concatenate([scaled(rows_of(qs, i), beta[i]) for i in range(nsub)], axis=0)
        k_dec = jnp.concatenate([scaled(rows_of(kd, j), beta[nsub] - beta[j + 1]) for j in range(nsub)], axis=0)
        decay = jnp.exp2(jnp.broadcast_to(beta[nsub], (LANES, GLA_DK))).T
        return jnp.concatenate([jnp.concatenate(a_rows, axis=0), qe], axis=1), k_dec, decay

    def outputs(h, lhs, k_dec, decay):
        vs = slice(h * GLA_DV, (h + 1) * GLA_DV)
        vh = v_ref[rows, vs]
        state = s_ref[h]
        o_ref[rows, vs] = _dot(lhs, jnp.concatenate([vh, state.astype(_BF16)], axis=0)).astype(o_ref.dtype)
        s_ref[h] = state * jnp.concatenate([decay] * (GLA_DV // LANES), axis=1) + _dot_tn(k_dec, vh)

    def gate_logits():
        return _dot(r_after, wg_ref[...]) + bg_ref[...]

    def gate_rows(x, i):
        xs = rows_of(x, i)
        gk = (jnp.minimum(xs, 0.0) - jnp.log1p(jnp.exp(-jnp.abs(xs)))) * (LOG2E / GLA_GATE_NORM)
        row = (step + 1) * c + i * GLA_SUB + lax.broadcasted_iota(jnp.int32, (GLA_SUB, 1), 0)
        gk = jnp.where(row >= META_ROW0, gk, 0.0)
        hi = gk.astype(_BF16)
        return hi, (gk - hi.astype(_F32)).astype(_BF16)

    def cumulative(pairs):
        tri = (lax.broadcasted_iota(jnp.int32, (c, c), 0)
               >= lax.broadcasted_iota(jnp.int32, (c, c), 1)).astype(_BF16)
        hi = jnp.concatenate([p[0] for p in pairs], axis=0)
        lo = jnp.concatenate([p[1] for p in pairs], axis=0)
        return _dot(tri, hi) + _dot(tri, lo)

    assert nsub == GLA_HEADS
    x_gate = gate_logits()
    pairs = []
    pending = None
    for h in range(GLA_HEADS):
        ready = attention_weights(h)
        pairs.append(gate_rows(x_gate, h))
        if pending is not None:
            outputs(h - 1, *pending)
        pending = ready
    b_next = cumulative(pairs)
    outputs(GLA_HEADS - 1, *pending)
    b_ref[1 - slot] = b_next


def _gla(proj, r, w_gate, b_gate):
    rows = proj.shape[0]
    c = GLA_CHUNK
    nc = GLA_STEP_CHUNKS
    steps = rows // (nc * c)
    last = rows // c - 1
    return pl.pallas_call(
        _gla_kernel,
        grid=(steps,),
        in_specs=[
            pl.BlockSpec((nc * c, GLA_QK), lambda s: (s, 0)),
            pl.BlockSpec((nc * c, GLA_QK), lambda s: (s, 1)),
            pl.BlockSpec((nc * c, GLA_VD), lambda s: (s, 1)),
            pl.BlockSpec((nc * c, RANK_PAD), lambda s: (s, 0)),
            pl.BlockSpec((c, RANK_PAD), lambda s: (jnp.minimum(nc * (s + 1), last), 0)),
            pl.BlockSpec((RANK_PAD, GLA_QK), lambda s: (0, 0)),
            pl.BlockSpec((1, GLA_QK), lambda s: (0, 0)),
        ],
        out_specs=pl.BlockSpec((nc * c, GLA_VD), lambda s: (s, 0)),
        out_shape=jax.ShapeDtypeStruct((rows, GLA_VD), _BF16),
        scratch_shapes=[pltpu.VMEM((GLA_HEADS, GLA_DK, GLA_DV), _F32),
                        pltpu.VMEM((2, c, GLA_QK), _F32)],
        compiler_params=_params("arbitrary"),
        name="gla",
    )(proj, proj, proj, r, r, w_gate, b_gate.reshape(1, GLA_QK))


def _gla_out_kernel(o_ref, g_ref, no_ref, w_ref, head_ref, x_ref, out_ref):
    parts = []
    for h in range(GLA_HEADS):
        vs = slice(h * GLA_DV, (h + 1) * GLA_DV)
        o = o_ref[:, vs].astype(_F32)
        hg = 0.5 * g_ref[:, vs].astype(_F32)
        silu = hg * jnp.tanh(hg) + hg
        parts.append((o * _rms_scale(o) * no_ref[...] * silu).astype(_BF16))
    y = jnp.concatenate(parts, axis=1)
    out_ref[...] = _stream_tile(head_ref, x_ref, pl.program_id(0)) + _dot(y, w_ref[...])


def _gla_out(o, proj, norm_out, w, head, x):
    rows, k = o.shape
    d = w.shape[1]
    return pl.pallas_call(
        _gla_out_kernel,
        grid=(rows // TM,),
        in_specs=[
            pl.BlockSpec((TM, k), lambda i: (i, 0)),
            pl.BlockSpec((TM, GLA_VD), lambda i: (i, 2)),
            pl.BlockSpec((1, GLA_DV), lambda i: (0, 0)),
            pl.BlockSpec((k, d), lambda i: (0, 0)),
        ] + _stream_specs(d, 1),
        out_specs=pl.BlockSpec((TM, d), lambda i: (i, 0)),
        out_shape=jax.ShapeDtypeStruct((rows, d), _F32),
        compiler_params=_params("parallel"),
        name="gla_out",
    )(o, proj, norm_out.reshape(1, GLA_DV), w, head, x)


def _swap_halves(x, width):
    lane = lax.broadcasted_iota(jnp.int32, x.shape, 1)
    first = (lane % (2 * width)) < width
    return jnp.where(first, pltpu.roll(x, LANES - width, 1), pltpu.roll(x, width, 1))


def _qkv_kernel(h_ref, gq_ref, gkv_ref, wq_ref, wkv_ref, q_ref, k_ref, v_ref, tab_ref):
    half = SWA_HEAD_DIM // 2
    i = pl.program_id(0)

    def inv_freq(shape):
        lane = lax.broadcasted_iota(jnp.int32, shape, 1)
        return jnp.exp((lane % half).astype(_F32) * (-math.log(ROPE_THETA) / half))

    @pl.when(i == 0)
    def _():
        off = lax.broadcasted_iota(jnp.int32, (TM, LANES), 0).astype(_F32) * inv_freq((TM, LANES))
        tab_ref[0] = jnp.cos(off)
        tab_ref[1] = jnp.sin(off)

    x = h_ref[...]
    xhat = x * _rms_scale(x)
    q = _dot((xhat * gq_ref[...]).astype(_BF16), wq_ref[...])
    kv = _dot((xhat * gkv_ref[...]).astype(_BF16), wkv_ref[...])

    base = (i * TM - META_ROW0).astype(_F32) * inv_freq((8, LANES))[:1]
    cos_b, sin_b = jnp.cos(base), jnp.sin(base)
    cos = cos_b * tab_ref[0] - sin_b * tab_ref[1]
    sin = sin_b * tab_ref[0] + cos_b * tab_ref[1]
    lane = lax.broadcasted_iota(jnp.int32, (TM, LANES), 1)
    sin = jnp.where((lane % SWA_HEAD_DIM) < half, -sin, sin)

    def rope(t):
        return t * cos + _swap_halves(t, half) * sin

    qscale = SWA_HEAD_DIM ** -0.5 * LOG2E
    for p in range(SWA_Q_HEADS // 2):
        sl = slice(p * LANES, (p + 1) * LANES)
        q_ref[:, sl] = (rope(q[:, sl]) * qscale).astype(q_ref.dtype)

    low = (lane % LANES) < SWA_HEAD_DIM
    kv_width = SWA_KV_HEADS * SWA_HEAD_DIM
    for p in range(SWA_KV_HEADS // 2):
        sl = slice(p * LANES, (p + 1) * LANES)
        for t, o_ref in ((rope(kv[:, sl]), k_ref), (kv[:, kv_width + p * LANES:kv_width + (p + 1) * LANES], v_ref)):
            swapped = pltpu.roll(t, SWA_HEAD_DIM, 1)
            o_ref[:, (2 * p) * LANES:(2 * p + 1) * LANES] = jnp.where(low, t, swapped).astype(o_ref.dtype)
            o_ref[:, (2 * p + 1) * LANES:(2 * p + 2) * LANES] = jnp.where(low, swapped, t).astype(o_ref.dtype)


def _qkv(h, gain_q, gain_kv, w_q, w_kv):
    rows, d = h.shape
    nq = w_q.shape[1]
    nkv = w_kv.shape[1]
    dup = SWA_KV_HEADS * LANES
    return pl.pallas_call(
        _qkv_kernel,
        grid=(rows // TM,),
        in_specs=[
            pl.BlockSpec((TM, d), lambda i: (i, 0)),
            pl.BlockSpec((1, d), lambda i: (0, 0)),
            pl.BlockSpec((1, d), lambda i: (0, 0)),
            pl.BlockSpec((d, nq), lambda i: (0, 0)),
            pl.BlockSpec((d, nkv), lambda i: (0, 0)),
        ],
        out_specs=[
            pl.BlockSpec((TM, nq), lambda i: (i, 0)),
            pl.BlockSpec((TM, dup), lambda i: (i, 0)),
            pl.BlockSpec((TM, dup), lambda i: (i, 0)),
        ],
        out_shape=[
            jax.ShapeDtypeStruct((rows, nq), _BF16),
            jax.ShapeDtypeStruct((rows, dup), _BF16),
            jax.ShapeDtypeStruct((rows, dup), _BF16),
        ],
        scratch_shapes=[pltpu.VMEM((2, TM, LANES), _F32)],
        compiler_params=_params("arbitrary"),
        name="qkv_rope",
    )(h, gain_q.reshape(1, d), gain_kv.reshape(1, d), w_q, w_kv)


def _swa_kernel(sink_ref, q_ref, kp_ref, kc_ref, km_ref, vp_ref, vc_ref, vm_ref, o_ref, bias_ref):
    for blk in range(SWA_STEP_BLOCKS):
        _swa_block(pl.program_id(0) * SWA_STEP_BLOCKS + blk, blk, sink_ref, q_ref, kp_ref, kc_ref, km_ref,
                   vp_ref, vc_ref, vm_ref, o_ref, bias_ref)


def _swa_block(n, blk, sink_ref, q_ref, kp_ref, kc_ref, km_ref, vp_ref, vc_ref, vm_ref, o_ref, bias_ref):
    row0 = blk * SWA_BLOCK
    first_block = META_ROW0 // SWA_BLOCK
    meta_lo = META_ROW0 % SWA_BLOCK
    band_w = SWA_KEYS - N_META
    win_start = (0, 2 * N_META + 2 * SWA_BLOCK - SWA_KEYS)
    spare_col = (SWA_KEYS - 1, 0)

    @pl.when((n >= first_block) & (n <= first_block + 2))
    def _():
        qi0 = lax.broadcasted_iota(jnp.int32, (SWA_HALF, SWA_KEYS), 0)
        col = lax.broadcasted_iota(jnp.int32, (SWA_HALF, SWA_KEYS), 1)
        for hq in range(2):
            qi = qi0 + hq * SWA_HALF
            if hq == 0:
                is_meta = col < N_META
                kj = col - N_META
                m = col
            else:
                is_meta = col >= band_w
                kj = col + (2 * SWA_BLOCK - band_w)
                m = col - band_w
            band_ok = (kj > qi) & (kj <= qi + SWA_WINDOW) & ((n - 1) * SWA_BLOCK + kj >= FRONT)
            meta_ok = (META_ROW0 + m) <= n * SWA_BLOCK + qi
            ok = (is_meta & meta_ok) | (jnp.logical_not(is_meta) & band_ok)
            mask_bias = jnp.where(ok, 0.0, NEG_INF).astype(_F32)
            for g in range(SWA_KV_HEADS):
                bias_ref[2 * g + hq] = jnp.concatenate(
                    [jnp.where(col == spare_col[hq], sink_ref[g * SWA_GROUP + j] * LOG2E, mask_bias)
                     for j in range(SWA_GROUP)], axis=0)

    @pl.when(n < first_block)
    def _():
        o_ref[row0:row0 + SWA_BLOCK, :] = jnp.zeros((SWA_BLOCK, o_ref.shape[1]), o_ref.dtype)

    @pl.when(n >= first_block)
    def _():
        cur = slice(row0, row0 + SWA_BLOCK)
        k_prev = kp_ref[...] if blk == 0 else kc_ref[row0 - SWA_BLOCK:row0, :]
        v_prev = vp_ref[...] if blk == 0 else vc_ref[row0 - SWA_BLOCK:row0, :]
        kb = jnp.concatenate([km_ref[meta_lo:, :], k_prev, kc_ref[cur, :], km_ref[meta_lo:, :]], axis=0)
        vb = jnp.concatenate([vm_ref[meta_lo:, :], v_prev, vc_ref[cur, :], vm_ref[meta_lo:, :]], axis=0)
        key_row = lax.broadcasted_iota(jnp.int32, kb.shape, 0)
        spare_row = [key_row == win_start[hq] + spare_col[hq] for hq in range(2)]
        kbs = [jnp.where(spare_row[hq], jnp.zeros_like(kb), kb) for hq in range(2)]
        vbs = [jnp.where(spare_row[hq], jnp.zeros_like(vb), vb) for hq in range(2)]
        lane = lax.broadcasted_iota(jnp.int32, (SWA_HALF, LANES), 1)
        low = lane < SWA_HEAD_DIM
        high = lane >= SWA_HEAD_DIM
        ones = jnp.ones((SWA_KEYS, LANES), _BF16)

        def scores(g, hq):
            rows = slice(row0 + hq * SWA_HALF, row0 + (hq + 1) * SWA_HALF)
            parts = []
            for j in range(SWA_GROUP):
                pair = (g * SWA_GROUP + j) // 2
                t = q_ref[rows, pair * LANES:(pair + 1) * LANES]
                parts.append(jnp.where(low if j % 2 == 0 else high, t, jnp.zeros_like(t)))
            lhs = jnp.concatenate(parts, axis=0)
            start = win_start[hq]
            s = _dot_nt(lhs, kbs[hq][start:start + SWA_KEYS, g * LANES:(g + 1) * LANES]) + bias_ref[2 * g + hq]
            row_max = jnp.max(s, axis=-1, keepdims=True)
            return s, jnp.broadcast_to(row_max, (SWA_GROUP * SWA_HALF, LANES))

        def probs(s, m):
            return jnp.concatenate([jnp.exp2(s[:, c * LANES:(c + 1) * LANES] - m)
                                    for c in range(SWA_KEYS // LANES)], axis=1).astype(_BF16)

        def outputs(p, g, hq):
            start = win_start[hq]
            rows = slice(row0 + hq * SWA_HALF, row0 + (hq + 1) * SWA_HALF)
            vw = jnp.concatenate([vbs[hq][start:start + SWA_KEYS, g * LANES:(g + 1) * LANES], ones], axis=1)
            od = _dot(p, vw)
            o = od[:, :LANES] / od[:, LANES:]
            for j in range(0, SWA_GROUP, 2):
                pair = (g * SWA_GROUP + j) // 2
                even = o[j * SWA_HALF:(j + 1) * SWA_HALF]
                odd = o[(j + 1) * SWA_HALF:(j + 2) * SWA_HALF]
                o_ref[rows, pair * LANES:(pair + 1) * LANES] = jnp.where(low, even, odd).astype(o_ref.dtype)

        tiles = [(g, hq) for g in range(SWA_KV_HEADS) for hq in range(2)]
        stage_a, stage_b = {}, {}
        for t in range(len(tiles) + 2):
            if t < len(tiles):
                stage_a[t] = scores(*tiles[t])
            if 0 <= t - 1 < len(tiles):
                stage_b[t - 1] = probs(*stage_a.pop(t - 1))
            if 0 <= t - 2 < len(tiles):
                outputs(stage_b.pop(t - 2), *tiles[t - 2])


def _swa(q, k_dup, v_dup, sinks):
    rows, nq = q.shape
    nb = SWA_STEP_BLOCKS
    steps = rows // (nb * SWA_BLOCK)
    dup = k_dup.shape[1]
    first_block = META_ROW0 // SWA_BLOCK
    blk = pl.BlockSpec((nb * SWA_BLOCK, dup), lambda n: (n, 0))
    prev = pl.BlockSpec((SWA_BLOCK, dup), lambda n: (jnp.maximum(nb * n - 1, 0), 0))
    meta = pl.BlockSpec((SWA_BLOCK, dup), lambda n: (first_block, 0))
    return pl.pallas_call(
        _swa_kernel,
        grid=(steps,),
        in_specs=[
            pl.BlockSpec(memory_space=pltpu.SMEM),
            pl.BlockSpec((nb * SWA_BLOCK, nq), lambda n: (n, 0)),
            prev, blk, meta, prev, blk, meta,
        ],
        out_specs=pl.BlockSpec((nb * SWA_BLOCK, nq), lambda n: (n, 0)),
        out_shape=jax.ShapeDtypeStruct((rows, nq), _BF16),
        scratch_shapes=[pltpu.VMEM((2 * SWA_KV_HEADS, SWA_GROUP * SWA_HALF, SWA_KEYS), _F32)],
        compiler_params=_params("arbitrary"),
        name="swa",
    )(sinks, q, k_dup, k_dup, k_dup, v_dup, v_dup, v_dup)


def kernel(x, meta_tokens, norm_mix, norm_mlp, w_mlp_up, w_mlp_down, a_w_in, a_w_gate_up, a_b_gate, a_norm_out, a_w_out, kv_norm, w_kv, b_w_q, b_sinks, b_w_out, norm_final):
    batch, seq, d = x.shape
    assert batch == 1 and d == D_MODEL and seq % TM == 0
    assert norm_mix.shape[0] == 2 and a_w_in.shape[0] == 1 and b_w_q.shape[0] == 1

    head = jnp.concatenate([jnp.zeros((META_ROW0, d), x.dtype), meta_tokens.astype(x.dtype)], axis=0)
    x2 = x[0]
    bf = lambda w: w.astype(_BF16)
    n_main = GLA_IN - GLA_RANK
    w_in = bf(a_w_in[0])
    w_r = jnp.pad(w_in[:, n_main:], ((0, 0), (0, RANK_PAD - GLA_RANK)))
    w_gate = jnp.pad(bf(a_w_gate_up[0]), ((0, RANK_PAD - GLA_RANK), (0, 0)))

    proj, r, (w_up, w_down, w_out_a, w_q, w_out_b, w_kv_b) = _in_proj(
        head, x2, norm_mix[0], w_in, n_main, w_r, w_mlp_up, w_mlp_down,
        (a_w_out[0], b_w_q[0], b_w_out[0], w_kv))
    o = _gla(proj, r, w_gate, a_b_gate[0])
    h = _gla_out(o, proj, a_norm_out[0], w_out_a, head, x2)
    h = _mlp(h, norm_mlp[0], w_up, w_down, 0)

    q, k_dup, v_dup = _qkv(h, norm_mix[1], kv_norm, w_q, w_kv_b)
    o = _swa(q, k_dup, v_dup, b_sinks[0])
    h = _matmul_residual(o, w_out_b, h)
    return _mlp(h, norm_mlp[1], w_up, w_down, 1, final_gain=norm_final)[None]
```

```python
import functools
import math

import jax
import jax.numpy as jnp
from jax import lax
from jax.experimental import pallas as pl
from jax.experimental.pallas import tpu as pltpu

D_MODEL = 2048
N_META = 16
D_FF = 4 * D_MODEL
RMS_EPS = 1e-6
NEG_INF = -1e30

GLA_HEADS = 4
GLA_DK = 256
GLA_DV = 512
GLA_QK = GLA_HEADS * GLA_DK
GLA_VD = GLA_HEADS * GLA_DV
GLA_RANK = 16
GLA_GATE_NORM = 16.0
GLA_IN = 2 * GLA_QK + 2 * GLA_VD + GLA_RANK

SWA_HEAD_DIM = 64
SWA_Q_HEADS = 32
SWA_KV_HEADS = 4
SWA_GROUP = SWA_Q_HEADS // SWA_KV_HEADS
SWA_WINDOW = 128
ROPE_THETA = 10000.0

LANES = 128
BF16_ROWS = 16
FRONT = 512
META_ROW0 = FRONT - N_META
TM = 512
IN_PROJ_TN = 1536
MLP_TF = 2048
GLA_CHUNK = 256
GLA_STEP_CHUNKS = 2
GLA_SUB = 64
SWA_BLOCK = 128
SWA_STEP_BLOCKS = 4
SWA_HALF = SWA_BLOCK // 2
SWA_KEYS = 2 * SWA_BLOCK
LOG2E = math.log2(math.e)
RANK_PAD = LANES
MAX_CAST_STEPS = 128
VMEM_LIMIT = 58 * 1024 * 1024

_F32 = jnp.float32
_BF16 = jnp.bfloat16


def _dot(a, b):
    return jnp.dot(a, b, preferred_element_type=_F32)


def _dot_nt(a, b):
    return lax.dot_general(a, b, (((1,), (1,)), ((), ())), preferred_element_type=_F32)


def _dot_tn(a, b):
    return lax.dot_general(a, b, (((0,), (0,)), ((), ())), preferred_element_type=_F32)


def _params(*semantics):
    return pltpu.CompilerParams(dimension_semantics=semantics, vmem_limit_bytes=VMEM_LIMIT)


def _rms_scale(x):
    return lax.rsqrt(jnp.mean(x * x, axis=-1, keepdims=True) + RMS_EPS)


def _normed(x, g_ref):
    return (x * _rms_scale(x) * g_ref[...]).astype(_BF16)


def _norm_next_rows(src_ref, g_ref, xn_ref, slot, j, n_steps):
    slab = -(-TM // (n_steps * BF16_ROWS)) * BF16_ROWS
    start = pl.multiple_of(jnp.minimum(j * slab, TM - slab), BF16_ROWS)
    xn_ref[slot, pl.ds(start, slab), :] = _normed(src_ref[pl.ds(start, slab), :], g_ref)


def _stream_tile(head_ref, x_ref, i):
    return jnp.where(i == 0, head_ref[...], x_ref[...])


def _stream_specs(d, grid_rank):
    if grid_rank == 1:
        return [pl.BlockSpec((TM, d), lambda i: (0, 0)),
                pl.BlockSpec((TM, d), lambda i: (jnp.maximum(i - 1, 0), 0))]
    return [pl.BlockSpec((TM, d), lambda i, j: (0, 0)),
            pl.BlockSpec((TM, d), lambda i, j: (jnp.maximum(i - 1, 0), 0))]


def _cast_slabs(src_refs, dst_refs):
    (wu_src, wd_src, *small_src), (wu, wd, *small) = src_refs, dst_refs
    for t in range(wu.shape[0]):
        wu[t] = wu_src[:, t * MLP_TF:(t + 1) * MLP_TF].astype(_BF16)
    wd[...] = wd_src[...].astype(_BF16)
    for src, dst in zip(small_src, small):
        dst[...] = src[...].astype(_BF16)


def _cast_specs(w_up, w_down, small, n_tiles, cast_steps):
    layers, d, ff = w_up.shape
    per_layer = cast_steps // layers
    ru, rd = d // per_layer, ff // per_layer
    step = lambda i, j: jnp.minimum(i * n_tiles + j, cast_steps - 1)
    layer = lambda i, j: step(i, j) // per_layer
    slab = lambda i, j: step(i, j) % per_layer
    in_specs = [pl.BlockSpec((None, ru, ff), lambda i, j: (layer(i, j), slab(i, j), 0)),
                pl.BlockSpec((None, rd, d), lambda i, j: (layer(i, j), slab(i, j), 0))]
    out_specs = [pl.BlockSpec((None, ff // MLP_TF, ru, MLP_TF), lambda i, j: (layer(i, j), 0, slab(i, j), 0)),
                 pl.BlockSpec((None, rd, d), lambda i, j: (layer(i, j), slab(i, j), 0))]
    out_shapes = [jax.ShapeDtypeStruct((layers, ff // MLP_TF, d, MLP_TF), _BF16),
                  jax.ShapeDtypeStruct((layers, ff, d), _BF16)]
    for w in small:
        r, c = w.shape
        spec = pl.BlockSpec((r // cast_steps, c), lambda i, j: (step(i, j), 0))
        in_specs.append(spec)
        out_specs.append(spec)
        out_shapes.append(jax.ShapeDtypeStruct((r, c), _BF16))
    return in_specs, out_specs, out_shapes


def _in_proj_kernel(*refs, n_tiles, n_cast):
    head_ref, xnext_ref, g_ref, w_ref, wr_ref = refs[:5]
    cast_src = refs[5:5 + n_cast]
    o_ref, r_ref = refs[5 + n_cast:7 + n_cast]
    cast_dst = refs[7 + n_cast:7 + 2 * n_cast]
    xn_ref = refs[-1]
    i, j = pl.program_id(0), pl.program_id(1)
    slot = i % 2

    @pl.when((i == 0) & (j == 0))
    def _():
        xn_ref[0] = _normed(head_ref[...], g_ref)

    def step(first):
        xn = xn_ref[slot]
        cols = pl.ds(pl.multiple_of(j * IN_PROJ_TN, IN_PROJ_TN), IN_PROJ_TN)
        o_ref[...] = _dot(xn, w_ref[:, cols]).astype(o_ref.dtype)
        if first:
            r_ref[...] = _dot(xn, wr_ref[...]).astype(r_ref.dtype)
        _norm_next_rows(xnext_ref, g_ref, xn_ref, 1 - slot, j, n_tiles)
        _cast_slabs(cast_src, cast_dst)

    pl.when(j == 0)(functools.partial(step, True))
    pl.when(j > 0)(functools.partial(step, False))


def _in_proj(head, x, gain, w, n, w_r, w_up, w_down, small):
    d = x.shape[1]
    rows = head.shape[0] + x.shape[0]
    tn = IN_PROJ_TN
    n_tiles = n // tn
    nr = w_r.shape[1]
    last_x = x.shape[0] // TM - 1
    steps = rows // TM * n_tiles
    cast_steps = 1 << (min(steps, MAX_CAST_STEPS).bit_length() - 1)
    cast_in, cast_out, cast_shapes = _cast_specs(w_up, w_down, small, n_tiles, cast_steps)
    proj, r, *casts = pl.pallas_call(
        functools.partial(_in_proj_kernel, n_tiles=n_tiles, n_cast=len(cast_in)),
        grid=(rows // TM, n_tiles),
        in_specs=[
            pl.BlockSpec((TM, d), lambda i, j: (0, 0)),
            pl.BlockSpec((TM, d), lambda i, j: (jnp.minimum(i, last_x), 0)),
            pl.BlockSpec((1, d), lambda i, j: (0, 0)),
            pl.BlockSpec(w.shape, lambda i, j: (0, 0), pipeline_mode=pl.Buffered(1)),
            pl.BlockSpec((d, nr), lambda i, j: (0, 0)),
        ] + cast_in,
        out_specs=[pl.BlockSpec((TM, tn), lambda i, j: (i, j)),
                   pl.BlockSpec((TM, nr), lambda i, j: (i, 0))] + cast_out,
        out_shape=[jax.ShapeDtypeStruct((rows, n), _BF16),
                   jax.ShapeDtypeStruct((rows, nr), _BF16)] + cast_shapes,
        scratch_shapes=[pltpu.VMEM((2, TM, d), _BF16)],
        compiler_params=_params("arbitrary", "arbitrary"),
        name="in_proj",
    )(head, x, gain.reshape(1, d), w, w_r, w_up, w_down, *small)
    return proj, r, casts


def _matmul_residual_kernel(a_ref, w_ref, h_ref, o_ref):
    o_ref[...] = h_ref[...] + _dot(a_ref[...], w_ref[...])


def _matmul_residual(a, w, h):
    rows, k = a.shape
    d = w.shape[1]
    return pl.pallas_call(
        _matmul_residual_kernel,
        grid=(rows // TM,),
        in_specs=[
            pl.BlockSpec((TM, k), lambda i: (i, 0)),
            pl.BlockSpec((k, d), lambda i: (0, 0)),
            pl.BlockSpec((TM, d), lambda i: (i, 0)),
        ],
        out_specs=pl.BlockSpec((TM, d), lambda i: (i, 0)),
        out_shape=jax.ShapeDtypeStruct((rows, d), _F32),
        compiler_params=_params("parallel"),
        name="matmul_residual",
    )(a, w, h)


def _mlp_kernel(slab_ref, g_ref, wu_ref, wd_ref, gf_ref, o_ref, xn_ref, res_ref, *, n_tiles, final):
    t = pl.program_id(0)
    j = t % n_tiles
    slot = (t // n_tiles) % 2
    slab = TM // n_tiles
    rows = pl.ds(pl.multiple_of(j * slab, slab), slab)

    def stage():
        x = slab_ref[...]
        res_ref[rows, :] = x
        xn_ref[slot, rows, :] = _normed(x, g_ref)

    def step(first):
        u = jnp.maximum(_dot(xn_ref[1 - slot], wu_ref[...]), 0.0)
        part = _dot((u * u).astype(_BF16), wd_ref[...])
        if first:
            o_ref[...] = res_ref[...] + part
        else:
            o_ref[...] += part
        stage()

    pl.when(t < n_tiles)(stage)
    pl.when((t >= n_tiles) & (j == 0))(functools.partial(step, True))
    pl.when((t >= n_tiles) & (j > 0))(functools.partial(step, False))

    if final:
        @pl.when((t >= n_tiles) & (j == n_tiles - 1))
        def _():
            y = o_ref[...]
            o_ref[...] = y * _rms_scale(y) * gf_ref[...]


def _mlp(h, gain, w_up, w_down, layer, final_gain=None):
    rows, d = h.shape
    _, n_tiles, _, tf = w_up.shape
    assert TM % (n_tiles * BF16_ROWS) == 0
    final = final_gain is not None
    skip = FRONT // TM if final else 0
    out_rows = rows - skip * TM
    slab = TM // n_tiles
    last_slab = rows // slab - 1
    gf = (final_gain if final else gain).reshape(1, d)
    hidden = lambda t: jnp.where(t < n_tiles, 0, t % n_tiles)
    row_tile = lambda t: jnp.maximum(t // n_tiles - 1, 0)
    return pl.pallas_call(
        functools.partial(_mlp_kernel, n_tiles=n_tiles, final=final),
        grid=(out_rows // TM * n_tiles + n_tiles,),
        in_specs=[
            pl.BlockSpec((slab, d), lambda t: (jnp.minimum(t + skip * n_tiles, last_slab), 0)),
            pl.BlockSpec((1, d), lambda t: (0, 0)),
            pl.BlockSpec((None, None, d, tf), lambda t: (layer, hidden(t), 0, 0)),
            pl.BlockSpec((None, tf, d), lambda t: (layer, hidden(t), 0)),
            pl.BlockSpec((1, d), lambda t: (0, 0)),
        ],
        out_specs=pl.BlockSpec((TM, d), lambda t: (row_tile(t), 0)),
        out_shape=jax.ShapeDtypeStruct((out_rows, d), _F32),
        scratch_shapes=[pltpu.VMEM((2, TM, d), _BF16), pltpu.VMEM((TM, d), _F32)],
        compiler_params=_params("arbitrary"),
        name="mlp_final" if final else "mlp",
    )(h, gain.reshape(1, d), w_up, w_down, gf)


def _gla_kernel(q_ref, k_ref, v_ref, r_ref, rnext_ref, wg_ref, bg_ref, o_ref, s_ref, b_ref):
    @pl.when(pl.program_id(0) == 0)
    def _():
        s_ref[...] = jnp.zeros_like(s_ref)
        b_ref[0] = jnp.zeros((GLA_CHUNK, GLA_QK), _F32)

    for ck in range(GLA_STEP_CHUNKS):
        rows = slice(ck * GLA_CHUNK, (ck + 1) * GLA_CHUNK)
        r_after = rnext_ref[...] if ck == GLA_STEP_CHUNKS - 1 else r_ref[(ck + 1) * GLA_CHUNK:(ck + 2) * GLA_CHUNK, :]
        _gla_chunk(pl.program_id(0) * GLA_STEP_CHUNKS + ck, ck % 2, rows, r_after,
                   q_ref, k_ref, v_ref, wg_ref, bg_ref, o_ref, s_ref, b_ref)


def _gla_chunk(step, slot, rows, r_after, q_ref, k_ref, v_ref, wg_ref, bg_ref, o_ref, s_ref, b_ref):
    c = GLA_CHUNK
    nsub = c // GLA_SUB

    b_all = b_ref[slot]

    causal = [lax.broadcasted_iota(jnp.int32, (GLA_SUB, c), 1)
              <= lax.broadcasted_iota(jnp.int32, (GLA_SUB, c), 0) + i * GLA_SUB for i in range(nsub)]

    def rows_of(t, i):
        return t[i * GLA_SUB:(i + 1) * GLA_SUB]

    def scaled(t, e):
        return t * jnp.exp2(e).astype(_BF16)

    q_shift = math.log2(GLA_DK ** -0.5)

    def attention_weights(h):
        ks = slice(h * GLA_DK, (h + 1) * GLA_DK)
        b = b_all[:, ks]
        beta = [jnp.zeros((1, GLA_DK), _F32)] + [b[i * GLA_SUB - 1:i * GLA_SUB, :] for i in range(1, nsub + 1)]
        q_exp = jnp.concatenate([rows_of(b, i) - beta[i] for i in range(nsub)], axis=0)
        k_exp = jnp.concatenate([beta[i + 1] - rows_of(b, i) for i in range(nsub)], axis=0)
        qs = q_ref[rows, ks] * jnp.exp2(q_exp + q_shift).astype(_BF16)
        kd = k_ref[rows, ks] * jnp.exp2(k_exp).astype(_BF16)
        a_rows = []
        for i in range(nsub):
            kt = jnp.concatenate([scaled(rows_of(kd, j), beta[i] - beta[j + 1]) for j in range(i + 1)]
                                 + [rows_of(kd, j) for j in range(i + 1, nsub)], axis=0)
            a_rows.append(jnp.where(causal[i], _dot_nt(rows_of(qs, i), kt), 0.0).astype(_BF16))
        qe = jnp.concatenate([scaled(rows_of(qs, i), beta[i]) for i in range(nsub)], axis=0)
        k_dec = jnp.concatenate([scaled(rows_of(kd, j), beta[nsub] - beta[j + 1]) for j in range(nsub)], axis=0)
        decay = jnp.exp2(jnp.broadcast_to(beta[nsub], (LANES, GLA_DK))).T
        return jnp.concatenate([jnp.concatenate(a_rows, axis=0), qe], axis=1), k_dec, decay

    def outputs(h, lhs, k_dec, decay):
        vs = slice(h * GLA_DV, (h + 1) * GLA_DV)
        vh = v_ref[rows, vs]
        state = s_ref[h]
        o_ref[rows, vs] = _dot(lhs, jnp.concatenate([vh, state.astype(_BF16)], axis=0)).astype(o_ref.dtype)
        s_ref[h] = state * jnp.concatenate([decay] * (GLA_DV // LANES), axis=1) + _dot_tn(k_dec, vh)

    def gate_logits():
        return _dot(r_after, wg_ref[...]) + bg_ref[...]

    def gate_rows(x, i):
        xs = rows_of(x, i)
        gk = (jnp.minimum(xs, 0.0) - jnp.log1p(jnp.exp(-jnp.abs(xs)))) * (LOG2E / GLA_GATE_NORM)
        row = (step + 1) * c + i * GLA_SUB + lax.broadcasted_iota(jnp.int32, (GLA_SUB, 1), 0)
        gk = jnp.where(row >= META_ROW0, gk, 0.0)
        hi = gk.astype(_BF16)
        return hi, (gk - hi.astype(_F32)).astype(_BF16)

    def cumulative(pairs):
        tri = (lax.broadcasted_iota(jnp.int32, (c, c), 0)
               >= lax.broadcasted_iota(jnp.int32, (c, c), 1)).astype(_BF16)
        hi = jnp.concatenate([p[0] for p in pairs], axis=0)
        lo = jnp.concatenate([p[1] for p in pairs], axis=0)
        return _dot(tri, hi) + _dot(tri, lo)

    assert nsub == GLA_HEADS
    x_gate = gate_logits()
    pairs = []
    pending = None
    for h in range(GLA_HEADS):
        ready = attention_weights(h)
        pairs.append(gate_rows(x_gate, h))
        if pending is not None:
            outputs(h - 1, *pending)
        pending = ready
    b_next = cumulative(pairs)
    outputs(GLA_HEADS - 1, *pending)
    b_ref[1 - slot] = b_next


def _gla(proj, r, w_gate, b_gate):
    rows = proj.shape[0]
    c = GLA_CHUNK
    nc = GLA_STEP_CHUNKS
    steps = rows // (nc * c)
    last = rows // c - 1
    return pl.pallas_call(
        _gla_kernel,
        grid=(steps,),
        in_specs=[
            pl.BlockSpec((nc * c, GLA_QK), lambda s: (s, 0)),
            pl.BlockSpec((nc * c, GLA_QK), lambda s: (s, 1)),
            pl.BlockSpec((nc * c, GLA_VD), lambda s: (s, 1)),
            pl.BlockSpec((nc * c, RANK_PAD), lambda s: (s, 0)),
            pl.BlockSpec((c, RANK_PAD), lambda s: (jnp.minimum(nc * (s + 1), last), 0)),
            pl.BlockSpec((RANK_PAD, GLA_QK), lambda s: (0, 0)),
            pl.BlockSpec((1, GLA_QK), lambda s: (0, 0)),
        ],
        out_specs=pl.BlockSpec((nc * c, GLA_VD), lambda s: (s, 0)),
        out_shape=jax.ShapeDtypeStruct((rows, GLA_VD), _BF16),
        scratch_shapes=[pltpu.VMEM((GLA_HEADS, GLA_DK, GLA_DV), _F32),
                        pltpu.VMEM((2, c, GLA_QK), _F32)],
        compiler_params=_params("arbitrary"),
        name="gla",
    )(proj, proj, proj, r, r, w_gate, b_gate.reshape(1, GLA_QK))


def _gla_out_kernel(o_ref, g_ref, no_ref, w_ref, head_ref, x_ref, out_ref):
    parts = []
    for h in range(GLA_HEADS):
        vs = slice(h * GLA_DV, (h + 1) * GLA_DV)
        o = o_ref[:, vs].astype(_F32)
        hg = 0.5 * g_ref[:, vs].astype(_F32)
        silu = hg * jnp.tanh(hg) + hg
        parts.append((o * _rms_scale(o) * no_ref[...] * silu).astype(_BF16))
    y = jnp.concatenate(parts, axis=1)
    out_ref[...] = _stream_tile(head_ref, x_ref, pl.program_id(0)) + _dot(y, w_ref[...])


def _gla_out(o, proj, norm_out, w, head, x):
    rows, k = o.shape
    d = w.shape[1]
    return pl.pallas_call(
        _gla_out_kernel,
        grid=(rows // TM,),
        in_specs=[
            pl.BlockSpec((TM, k), lambda i: (i, 0)),
            pl.BlockSpec((TM, GLA_VD), lambda i: (i, 2)),
            pl.BlockSpec((1, GLA_DV), lambda i: (0, 0)),
            pl.BlockSpec((k, d), lambda i: (0, 0)),
        ] + _stream_specs(d, 1),
        out_specs=pl.BlockSpec((TM, d), lambda i: (i, 0)),
        out_shape=jax.ShapeDtypeStruct((rows, d), _F32),
        compiler_params=_params("parallel"),
        name="gla_out",
    )(o, proj, norm_out.reshape(1, GLA_DV), w, head, x)


def _swap_halves(x, width):
    lane = lax.broadcasted_iota(jnp.int32, x.shape, 1)
    first = (lane % (2 * width)) < width
    return jnp.where(first, pltpu.roll(x, LANES - width, 1), pltpu.roll(x, width, 1))


def _qkv_kernel(h_ref, gq_ref, gkv_ref, wq_ref, wkv_ref, q_ref, k_ref, v_ref, tab_ref):
    half = SWA_HEAD_DIM // 2
    i = pl.program_id(0)

    def inv_freq(shape):
        lane = lax.broadcasted_iota(jnp.int32, shape, 1)
        return jnp.exp((lane % half).astype(_F32) * (-math.log(ROPE_THETA) / half))

    @pl.when(i == 0)
    def _():
        off = lax.broadcasted_iota(jnp.int32, (TM, LANES), 0).astype(_F32) * inv_freq((TM, LANES))
        tab_ref[0] = jnp.cos(off)
        tab_ref[1] = jnp.sin(off)

    x = h_ref[...]
    xhat = x * _rms_scale(x)
    q = _dot((xhat * gq_ref[...]).astype(_BF16), wq_ref[...])
    kv = _dot((xhat * gkv_ref[...]).astype(_BF16), wkv_ref[...])

    base = (i * TM - META_ROW0).astype(_F32) * inv_freq((8, LANES))[:1]
    cos_b, sin_b = jnp.cos(base), jnp.sin(base)
    cos = cos_b * tab_ref[0] - sin_b * tab_ref[1]
    sin = sin_b * tab_ref[0] + cos_b * tab_ref[1]
    lane = lax.broadcasted_iota(jnp.int32, (TM, LANES), 1)
    sin = jnp.where((lane % SWA_HEAD_DIM) < half, -sin, sin)

    def rope(t):
        return t * cos + _swap_halves(t, half) * sin

    qscale = SWA_HEAD_DIM ** -0.5 * LOG2E
    for p in range(SWA_Q_HEADS // 2):
        sl = slice(p * LANES, (p + 1) * LANES)
        q_ref[:, sl] = (rope(q[:, sl]) * qscale).astype(q_ref.dtype)

    low = (lane % LANES) < SWA_HEAD_DIM
    kv_width = SWA_KV_HEADS * SWA_HEAD_DIM
    for p in range(SWA_KV_HEADS // 2):
        sl = slice(p * LANES, (p + 1) * LANES)
        for t, o_ref in ((rope(kv[:, sl]), k_ref), (kv[:, kv_width + p * LANES:kv_width + (p + 1) * LANES], v_ref)):
            swapped = pltpu.roll(t, SWA_HEAD_DIM, 1)
            o_ref[:, (2 * p) * LANES:(2 * p + 1) * LANES] = jnp.where(low, t, swapped).astype(o_ref.dtype)
            o_ref[:, (2 * p + 1) * LANES:(2 * p + 2) * LANES] = jnp.where(low, swapped, t).astype(o_ref.dtype)


def _qkv(h, gain_q, gain_kv, w_q, w_kv):
    rows, d = h.shape
    nq = w_q.shape[1]
    nkv = w_kv.shape[1]
    dup = SWA_KV_HEADS * LANES
    return pl.pallas_call(
        _qkv_kernel,
        grid=(rows // TM,),
        in_specs=[
            pl.BlockSpec((TM, d), lambda i: (i, 0)),
            pl.BlockSpec((1, d), lambda i: (0, 0)),
            pl.BlockSpec((1, d), lambda i: (0, 0)),
            pl.BlockSpec((d, nq), lambda i: (0, 0)),
            pl.BlockSpec((d, nkv), lambda i: (0, 0)),
        ],
        out_specs=[
            pl.BlockSpec((TM, nq), lambda i: (i, 0)),
            pl.BlockSpec((TM, dup), lambda i: (i, 0)),
            pl.BlockSpec((TM, dup), lambda i: (i, 0)),
        ],
        out_shape=[
            jax.ShapeDtypeStruct((rows, nq), _BF16),
            jax.ShapeDtypeStruct((rows, dup), _BF16),
            jax.ShapeDtypeStruct((rows, dup), _BF16),
        ],
        scratch_shapes=[pltpu.VMEM((2, TM, LANES), _F32)],
        compiler_params=_params("arbitrary"),
        name="qkv_rope",
    )(h, gain_q.reshape(1, d), gain_kv.reshape(1, d), w_q, w_kv)


def _swa_kernel(sink_ref, q_ref, kp_ref, kc_ref, km_ref, vp_ref, vc_ref, vm_ref, o_ref, bias_ref):
    for blk in range(SWA_STEP_BLOCKS):
        _swa_block(pl.program_id(0) * SWA_STEP_BLOCKS + blk, blk, sink_ref, q_ref, kp_ref, kc_ref, km_ref,
                   vp_ref, vc_ref, vm_ref, o_ref, bias_ref)


def _swa_block(n, blk, sink_ref, q_ref, kp_ref, kc_ref, km_ref, vp_ref, vc_ref, vm_ref, o_ref, bias_ref):
    row0 = blk * SWA_BLOCK
    first_block = META_ROW0 // SWA_BLOCK
    meta_lo = META_ROW0 % SWA_BLOCK
    band_w = SWA_KEYS - N_META
    win_start = (0, 2 * N_META + 2 * SWA_BLOCK - SWA_KEYS)
    spare_col = (SWA_KEYS - 1, 0)

    @pl.when((n >= first_block) & (n <= first_block + 2))
    def _():
        qi0 = lax.broadcasted_iota(jnp.int32, (SWA_HALF, SWA_KEYS), 0)
        col = lax.broadcasted_iota(jnp.int32, (SWA_HALF, SWA_KEYS), 1)
        for hq in range(2):
            qi = qi0 + hq * SWA_HALF
            if hq == 0:
                is_meta = col < N_META
                kj = col - N_META
                m = col
            else:
                is_meta = col >= band_w
                kj = col + (2 * SWA_BLOCK - band_w)
                m = col - band_w
            band_ok = (kj > qi) & (kj <= qi + SWA_WINDOW) & ((n - 1) * SWA_BLOCK + kj >= FRONT)
            meta_ok = (META_ROW0 + m) <= n * SWA_BLOCK + qi
            ok = (is_meta & meta_ok) | (jnp.logical_not(is_meta) & band_ok)
            mask_bias = jnp.where(ok, 0.0, NEG_INF).astype(_F32)
            for g in range(SWA_KV_HEADS):
                bias_ref[2 * g + hq] = jnp.concatenate(
                    [jnp.where(col == spare_col[hq], sink_ref[g * SWA_GROUP + j] * LOG2E, mask_bias)
                     for j in range(SWA_GROUP)], axis=0)

    @pl.when(n < first_block)
    def _():
        o_ref[row0:row0 + SWA_BLOCK, :] = jnp.zeros((SWA_BLOCK, o_ref.shape[1]), o_ref.dtype)

    @pl.when(n >= first_block)
    def _():
        cur = slice(row0, row0 + SWA_BLOCK)
        k_prev = kp_ref[...] if blk == 0 else kc_ref[row0 - SWA_BLOCK:row0, :]
        v_prev = vp_ref[...] if blk == 0 else vc_ref[row0 - SWA_BLOCK:row0, :]
        kb = jnp.concatenate([km_ref[meta_lo:, :], k_prev, kc_ref[cur, :], km_ref[meta_lo:, :]], axis=0)
        vb = jnp.concatenate([vm_ref[meta_lo:, :], v_prev, vc_ref[cur, :], vm_ref[meta_lo:, :]], axis=0)
        key_row = lax.broadcasted_iota(jnp.int32, kb.shape, 0)
        spare_row = [key_row == win_start[hq] + spare_col[hq] for hq in range(2)]
        kbs = [jnp.where(spare_row[hq], jnp.zeros_like(kb), kb) for hq in range(2)]
        vbs = [jnp.where(spare_row[hq], jnp.zeros_like(vb), vb) for hq in range(2)]
        lane = lax.broadcasted_iota(jnp.int32, (SWA_HALF, LANES), 1)
        low = lane < SWA_HEAD_DIM
        high = lane >= SWA_HEAD_DIM
        ones = jnp.ones((SWA_KEYS, LANES), _BF16)

        def scores(g, hq):
            rows = slice(row0 + hq * SWA_HALF, row0 + (hq + 1) * SWA_HALF)
            parts = []
            for j in range(SWA_GROUP):
                pair = (g * SWA_GROUP + j) // 2
                t = q_ref[rows, pair * LANES:(pair + 1) * LANES]
                parts.append(jnp.where(low if j % 2 == 0 else high, t, jnp.zeros_like(t)))
            lhs = jnp.concatenate(parts, axis=0)
            start = win_start[hq]
            s = _dot_nt(lhs, kbs[hq][start:start + SWA_KEYS, g * LANES:(g + 1) * LANES]) + bias_ref[2 * g + hq]
            row_max = jnp.max(s, axis=-1, keepdims=True)
            return s, jnp.broadcast_to(row_max, (SWA_GROUP * SWA_HALF, LANES))

        def probs(s, m):
            return jnp.concatenate([jnp.exp2(s[:, c * LANES:(c + 1) * LANES] - m)
                                    for c in range(SWA_KEYS // LANES)], axis=1).astype(_BF16)

        def outputs(p, g, hq):
            start = win_start[hq]
            rows = slice(row0 + hq * SWA_HALF, row0 + (hq + 1) * SWA_HALF)
            vw = jnp.concatenate([vbs[hq][start:start + SWA_KEYS, g * LANES:(g + 1) * LANES], ones], axis=1)
            od = _dot(p, vw)
            o = od[:, :LANES] / od[:, LANES:]
            for j in range(0, SWA_GROUP, 2):
                pair = (g * SWA_GROUP + j) // 2
                even = o[j * SWA_HALF:(j + 1) * SWA_HALF]
                odd = o[(j + 1) * SWA_HALF:(j + 2) * SWA_HALF]
                o_ref[rows, pair * LANES:(pair + 1) * LANES] = jnp.where(low, even, odd).astype(o_ref.dtype)

        tiles = [(g, hq) for g in range(SWA_KV_HEADS) for hq in range(2)]
        stage_a, stage_b = {}, {}
        for t in range(len(tiles) + 2):
            if t < len(tiles):
                stage_a[t] = scores(*tiles[t])
            if 0 <= t - 1 < len(tiles):
                stage_b[t - 1] = probs(*stage_a.pop(t - 1))
            if 0 <= t - 2 < len(tiles):
                outputs(stage_b.pop(t - 2), *tiles[t - 2])


def _swa(q, k_dup, v_dup, sinks):
    rows, nq = q.shape
    nb = SWA_STEP_BLOCKS
    steps = rows // (nb * SWA_BLOCK)
    dup = k_dup.shape[1]
    first_block = META_ROW0 // SWA_BLOCK
    blk = pl.BlockSpec((nb * SWA_BLOCK, dup), lambda n: (n, 0))
    prev = pl.BlockSpec((SWA_BLOCK, dup), lambda n: (jnp.maximum(nb * n - 1, 0), 0))
    meta = pl.BlockSpec((SWA_BLOCK, dup), lambda n: (first_block, 0))
    return pl.pallas_call(
        _swa_kernel,
        grid=(steps,),
        in_specs=[
            pl.BlockSpec(memory_space=pltpu.SMEM),
            pl.BlockSpec((nb * SWA_BLOCK, nq), lambda n: (n, 0)),
            prev, blk, meta, prev, blk, meta,
        ],
        out_specs=pl.BlockSpec((nb * SWA_BLOCK, nq), lambda n: (n, 0)),
        out_shape=jax.ShapeDtypeStruct((rows, nq), _BF16),
        scratch_shapes=[pltpu.VMEM((2 * SWA_KV_HEADS, SWA_GROUP * SWA_HALF, SWA_KEYS), _F32)],
        compiler_params=_params("arbitrary"),
        name="swa",
    )(sinks, q, k_dup, k_dup, k_dup, v_dup, v_dup, v_dup)


def kernel(x, meta_tokens, norm_mix, norm_mlp, w_mlp_up, w_mlp_down, a_w_in, a_w_gate_up, a_b_gate, a_norm_out, a_w_out, kv_norm, w_kv, b_w_q, b_sinks, b_w_out, norm_final):
    batch, seq, d = x.shape
    assert batch == 1 and d == D_MODEL and seq % TM == 0
    assert norm_mix.shape[0] == 2 and a_w_in.shape[0] == 1 and b_w_q.shape[0] == 1

    head = jnp.concatenate([jnp.zeros((META_ROW0, d), x.dtype), meta_tokens.astype(x.dtype)], axis=0)
    x2 = x[0]
    bf = lambda w: w.astype(_BF16)
    n_main = GLA_IN - GLA_RANK
    w_in = bf(a_w_in[0])
    w_r = jnp.pad(w_in[:, n_main:], ((0, 0), (0, RANK_PAD - GLA_RANK)))
    w_gate = jnp.pad(bf(a_w_gate_up[0]), ((0, RANK_PAD - GLA_RANK), (0, 0)))

    proj, r, (w_up, w_down, w_out_a, w_q, w_out_b, w_kv_b) = _in_proj(
        head, x2, norm_mix[0], w_in, n_main, w_r, w_mlp_up, w_mlp_down,
        (a_w_out[0], b_w_q[0], b_w_out[0], w_kv))
    o = _gla(proj, r, w_gate, a_b_gate[0])
    h = _gla_out(o, proj, a_norm_out[0], w_out_a, head, x2)
    h = _mlp(h, norm_mlp[0], w_up, w_down, 0)

    q, k_dup, v_dup = _qkv(h, norm_mix[1], kv_norm, w_q, w_kv_b)
    o = _swa(q, k_dup, v_dup, b_sinks[0])
    h = _matmul_residual(o, w_out_b, h)
    return _mlp(h, norm_mlp[1], w_up, w_down, 1, final_gain=norm_final)[None]
```
